```python
import jax, jax.numpy as jnp
from jax import lax
import numpy as np

D_MODEL = 1024
BATCH = 4
SEQ = 4096
DEPTH = 2

CHUNK = 64
EPS = 1e-6
D_CONV = D_MODEL // 2
N_CONV_GROUPS = 4
CONV_WIDTH = 31
D_POOL = D_MODEL // 2
POOL_WINDOWS = (2, 4, 8, 16)
N_POOL_GROUPS = len(POOL_WINDOWS)
POOL_GROUP = D_POOL // N_POOL_GROUPS
HG_HEADS = 8
HG_DK = D_MODEL // HG_HEADS
HG_DV = HG_DK
D_HG = HG_HEADS * HG_DK
D_FF = -(-8 * D_MODEL // (3 * 256)) * 256
N_EVEN = (DEPTH + 1) // 2
N_ODD = DEPTH // 2

kernel_name = "hybrid_conformer_pool_hgrn2_trunk"


def rmsnorm(x, g):
    xf = x.astype(jnp.float32)
    y = xf * lax.rsqrt(jnp.mean(xf * xf, axis=-1, keepdims=True) + EPS)
    return (y * g.astype(jnp.float32)).astype(x.dtype)


def layernorm(x, g, b):
    xf = x.astype(jnp.float32)
    mu = jnp.mean(xf, axis=-1, keepdims=True)
    var = jnp.mean(jnp.square(xf - mu), axis=-1, keepdims=True)
    y = (xf - mu) * lax.rsqrt(var + EPS)
    return (y * g.astype(jnp.float32) + b.astype(jnp.float32)).astype(x.dtype)


def swiglu_ffn(h, w1, w3, w2):
    return (jax.nn.silu(h @ w1) * (h @ w3)) @ w2


def conv_pool_mixer(h, w_in, dw_w, dw_b, ln_g, ln_b, pool_w, pool_scale, w_out):
    B, T, _ = h.shape
    z = h @ w_in
    a_val = z[..., :D_CONV]
    a_gate = z[..., D_CONV:2 * D_CONV]
    u = z[..., 2 * D_CONV:]

    a = a_val * jax.nn.sigmoid(a_gate)
    a = lax.conv_general_dilated(
        a, dw_w[:, None, :].astype(a.dtype), window_strides=(1,),
        padding=[(CONV_WIDTH - 1, 0)],
        dimension_numbers=('NWC', 'WIO', 'NWC'),
        feature_group_count=D_CONV) + dw_b
    a = jax.nn.silu(layernorm(a, ln_g, ln_b))

    ug = u.reshape(B, T, N_POOL_GROUPS, POOL_GROUP)
    cs = jnp.cumsum(ug.astype(jnp.float32), axis=1)
    cs = jnp.pad(cs, ((0, 0), (1, 0), (0, 0), (0, 0)))
    t = jnp.arange(T)
    means = []
    for g, w in enumerate(POOL_WINDOWS):
        lo = jnp.maximum(t + 1 - w, 0)
        s = cs[:, 1:, g] - cs[:, lo, g]
        cnt = jnp.minimum(t + 1, w).astype(jnp.float32)[:, None]
        means.append(s / cnt)
    pooled = jnp.stack(means, axis=2)
    d = (pooled - ug.astype(jnp.float32)).astype(u.dtype)
    p = jnp.einsum('btgc,gcd->btgd', d, pool_w).reshape(B, T, D_POOL) * pool_scale

    return jnp.concatenate([a, p], axis=-1) @ w_out


def hgrn2_mixer(h, w_in, lb, gn_g, w_out):
    B, T, _ = h.shape
    NC = T // CHUNK
    z = h @ w_in
    q, f_raw, i, g = jnp.split(z, 4, axis=-1)
    f = lb + (1.0 - lb) * jax.nn.sigmoid(f_raw.astype(jnp.float32))
    log_f = jnp.log(f)
    k = 1.0 - f

    def to_chunks(a, dh):
        a = a.astype(jnp.float32).reshape(B, NC, CHUNK, HG_HEADS, dh)
        return a.transpose(1, 0, 3, 2, 4)

    qc = to_chunks(q, HG_DK)
    kc = to_chunks(k, HG_DK)
    vc = to_chunks(i, HG_DV)
    bc = jnp.cumsum(to_chunks(log_f, HG_DK), axis=3)
    tril = jnp.tril(jnp.ones((CHUNK, CHUNK), dtype=bool))

    def step(S, inp):
        qt, kt, vt, bt = inp
        diff = bt[:, :, :, None, :] - bt[:, :, None, :, :]
        decay = jnp.exp(jnp.where(tril[:, :, None], diff, -jnp.inf))
        scores = jnp.einsum('bhtk,bhsk,bhtsk->bhts', qt, kt, decay)
        o = (jnp.einsum('bhts,bhsv->bhtv', scores, vt)
             + jnp.einsum('bhtk,bhkv->bhtv', qt * jnp.exp(bt), S))
        b_last = bt[:, :, -1:, :]
        S = (jnp.exp(b_last[:, :, 0, :])[..., None] * S
             + jnp.einsum('bhsk,bhsv->bhkv', kt * jnp.exp(b_last - bt), vt))
        return S, o

    S0 = jnp.zeros((B, HG_HEADS, HG_DK, HG_DV), jnp.float32)
    _, o = lax.scan(step, S0, (qc, kc, vc, bc))
    o = o.transpose(1, 0, 3, 2, 4).reshape(B, T, HG_HEADS, HG_DV)
    o = o * lax.rsqrt(jnp.mean(o * o, axis=-1, keepdims=True) + EPS)
    o = o * gn_g.astype(jnp.float32).reshape(HG_HEADS, HG_DV)
    o = o.reshape(B, T, D_HG).astype(h.dtype) * jax.nn.silu(g)
    return o @ w_out


def setup_inputs(seed: int = 0) -> dict:
    key = jax.random.key(seed)
    ks = jax.random.split(key, 20)
    nrm = lambda k, shape, s: jax.random.normal(k, shape, jnp.float32) * s
    gain = lambda k, shape: 1.0 + 0.05 * jax.random.normal(k, shape, jnp.float32)
    d_in0 = 2 * D_CONV + D_POOL
    return {
        "x": jax.random.normal(ks[0], (BATCH, SEQ, D_MODEL), jnp.float32),
        "norm_mix_g": gain(ks[1], (DEPTH, D_MODEL)),
        "norm_ffn_g": gain(ks[2], (DEPTH, D_MODEL)),
        "final_g": gain(ks[3], (D_MODEL,)),
        "cp_w_in": nrm(ks[4], (N_EVEN, D_MODEL, d_in0), D_MODEL ** -0.5),
        "cp_dw_w": nrm(ks[5], (N_EVEN, CONV_WIDTH, D_CONV), CONV_WIDTH ** -0.5),
        "cp_dw_b": nrm(ks[6], (N_EVEN, D_CONV), 0.01),
        "cp_ln_g": gain(ks[7], (N_EVEN, D_CONV)),
        "cp_ln_b": nrm(ks[8], (N_EVEN, D_CONV), 0.01),
        "cp_pool_w": nrm(ks[9], (N_EVEN, N_POOL_GROUPS, POOL_GROUP, POOL_GROUP), POOL_GROUP ** -0.5),
        "cp_pool_scale": gain(ks[10], (N_EVEN, D_POOL)),
        "cp_w_out": nrm(ks[11], (N_EVEN, D_CONV + D_POOL, D_MODEL), (D_CONV + D_POOL) ** -0.5),
        "hg_w_in": nrm(ks[12], (N_ODD, D_MODEL, 4 * D_HG), D_MODEL ** -0.5),
        "hg_lb_logits": nrm(ks[13], (DEPTH, D_HG), 0.1),
        "hg_gn_g": gain(ks[14], (N_ODD, D_HG)),
        "hg_w_out": nrm(ks[15], (N_ODD, D_HG, D_MODEL), D_HG ** -0.5),
        "ffn_w1": nrm(ks[16], (DEPTH, D_MODEL, D_FF), D_MODEL ** -0.5),
        "ffn_w3": nrm(ks[17], (DEPTH, D_MODEL, D_FF), D_MODEL ** -0.5),
        "ffn_w2": nrm(ks[18], (DEPTH, D_FF, D_MODEL), D_FF ** -0.5),
    }


def reference(x, norm_mix_g, norm_ffn_g, final_g, cp_w_in, cp_dw_w, cp_dw_b,
              cp_ln_g, cp_ln_b, cp_pool_w, cp_pool_scale, cp_w_out, hg_w_in,
              hg_lb_logits, hg_gn_g, hg_w_out, ffn_w1, ffn_w3, ffn_w2):
    p = jax.nn.softmax(hg_lb_logits.astype(jnp.float32), axis=0)
    lower_bounds = jnp.cumsum(p, axis=0) - p[0:1]
    h = x
    for layer in range(DEPTH):
        hn = rmsnorm(h, norm_mix_g[layer])
        j = layer // 2
        if layer % 2 == 0:
            mix = conv_pool_mixer(hn, cp_w_in[j], cp_dw_w[j], cp_dw_b[j], cp_ln_g[j],
                                  cp_ln_b[j], cp_pool_w[j], cp_pool_scale[j], cp_w_out[j])
        else:
            mix = hgrn2_mixer(hn, hg_w_in[j], lower_bounds[layer], hg_gn_g[j], hg_w_out[j])
        h = h + mix
        h = h + swiglu_ffn(rmsnorm(h, norm_ffn_g[layer]), ffn_w1[layer], ffn_w3[layer], ffn_w2[layer])
    return rmsnorm(h, final_g)
```

```python
import functools

import jax
import jax.numpy as jnp
from jax import lax
from jax.experimental import pallas as pl
from jax.experimental.pallas import tpu as pltpu

F32 = jnp.float32
BF16 = jnp.bfloat16

EPS = 1e-6
CHUNK = 64
CONV_WIDTH = 31
POOL_WINDOWS = (2, 4, 8, 16)
HG_HEADS = 8

V7X_VMEM_BYTES = 64 * 1024 * 1024
SUBLANES = 8
LANES = 128

TOKEN_TILE = 512
CONV_HALO = 32
POOL_HALO = 16
CONV_ROWS = 32


def _vmem_limit(resident_bytes):
    return int(min(resident_bytes * 1.5 + (8 << 20), V7X_VMEM_BYTES - (6 << 20)))


def _rmsnorm(x, g):
    return x * lax.rsqrt(jnp.mean(x * x, axis=-1, keepdims=True) + EPS) * g


def _resident(shape, ngrid):
    zeros = (0,) * len(shape)
    if ngrid == 1:
        index_map = lambda i: zeros
    else:
        index_map = lambda b, t: zeros
    return pl.BlockSpec(shape, index_map, pipeline_mode=pl.Buffered(1))


def _ffn_kernel(h_ref, g_ref, w1_ref, w3_ref, w2_ref, fg_ref, o_ref, *, final_norm):
    h = h_ref[...]
    n = _rmsnorm(h, g_ref[...]).astype(BF16)
    a = jnp.dot(n, w1_ref[...], preferred_element_type=F32)
    b = jnp.dot(n, w3_ref[...], preferred_element_type=F32)
    gated = (a * jax.nn.sigmoid(a) * b).astype(BF16)
    out = h + jnp.dot(gated, w2_ref[...], preferred_element_type=F32)
    if final_norm:
        out = _rmsnorm(out, fg_ref[...])
    o_ref[...] = out


def _ffn_block(h2d, g, w1, w3, w2, final_g, *, final_norm):
    n_tok, d = h2d.shape
    d_ff = w1.shape[1]
    tm = TOKEN_TILE
    resident = 3 * d * d_ff * 2 + 4 * tm * d * 4 + 3 * tm * d_ff * 4
    return pl.pallas_call(
        functools.partial(_ffn_kernel, final_norm=final_norm),
        grid=(n_tok // tm,),
        in_specs=[
            pl.BlockSpec((tm, d), lambda i: (i, 0)),
            _resident((1, d), 1),
            _resident((d, d_ff), 1),
            _resident((d, d_ff), 1),
            _resident((d_ff, d), 1),
            _resident((1, d), 1),
        ],
        out_specs=pl.BlockSpec((tm, d), lambda i: (i, 0)),
        out_shape=jax.ShapeDtypeStruct((n_tok, d), F32),
        compiler_params=pltpu.CompilerParams(
            dimension_semantics=("arbitrary",),
            vmem_limit_bytes=_vmem_limit(resident)),
        name="ffn_block",
    )(h2d, g, w1, w3, w2, final_g)


def _convpool_kernel(x_ref, g_ref, w_in_ref, dw_w_ref, dw_b_ref, ln_g_ref, ln_b_ref,
                     pool_w_ref, pool_s_ref, w_out_ref, o_ref, a_ext, u_ext, conv_buf, *, tm):
    t = pl.program_id(1)
    d_conv = a_ext.shape[1]
    d_pool = u_ext.shape[1]
    pool_group = d_pool // len(POOL_WINDOWS)

    x = x_ref[...]
    n = _rmsnorm(x, g_ref[...]).astype(BF16)
    z = jnp.dot(n, w_in_ref[...], preferred_element_type=F32)

    @pl.when(t == 0)
    def _():
        a_ext[0:CONV_HALO, :] = jnp.zeros((CONV_HALO, d_conv), F32)
        u_ext[0:POOL_HALO, :] = jnp.zeros((POOL_HALO, d_pool), F32)

    @pl.when(t > 0)
    def _():
        a_ext[0:CONV_HALO, :] = a_ext[tm:tm + CONV_HALO, :]
        u_ext[0:POOL_HALO, :] = u_ext[tm:tm + POOL_HALO, :]

    a_ext[CONV_HALO:CONV_HALO + tm, :] = z[:, :d_conv] * jax.nn.sigmoid(z[:, d_conv:2 * d_conv])
    u_ext[POOL_HALO:POOL_HALO + tm, :] = z[:, 2 * d_conv:]

    base = CONV_HALO - (CONV_WIDTH - 1)
    for r0 in range(0, tm, CONV_ROWS):
        acc = jnp.broadcast_to(dw_b_ref[...], (CONV_ROWS, d_conv))
        for k in range(CONV_WIDTH):
            acc = acc + a_ext[r0 + base + k:r0 + base + k + CONV_ROWS, :] * dw_w_ref[k:k + 1, :]
        conv_buf[r0:r0 + CONV_ROWS, :] = acc
    c = conv_buf[...]
    mu = jnp.mean(c, axis=-1, keepdims=True)
    var = jnp.mean(jnp.square(c - mu), axis=-1, keepdims=True)
    c = (c - mu) * lax.rsqrt(var + EPS) * ln_g_ref[...] + ln_b_ref[...]
    a_out = (c * jax.nn.sigmoid(c)).astype(BF16)

    frame = lax.broadcasted_iota(jnp.int32, (tm, 1), 0) + t * tm
    pooled = []
    for gi, w in enumerate(POOL_WINDOWS):
        lanes = slice(gi * pool_group, (gi + 1) * pool_group)
        tok = u_ext[POOL_HALO:POOL_HALO + tm, lanes]
        s = tok
        for j in range(1, w):
            s = s + u_ext[POOL_HALO - j:POOL_HALO - j + tm, lanes]
        cnt = jnp.minimum(frame + 1, w).astype(F32)
        dlt = (s / cnt - tok).astype(BF16)
        pooled.append(jnp.dot(dlt, pool_w_ref[gi], preferred_element_type=F32))
    p_out = (jnp.concatenate(pooled, axis=-1) * pool_s_ref[...]).astype(BF16)

    cat = jnp.concatenate([a_out, p_out], axis=-1)
    o_ref[...] = x + jnp.dot(cat, w_out_ref[...], preferred_element_type=F32)


def _convpool_mixer(x, g, w_in, dw_w, dw_b, ln_g, ln_b, pool_w, pool_s, w_out):
    bsz, seq, d = x.shape
    d_conv = dw_w.shape[1]
    d_pool = pool_s.shape[1]
    tm = TOKEN_TILE
    resident = (w_in.size + w_out.size + pool_w.size) * 2 + 4 * tm * d * 4 \
        + tm * w_in.shape[1] * 4 + (2 * tm + CONV_HALO) * d_conv * 4 + (tm + POOL_HALO) * d_pool * 4
    return pl.pallas_call(
        functools.partial(_convpool_kernel, tm=tm),
        grid=(bsz, seq // tm),
        in_specs=[
            pl.BlockSpec((None, tm, d), lambda b, t: (b, t, 0)),
            _resident((1, d), 2),
            _resident(w_in.shape, 2),
            _resident(dw_w.shape, 2),
            _resident((1, d_conv), 2),
            _resident((1, d_conv), 2),
            _resident((1, d_conv), 2),
            _resident(pool_w.shape, 2),
            _resident((1, d_pool), 2),
            _resident(w_out.shape, 2),
        ],
        out_specs=pl.BlockSpec((None, tm, d), lambda b, t: (b, t, 0)),
        out_shape=jax.ShapeDtypeStruct((bsz, seq, d), F32),
        scratch_shapes=[
            pltpu.VMEM((CONV_HALO + tm, d_conv), F32),
            pltpu.VMEM((POOL_HALO + tm, d_pool), F32),
            pltpu.VMEM((tm, d_conv), F32),
        ],
        compiler_params=pltpu.CompilerParams(
            dimension_semantics=("arbitrary", "arbitrary"),
            vmem_limit_bytes=_vmem_limit(resident)),
        name="convpool_mixer",
    )(x, g, w_in, dw_w, dw_b, ln_g, ln_b, pool_w, pool_s, w_out)


def _hgrn_kernel(h_ref, g_ref, w_in_ref, lb_logits_ref, gn_ref, w_out_ref, o_ref,
                 z_ref, y_ref, state_ref, *, tm, layer):
    t = pl.program_id(1)
    d_hg = y_ref.shape[1]
    dk = d_hg // HG_HEADS

    @pl.when(t == 0)
    def _():
        state_ref[...] = jnp.zeros(state_ref.shape, F32)

    x = h_ref[...]
    n = _rmsnorm(x, g_ref[...]).astype(BF16)
    z_ref[...] = jnp.dot(n, w_in_ref[...], preferred_element_type=F32)

    logits = lb_logits_ref[...]
    e = jnp.exp(logits - jnp.max(logits, axis=0, keepdims=True))
    p = e / jnp.sum(e, axis=0, keepdims=True)
    lb = jnp.sum(p[0:layer + 1], axis=0, keepdims=True) - p[0:1]

    row = lax.broadcasted_iota(jnp.int32, (CHUNK, CHUNK), 0)
    col = lax.broadcasted_iota(jnp.int32, (CHUNK, CHUNK), 1)
    causal = row >= col
    tril = causal.astype(BF16)
    gn = gn_ref[...]
    mid = CHUNK // 2 - 1

    def chunk_step(c, carry):
        r0 = pl.multiple_of(c * CHUNK, CHUNK)
        rows = pl.ds(r0, CHUNK)
        q = z_ref[rows, 0:d_hg]
        f = lb + (1.0 - lb) * jax.nn.sigmoid(z_ref[rows, d_hg:2 * d_hg])
        log_f = jnp.log(f)
        k = 1.0 - f
        v = z_ref[rows, 2 * d_hg:3 * d_hg].astype(BF16)
        gate = z_ref[rows, 3 * d_hg:4 * d_hg]
        gate = gate * jax.nn.sigmoid(gate)

        hi = log_f.astype(BF16)
        rem = log_f - hi.astype(F32)
        md = rem.astype(BF16)
        lo = (rem - md.astype(F32)).astype(BF16)
        b = (jnp.dot(tril, hi, preferred_element_type=F32)
             + jnp.dot(tril, md, preferred_element_type=F32)
             + jnp.dot(tril, lo, preferred_element_type=F32))

        b_mid = b[mid:mid + 1, :]
        b_last = b[CHUNK - 1:CHUNK, :]
        q_rel = q * jnp.exp(b - b_mid)
        k_rel = k * jnp.exp(b_mid - b)
        q_in = (q_rel * jnp.exp(b_mid)).astype(BF16)
        k_up = (k_rel * jnp.exp(b_last - b_mid)).astype(BF16)
        q_rel = q_rel.astype(BF16)
        k_rel = k_rel.astype(BF16)
        decay_last = jnp.exp(b_last)

        for hd in range(HG_HEADS):
            sl = slice(hd * dk, (hd + 1) * dk)
            scores = lax.dot_general(q_rel[:, sl], k_rel[:, sl], (((1,), (1,)), ((), ())),
                                     preferred_element_type=F32)
            scores = jnp.where(causal, scores, 0.0).astype(BF16)
            state_t = state_ref[hd]
            o = (jnp.dot(scores, v[:, sl], preferred_element_type=F32)
                 + lax.dot_general(q_in[:, sl], state_t.astype(BF16), (((1,), (1,)), ((), ())),
                                   preferred_element_type=F32))
            state_ref[hd] = (state_t * decay_last[:, sl]
                             + lax.dot_general(v[:, sl], k_up[:, sl], (((0,), (0,)), ((), ())),
                                               preferred_element_type=F32))
            o = o * lax.rsqrt(jnp.mean(o * o, axis=-1, keepdims=True) + EPS) * gn[:, sl]
            y_ref[rows, sl] = (o * gate[:, sl]).astype(BF16)
        return carry

    lax.fori_loop(0, tm // CHUNK, chunk_step, 0)
    o_ref[...] = x + jnp.dot(y_ref[...], w_out_ref[...], preferred_element_type=F32)


def _hgrn_mixer(h, g, w_in, lb_logits, gn_g, w_out, *, layer):
    bsz, seq, d = h.shape
    d_hg = w_out.shape[0]
    dk = d_hg // HG_HEADS
    tm = TOKEN_TILE
    resident = (w_in.size + w_out.size) * 2 + 4 * tm * d * 4 + tm * 4 * d_hg * 4 * 2 \
        + tm * d_hg * 2 + HG_HEADS * dk * dk * 4
    return pl.pallas_call(
        functools.partial(_hgrn_kernel, tm=tm, layer=layer),
        grid=(bsz, seq // tm),
        in_specs=[
            pl.BlockSpec((None, tm, d), lambda b, t: (b, t, 0)),
            _resident((1, d), 2),
            _resident(w_in.shape, 2),
            _resident(lb_logits.shape, 2),
            _resident((1, d_hg), 2),
            _resident(w_out.shape, 2),
        ],
        out_specs=pl.BlockSpec((None, tm, d), lambda b, t: (b, t, 0)),
        out_shape=jax.ShapeDtypeStruct((bsz, seq, d), F32),
        scratch_shapes=[
            pltpu.VMEM((tm, 4 * d_hg), F32),
            pltpu.VMEM((tm, d_hg), BF16),
            pltpu.VMEM((HG_HEADS, dk, dk), F32),
        ],
        compiler_params=pltpu.CompilerParams(
            dimension_semantics=("arbitrary", "arbitrary"),
            vmem_limit_bytes=_vmem_limit(resident)),
        name="hgrn_mixer",
    )(h, g, w_in, lb_logits, gn_g, w_out)


def kernel(x, norm_mix_g, norm_ffn_g, final_g, cp_w_in, cp_dw_w, cp_dw_b, cp_ln_g, cp_ln_b, cp_pool_w,
           cp_pool_scale, cp_w_out, hg_w_in, hg_lb_logits, hg_gn_g, hg_w_out, ffn_w1, ffn_w3, ffn_w2):
    bsz, seq, d = x.shape
    depth = norm_mix_g.shape[0]
    row = lambda a: a.reshape(1, -1).astype(F32)
    h = x
    for layer in range(depth):
        j = layer // 2
        if layer % 2 == 0:
            h = _convpool_mixer(
                h, row(norm_mix_g[layer]), cp_w_in[j].astype(BF16), cp_dw_w[j], row(cp_dw_b[j]),
                row(cp_ln_g[j]), row(cp_ln_b[j]), cp_pool_w[j].astype(BF16), row(cp_pool_scale[j]),
                cp_w_out[j].astype(BF16))
        else:
            h = _hgrn_mixer(
                h, row(norm_mix_g[layer]), hg_w_in[j].astype(BF16), hg_lb_logits.astype(F32),
                row(hg_gn_g[j]), hg_w_out[j].astype(BF16), layer=layer)
        last = layer == depth - 1
        h = _ffn_block(
            h.reshape(bsz * seq, d), row(norm_ffn_g[layer]), ffn_w1[layer].astype(BF16),
            ffn_w3[layer].astype(BF16), ffn_w2[layer].astype(BF16), row(final_g),
            final_norm=last).reshape(bsz, seq, d)
    return h
```

```python
import functools

import jax
import jax.numpy as jnp
from jax import lax
from jax.experimental import pallas as pl
from jax.experimental.pallas import tpu as pltpu

F32 = jnp.float32
BF16 = jnp.bfloat16

EPS = 1e-6
CHUNK = 64
CONV_WIDTH = 31
POOL_WINDOWS = (2, 4, 8, 16)
HG_HEADS = 8

V7X_VMEM_BYTES = 64 * 1024 * 1024
SUBLANES = 8
LANES = 128

TOKEN_TILE = 512
CONV_HALO = 32
POOL_HALO = 16
TIME_STRIDE = 4
TIME_BLOCK = SUBLANES * TIME_STRIDE
CONV_GROUP = 4


def _interleaved(start):
    return pl.ds(start, SUBLANES, stride=TIME_STRIDE)


def _vmem_limit(resident_bytes):
    return int(min(resident_bytes * 1.5 + (8 << 20), V7X_VMEM_BYTES - (6 << 20)))


def _rmsnorm(x, g):
    return x * lax.rsqrt(jnp.mean(x * x, axis=-1, keepdims=True) + EPS) * g


def _resident(shape, ngrid):
    zeros = (0,) * len(shape)
    if ngrid == 1:
        index_map = lambda i: zeros
    else:
        index_map = lambda b, t: zeros
    return pl.BlockSpec(shape, index_map, pipeline_mode=pl.Buffered(1))


def _ffn_kernel(h_ref, g_ref, w1_ref, w3_ref, w2_ref, fg_ref, o_ref, *, final_norm):
    h = h_ref[...]
    n = _rmsnorm(h, g_ref[...]).astype(BF16)
    a = jnp.dot(n, w1_ref[...], preferred_element_type=F32)
    b = jnp.dot(n, w3_ref[...], preferred_element_type=F32)
    gated = (a * jax.nn.sigmoid(a) * b).astype(BF16)
    out = h + jnp.dot(gated, w2_ref[...], preferred_element_type=F32)
    if final_norm:
        out = _rmsnorm(out, fg_ref[...])
    o_ref[...] = out


def _ffn_block(h2d, g, w1, w3, w2, final_g, *, final_norm):
    n_tok, d = h2d.shape
    d_ff = w1.shape[1]
    tm = TOKEN_TILE
    resident = 3 * d * d_ff * 2 + 4 * tm * d * 4 + 3 * tm * d_ff * 4
    return pl.pallas_call(
        functools.partial(_ffn_kernel, final_norm=final_norm),
        grid=(n_tok // tm,),
        in_specs=[
            pl.BlockSpec((tm, d), lambda i: (i, 0)),
            _resident((1, d), 1),
            _resident((d, d_ff), 1),
            _resident((d, d_ff), 1),
            _resident((d_ff, d), 1),
            _resident((1, d), 1),
        ],
        out_specs=pl.BlockSpec((tm, d), lambda i: (i, 0)),
        out_shape=jax.ShapeDtypeStruct((n_tok, d), F32),
        compiler_params=pltpu.CompilerParams(
            dimension_semantics=("arbitrary",),
            vmem_limit_bytes=_vmem_limit(resident)),
        name="ffn_block",
    )(h2d, g, w1, w3, w2, final_g)


def _convpool_kernel(x_ref, g_ref, w_in_ref, dw_w_ref, dw_b_ref, ln_g_ref, ln_b_ref,
                     pool_w_ref, pool_s_ref, w_out_ref, o_ref, a_ext, u_ext, conv_buf, dlt_buf, *, tm):
    t = pl.program_id(1)
    n_conv = a_ext.shape[0]
    n_pool = u_ext.shape[0]
    d_conv = n_conv * LANES

    x = x_ref[...]
    n = _rmsnorm(x, g_ref[...]).astype(BF16)
    z = jnp.dot(n, w_in_ref[...], preferred_element_type=F32)

    @pl.when(t == 0)
    def _():
        a_ext[:, 0:CONV_HALO, :] = jnp.zeros((n_conv, CONV_HALO, LANES), F32)
        u_ext[:, 0:POOL_HALO, :] = jnp.zeros((n_pool, POOL_HALO, LANES), F32)

    @pl.when(t > 0)
    def _():
        a_ext[:, 0:CONV_HALO, :] = a_ext[:, tm:tm + CONV_HALO, :]
        u_ext[:, 0:POOL_HALO, :] = u_ext[:, tm:tm + POOL_HALO, :]

    glu = z[:, :d_conv] * jax.nn.sigmoid(z[:, d_conv:2 * d_conv])
    for s in range(n_conv):
        a_ext[s, CONV_HALO:CONV_HALO + tm, :] = glu[:, s * LANES:(s + 1) * LANES]
    for s in range(n_pool):
        u_ext[s, POOL_HALO:POOL_HALO + tm, :] = z[:, 2 * d_conv + s * LANES:2 * d_conv + (s + 1) * LANES]

    base = CONV_HALO - (CONV_WIDTH - 1)
    for s in range(n_conv):
        lanes = slice(s * LANES, (s + 1) * LANES)
        bias = jnp.broadcast_to(dw_b_ref[:, lanes], (SUBLANES, LANES))
        for g0 in range(0, tm, CONV_GROUP * TIME_BLOCK):
            starts = [g0 + bi * TIME_BLOCK + m for bi in range(CONV_GROUP) for m in range(TIME_STRIDE)]
            accs = [bias] * len(starts)
            loaded = {}
            for k in range(CONV_WIDTH):
                wk = jnp.broadcast_to(dw_w_ref[k:k + 1, lanes], (SUBLANES, LANES))
                for i, r in enumerate(starts):
                    src = r + base + k
                    if src not in loaded:
                        loaded[src] = a_ext[s, _interleaved(src), :]
                    accs[i] = accs[i] + loaded[src] * wk
            for i, r in enumerate(starts):
                conv_buf[s, _interleaved(r), :] = accs[i]

    cs = [conv_buf[s] for s in range(n_conv)]
    mu = jnp.sum(functools.reduce(jnp.add, cs), axis=-1, keepdims=True) / d_conv
    devs = [c - mu for c in cs]
    var = jnp.sum(functools.reduce(jnp.add, [dv * dv for dv in devs]), axis=-1, keepdims=True) / d_conv
    inv = lax.rsqrt(var + EPS)
    a_out = []
    for s in range(n_conv):
        lanes = slice(s * LANES, (s + 1) * LANES)
        c = devs[s] * inv * ln_g_ref[:, lanes] + ln_b_ref[:, lanes]
        a_out.append((c * jax.nn.sigmoid(c)).astype(BF16))

    pooled = []
    for gi, w in enumerate(POOL_WINDOWS):
        for t0 in range(0, tm, TIME_BLOCK):
            loaded = {}
            for m in range(TIME_STRIDE):
                for j in range(w):
                    src = t0 + POOL_HALO + m - j
                    if src not in loaded:
                        loaded[src] = u_ext[gi, _interleaved(src), :]
                tok = loaded[t0 + POOL_HALO + m]
                tot = functools.reduce(jnp.add, [loaded[t0 + POOL_HALO + m - j] for j in range(w)])
                if t0 < w - 1:
                    frame = t * tm + t0 + m + TIME_STRIDE * lax.broadcasted_iota(jnp.int32, (SUBLANES, LANES), 0)
                    mean = tot / jnp.minimum(frame + 1, w).astype(F32)
                elif w & (w - 1) == 0:
                    mean = tot * (1.0 / w)
                else:
                    mean = tot / float(w)
                dlt_buf[gi, _interleaved(t0 + m), :] = mean - tok
        pooled.append(jnp.dot(dlt_buf[gi].astype(BF16), pool_w_ref[gi], preferred_element_type=F32))
    p_out = (jnp.concatenate(pooled, axis=-1) * pool_s_ref[...]).astype(BF16)

    cat = jnp.concatenate(a_out + [p_out], axis=-1)
    o_ref[...] = x + jnp.dot(cat, w_out_ref[...], preferred_element_type=F32)


def _convpool_mixer(x, g, w_in, dw_w, dw_b, ln_g, ln_b, pool_w, pool_s, w_out):
    bsz, seq, d = x.shape
    d_conv = dw_w.shape[1]
    d_pool = pool_s.shape[1]
    tm = TOKEN_TILE
    resident = (w_in.size + w_out.size + pool_w.size) * 2 + 4 * tm * d * 4 \
        + tm * w_in.shape[1] * 4 + (2 * tm + CONV_HALO) * d_conv * 4 + (2 * tm + POOL_HALO) * d_pool * 4
    assert d_pool == len(POOL_WINDOWS) * LANES and d_conv % LANES == 0
    assert tm % (CONV_GROUP * TIME_BLOCK) == 0 and seq % tm == 0
    return pl.pallas_call(
        functools.partial(_convpool_kernel, tm=tm),
        grid=(bsz, seq // tm),
        in_specs=[
            pl.BlockSpec((None, tm, d), lambda b, t: (b, t, 0)),
            _resident((1, d), 2),
            _resident(w_in.shape, 2),
            _resident(dw_w.shape, 2),
            _resident((1, d_conv), 2),
            _resident((1, d_conv), 2),
            _resident((1, d_conv), 2),
            _resident(pool_w.shape, 2),
            _resident((1, d_pool), 2),
            _resident(w_out.shape, 2),
        ],
        out_specs=pl.BlockSpec((None, tm, d), lambda b, t: (b, t, 0)),
        out_shape=jax.ShapeDtypeStruct((bsz, seq, d), F32),
        scratch_shapes=[
            pltpu.VMEM((d_conv // LANES, CONV_HALO + tm, LANES), F32),
            pltpu.VMEM((d_pool // LANES, POOL_HALO + tm, LANES), F32),
            pltpu.VMEM((d_conv // LANES, tm, LANES), F32),
            pltpu.VMEM((d_pool // LANES, tm, LANES), F32),
        ],
        compiler_params=pltpu.CompilerParams(
            dimension_semantics=("arbitrary", "arbitrary"),
            vmem_limit_bytes=_vmem_limit(resident)),
        name="convpool_mixer",
    )(x, g, w_in, dw_w, dw_b, ln_g, ln_b, pool_w, pool_s, w_out)


def _hgrn_kernel(h_ref, g_ref, w_in_ref, lb_logits_ref, gn_ref, w_out_ref, o_ref,
                 z_ref, y_ref, state_ref, *, tm, layer):
    t = pl.program_id(1)
    d_hg = y_ref.shape[1]
    dk = d_hg // HG_HEADS

    @pl.when(t == 0)
    def _():
        state_ref[...] = jnp.zeros(state_ref.shape, F32)

    x = h_ref[...]
    n = _rmsnorm(x, g_ref[...]).astype(BF16)
    z_ref[...] = jnp.dot(n, w_in_ref[...], preferred_element_type=F32)

    logits = lb_logits_ref[...]
    e = jnp.exp(logits - jnp.max(logits, axis=0, keepdims=True))
    p = e / jnp.sum(e, axis=0, keepdims=True)
    lb = jnp.sum(p[0:layer + 1], axis=0, keepdims=True) - p[0:1]

    row = lax.broadcasted_iota(jnp.int32, (CHUNK, CHUNK), 0)
    col = lax.broadcasted_iota(jnp.int32, (CHUNK, CHUNK), 1)
    causal = row >= col
    tril = causal.astype(BF16)
    gn = gn_ref[...]
    mid = CHUNK // 2 - 1

    def chunk_step(c, carry):
        r0 = pl.multiple_of(c * CHUNK, CHUNK)
        rows = pl.ds(r0, CHUNK)
        q = z_ref[rows, 0:d_hg]
        f = lb + (1.0 - lb) * jax.nn.sigmoid(z_ref[rows, d_hg:2 * d_hg])
        log_f = jnp.log(f)
        k = 1.0 - f
        v = z_ref[rows, 2 * d_hg:3 * d_hg].astype(BF16)
        gate = z_ref[rows, 3 * d_hg:4 * d_hg]
        gate = gate * jax.nn.sigmoid(gate)

        hi = log_f.astype(BF16)
        rem = log_f - hi.astype(F32)
        md = rem.astype(BF16)
        lo = (rem - md.astype(F32)).astype(BF16)
        b = (jnp.dot(tril, hi, preferred_element_type=F32)
             + jnp.dot(tril, md, preferred_element_type=F32)
             + jnp.dot(tril, lo, preferred_element_type=F32))

        b_mid = b[mid:mid + 1, :]
        b_last = b[CHUNK - 1:CHUNK, :]
        q_rel = q * jnp.exp(b - b_mid)
        k_rel = k * jnp.exp(b_mid - b)
        q_in = (q_rel * jnp.exp(b_mid)).astype(BF16)
        k_up = (k_rel * jnp.exp(b_last - b_mid)).astype(BF16)
        q_rel = q_rel.astype(BF16)
        k_rel = k_rel.astype(BF16)
        decay_last = jnp.exp(b_last)

        for hd in range(HG_HEADS):
            sl = slice(hd * dk, (hd + 1) * dk)
            scores = lax.dot_general(q_rel[:, sl], k_rel[:, sl], (((1,), (1,)), ((), ())),
                                     preferred_element_type=F32)
            scores = jnp.where(causal, scores, 0.0).astype(BF16)
            state_t = state_ref[hd]
            o = (jnp.dot(scores, v[:, sl], preferred_element_type=F32)
                 + lax.dot_general(q_in[:, sl], state_t.astype(BF16), (((1,), (1,)), ((), ())),
                                   preferred_element_type=F32))
            state_ref[hd] = (state_t * decay_last[:, sl]
                             + lax.dot_general(v[:, sl], k_up[:, sl], (((0,), (0,)), ((), ())),
                                               preferred_element_type=F32))
            o = o * lax.rsqrt(jnp.mean(o * o, axis=-1, keepdims=True) + EPS) * gn[:, sl]
            y_ref[rows, sl] = (o * gate[:, sl]).astype(BF16)
        return carry

    lax.fori_loop(0, tm // CHUNK, chunk_step, 0)
    o_ref[...] = x + jnp.dot(y_ref[...], w_out_ref[...], preferred_element_type=F32)


def _hgrn_mixer(h, g, w_in, lb_logits, gn_g, w_out, *, layer):
    bsz, seq, d = h.shape
    d_hg = w_out.shape[0]
    dk = d_hg // HG_HEADS
    tm = TOKEN_TILE
    resident = (w_in.size + w_out.size) * 2 + 4 * tm * d * 4 + tm * 4 * d_hg * 4 * 2 \
        + tm * d_hg * 2 + HG_HEADS * dk * dk * 4
    return pl.pallas_call(
        functools.partial(_hgrn_kernel, tm=tm, layer=layer),
        grid=(bsz, seq // tm),
        in_specs=[
            pl.BlockSpec((None, tm, d), lambda b, t: (b, t, 0)),
            _resident((1, d), 2),
            _resident(w_in.shape, 2),
            _resident(lb_logits.shape, 2),
            _resident((1, d_hg), 2),
            _resident(w_out.shape, 2),
        ],
        out_specs=pl.BlockSpec((None, tm, d), lambda b, t: (b, t, 0)),
        out_shape=jax.ShapeDtypeStruct((bsz, seq, d), F32),
        scratch_shapes=[
            pltpu.VMEM((tm, 4 * d_hg), F32),
            pltpu.VMEM((tm, d_hg), BF16),
            pltpu.VMEM((HG_HEADS, dk, dk), F32),
        ],
        compiler_params=pltpu.CompilerParams(
            dimension_semantics=("arbitrary", "arbitrary"),
            vmem_limit_bytes=_vmem_limit(resident)),
        name="hgrn_mixer",
    )(h, g, w_in, lb_logits, gn_g, w_out)


def kernel(x, norm_mix_g, norm_ffn_g, final_g, cp_w_in, cp_dw_w, cp_dw_b, cp_ln_g, cp_ln_b, cp_pool_w,
           cp_pool_scale, cp_w_out, hg_w_in, hg_lb_logits, hg_gn_g, hg_w_out, ffn_w1, ffn_w3, ffn_w2):
    bsz, seq, d = x.shape
    depth = norm_mix_g.shape[0]
    row = lambda a: a.reshape(1, -1).astype(F32)
    h = x
    for layer in range(depth):
        j = layer // 2
        if layer % 2 == 0:
            h = _convpool_mixer(
                h, row(norm_mix_g[layer]), cp_w_in[j].astype(BF16), cp_dw_w[j], row(cp_dw_b[j]),
                row(cp_ln_g[j]), row(cp_ln_b[j]), cp_pool_w[j].astype(BF16), row(cp_pool_scale[j]),
                cp_w_out[j].astype(BF16))
        else:
            h = _hgrn_mixer(
                h, row(norm_mix_g[layer]), hg_w_in[j].astype(BF16), hg_lb_logits.astype(F32),
                row(hg_gn_g[j]), hg_w_out[j].astype(BF16), layer=layer)
        last = layer == depth - 1
        h = _ffn_block(
            h.reshape(bsz * seq, d), row(norm_ffn_g[layer]), ffn_w1[layer].astype(BF16),
            ffn_w3[layer].astype(BF16), ffn_w2[layer].astype(BF16), row(final_g),
            final_norm=last).reshape(bsz, seq, d)
    return h
```

```python
import functools

import jax
import jax.numpy as jnp
from jax import lax
from jax.experimental import pallas as pl
from jax.experimental.pallas import tpu as pltpu

F32 = jnp.float32
BF16 = jnp.bfloat16

EPS = 1e-6
CHUNK = 64
CONV_WIDTH = 31
POOL_WINDOWS = (2, 4, 8, 16)
HG_HEADS = 8

V7X_VMEM_BYTES = 64 * 1024 * 1024
SUBLANES = 8
LANES = 128

TOKEN_TILE = 512
CONV_HALO = 32
POOL_HALO = 16
TIME_STRIDE = 4
TIME_BLOCK = SUBLANES * TIME_STRIDE
CONV_GROUP = 4


def _interleaved(start):
    return pl.ds(start, SUBLANES, stride=TIME_STRIDE)


def _vmem_limit(resident_bytes):
    return int(min(resident_bytes * 1.5 + (8 << 20), V7X_VMEM_BYTES - (6 << 20)))


def _rmsnorm(x, g):
    return x * lax.rsqrt(jnp.mean(x * x, axis=-1, keepdims=True) + EPS) * g


def _resident(shape, ngrid):
    zeros = (0,) * len(shape)
    if ngrid == 1:
        index_map = lambda i: zeros
    else:
        index_map = lambda b, t: zeros
    return pl.BlockSpec(shape, index_map, pipeline_mode=pl.Buffered(1))


def _ffn_kernel(h_ref, g_ref, w1_ref, w3_ref, w2_ref, fg_ref, o_ref, *, final_norm):
    h = h_ref[...]
    n = _rmsnorm(h, g_ref[...]).astype(BF16)
    a = jnp.dot(n, w1_ref[...], preferred_element_type=F32)
    b = jnp.dot(n, w3_ref[...], preferred_element_type=F32)
    gated = (a * jax.nn.sigmoid(a) * b).astype(BF16)
    out = h + jnp.dot(gated, w2_ref[...], preferred_element_type=F32)
    if final_norm:
        out = _rmsnorm(out, fg_ref[...])
    o_ref[...] = out


def _ffn_block(h2d, g, w1, w3, w2, final_g, *, final_norm):
    n_tok, d = h2d.shape
    d_ff = w1.shape[1]
    tm = TOKEN_TILE
    resident = 3 * d * d_ff * 2 + 4 * tm * d * 4 + 3 * tm * d_ff * 4
    return pl.pallas_call(
        functools.partial(_ffn_kernel, final_norm=final_norm),
        grid=(n_tok // tm,),
        in_specs=[
            pl.BlockSpec((tm, d), lambda i: (i, 0)),
            _resident((1, d), 1),
            _resident((d, d_ff), 1),
            _resident((d, d_ff), 1),
            _resident((d_ff, d), 1),
            _resident((1, d), 1),
        ],
        out_specs=pl.BlockSpec((tm, d), lambda i: (i, 0)),
        out_shape=jax.ShapeDtypeStruct((n_tok, d), F32),
        compiler_params=pltpu.CompilerParams(
            dimension_semantics=("arbitrary",),
            vmem_limit_bytes=_vmem_limit(resident)),
        name="ffn_block",
    )(h2d, g, w1, w3, w2, final_g)


def _convpool_kernel(x_ref, g_ref, w_in_ref, dw_w_ref, dw_b_ref, ln_g_ref, ln_b_ref,
                     pool_w_ref, pool_s_ref, w_out_ref, o_ref, a_ext, u_ext, conv_buf, dlt_buf, *, tm):
    t = pl.program_id(1)
    n_conv = a_ext.shape[0]
    n_pool = u_ext.shape[0]
    d_conv = n_conv * LANES

    x = x_ref[...]
    n = _rmsnorm(x, g_ref[...]).astype(BF16)
    z = jnp.dot(n, w_in_ref[...], preferred_element_type=F32)

    @pl.when(t == 0)
    def _():
        a_ext[:, 0:CONV_HALO, :] = jnp.zeros((n_conv, CONV_HALO, LANES), F32)
        u_ext[:, 0:POOL_HALO, :] = jnp.zeros((n_pool, POOL_HALO, LANES), F32)

    @pl.when(t > 0)
    def _():
        a_ext[:, 0:CONV_HALO, :] = a_ext[:, tm:tm + CONV_HALO, :]
        u_ext[:, 0:POOL_HALO, :] = u_ext[:, tm:tm + POOL_HALO, :]

    glu = z[:, :d_conv] * jax.nn.sigmoid(z[:, d_conv:2 * d_conv])
    for s in range(n_conv):
        a_ext[s, CONV_HALO:CONV_HALO + tm, :] = glu[:, s * LANES:(s + 1) * LANES]
    for s in range(n_pool):
        u_ext[s, POOL_HALO:POOL_HALO + tm, :] = z[:, 2 * d_conv + s * LANES:2 * d_conv + (s + 1) * LANES]

    base = CONV_HALO - (CONV_WIDTH - 1)
    for s in range(n_conv):
        lanes = slice(s * LANES, (s + 1) * LANES)
        bias = jnp.broadcast_to(dw_b_ref[:, lanes], (SUBLANES, LANES))
        for g0 in range(0, tm, CONV_GROUP * TIME_BLOCK):
            starts = [g0 + bi * TIME_BLOCK + m for bi in range(CONV_GROUP) for m in range(TIME_STRIDE)]
            accs = [bias] * len(starts)
            loaded = {}
            for k in range(CONV_WIDTH):
                wk = jnp.broadcast_to(dw_w_ref[k:k + 1, lanes], (SUBLANES, LANES))
                for i, r in enumerate(starts):
                    src = r + base + k
                    if src not in loaded:
                        loaded[src] = a_ext[s, _interleaved(src), :]
                    accs[i] = accs[i] + loaded[src] * wk
            for i, r in enumerate(starts):
                conv_buf[s, _interleaved(r), :] = accs[i]

    cs = [conv_buf[s] for s in range(n_conv)]
    mu = jnp.sum(functools.reduce(jnp.add, cs), axis=-1, keepdims=True) / d_conv
    devs = [c - mu for c in cs]
    var = jnp.sum(functools.reduce(jnp.add, [dv * dv for dv in devs]), axis=-1, keepdims=True) / d_conv
    inv = lax.rsqrt(var + EPS)
    a_out = []
    for s in range(n_conv):
        lanes = slice(s * LANES, (s + 1) * LANES)
        c = devs[s] * inv * ln_g_ref[:, lanes] + ln_b_ref[:, lanes]
        a_out.append((c * jax.nn.sigmoid(c)).astype(BF16))

    pooled = []
    for gi, w in enumerate(POOL_WINDOWS):
        for t0 in range(0, tm, TIME_BLOCK):
            loaded = {}
            for m in range(TIME_STRIDE):
                for j in range(w):
                    src = t0 + POOL_HALO + m - j
                    if src not in loaded:
                        loaded[src] = u_ext[gi, _interleaved(src), :]
                tok = loaded[t0 + POOL_HALO + m]
                tot = functools.reduce(jnp.add, [loaded[t0 + POOL_HALO + m - j] for j in range(w)])
                if t0 < w - 1:
                    frame = t * tm + t0 + m + TIME_STRIDE * lax.broadcasted_iota(jnp.int32, (SUBLANES, LANES), 0)
                    mean = tot / jnp.minimum(frame + 1, w).astype(F32)
                elif w & (w - 1) == 0:
                    mean = tot * (1.0 / w)
                else:
                    mean = tot / float(w)
                dlt_buf[gi, _interleaved(t0 + m), :] = mean - tok
        pooled.append(jnp.dot(dlt_buf[gi].astype(BF16), pool_w_ref[gi], preferred_element_type=F32))
    p_out = (jnp.concatenate(pooled, axis=-1) * pool_s_ref[...]).astype(BF16)

    cat = jnp.concatenate(a_out + [p_out], axis=-1)
    o_ref[...] = x + jnp.dot(cat, w_out_ref[...], preferred_element_type=F32)


def _convpool_mixer(x, g, w_in, dw_w, dw_b, ln_g, ln_b, pool_w, pool_s, w_out):
    bsz, seq, d = x.shape
    d_conv = dw_w.shape[1]
    d_pool = pool_s.shape[1]
    tm = TOKEN_TILE
    resident = (w_in.size + w_out.size + pool_w.size) * 2 + 4 * tm * d * 4 \
        + tm * w_in.shape[1] * 4 + (2 * tm + CONV_HALO) * d_conv * 4 + (2 * tm + POOL_HALO) * d_pool * 4
    assert d_pool == len(POOL_WINDOWS) * LANES and d_conv % LANES == 0
    assert tm % (CONV_GROUP * TIME_BLOCK) == 0 and seq % tm == 0
    return pl.pallas_call(
        functools.partial(_convpool_kernel, tm=tm),
        grid=(bsz, seq // tm),
        in_specs=[
            pl.BlockSpec((None, tm, d), lambda b, t: (b, t, 0)),
            _resident((1, d), 2),
            _resident(w_in.shape, 2),
            _resident(dw_w.shape, 2),
            _resident((1, d_conv), 2),
            _resident((1, d_conv), 2),
            _resident((1, d_conv), 2),
            _resident(pool_w.shape, 2),
            _resident((1, d_pool), 2),
            _resident(w_out.shape, 2),
        ],
        out_specs=pl.BlockSpec((None, tm, d), lambda b, t: (b, t, 0)),
        out_shape=jax.ShapeDtypeStruct((bsz, seq, d), F32),
        scratch_shapes=[
            pltpu.VMEM((d_conv // LANES, CONV_HALO + tm, LANES), F32),
            pltpu.VMEM((d_pool // LANES, POOL_HALO + tm, LANES), F32),
            pltpu.VMEM((d_conv // LANES, tm, LANES), F32),
            pltpu.VMEM((d_pool // LANES, tm, LANES), F32),
        ],
        compiler_params=pltpu.CompilerParams(
            dimension_semantics=("arbitrary", "arbitrary"),
            vmem_limit_bytes=_vmem_limit(resident)),
        name="convpool_mixer",
    )(x, g, w_in, dw_w, dw_b, ln_g, ln_b, pool_w, pool_s, w_out)


def _hgrn_kernel(h_ref, g_ref, w_in_ref, lb_logits_ref, gn_ref, w_out_ref, o_ref,
                 z_ref, y_ref, state_ref, *, tm, layer):
    t = pl.program_id(1)
    d_hg = y_ref.shape[1]
    dk = d_hg // HG_HEADS

    @pl.when(t == 0)
    def _():
        state_ref[...] = jnp.zeros(state_ref.shape, F32)

    x = h_ref[...]
    n = _rmsnorm(x, g_ref[...]).astype(BF16)
    z_ref[...] = jnp.dot(n, w_in_ref[...], preferred_element_type=F32)

    logits = lb_logits_ref[...]
    e = jnp.exp(logits - jnp.max(logits, axis=0, keepdims=True))
    p = e / jnp.sum(e, axis=0, keepdims=True)
    lb = jnp.sum(p[0:layer + 1], axis=0, keepdims=True) - p[0:1]

    row = lax.broadcasted_iota(jnp.int32, (CHUNK, CHUNK), 0)
    col = lax.broadcasted_iota(jnp.int32, (CHUNK, CHUNK), 1)
    causal = row >= col
    row3 = lax.broadcasted_iota(jnp.int32, (CHUNK, 3 * CHUNK), 0)
    col3 = lax.broadcasted_iota(jnp.int32, (CHUNK, 3 * CHUNK), 1) & (CHUNK - 1)
    tril3 = (row3 >= col3).astype(BF16)
    gn = gn_ref[...]
    mid = CHUNK // 2 - 1

    def chunk_step(c, carry):
        r0 = pl.multiple_of(c * CHUNK, CHUNK)
        rows = pl.ds(r0, CHUNK)
        q = z_ref[rows, 0:d_hg]
        f = lb + (1.0 - lb) * jax.nn.sigmoid(z_ref[rows, d_hg:2 * d_hg])
        log_f = jnp.log(f)
        k = 1.0 - f
        v = z_ref[rows, 2 * d_hg:3 * d_hg].astype(BF16)
        gate = z_ref[rows, 3 * d_hg:4 * d_hg]
        gate = gate * jax.nn.sigmoid(gate)

        hi = log_f.astype(BF16)
        rem = log_f - hi.astype(F32)
        md = rem.astype(BF16)
        lo = (rem - md.astype(F32)).astype(BF16)
        b = jnp.dot(tril3, jnp.concatenate([hi, md, lo], axis=0),
                    preferred_element_type=F32)

        b_mid = b[mid:mid + 1, :]
        b_last = b[CHUNK - 1:CHUNK, :]
        q_rel = q * jnp.exp(b - b_mid)
        k_rel = k * jnp.exp(b_mid - b)
        q_in = (q_rel * jnp.exp(b_mid)).astype(BF16)
        k_up = (k_rel * jnp.exp(b_last - b_mid)).astype(BF16)
        q_rel = q_rel.astype(BF16)
        k_rel = k_rel.astype(BF16)
        decay_last = jnp.exp(b_last)

        heads = [slice(hd * dk, (hd + 1) * dk) for hd in range(HG_HEADS)]
        nt = (((1,), (1,)), ((), ()))
        tn = (((0,), (0,)), ((), ()))
        scores = [lax.dot_general(q_rel[:, sl], k_rel[:, sl], nt, preferred_element_type=F32)
                  for sl in heads]
        state_t = [state_ref[hd] for hd in range(HG_HEADS)]
        o_inter = [lax.dot_general(q_in[:, sl], st.astype(BF16), nt, preferred_element_type=F32)
                   for sl, st in zip(heads, state_t)]
        kv = [lax.dot_general(v[:, sl], k_up[:, sl], tn, preferred_element_type=F32) for sl in heads]
        for hd, sl in enumerate(heads):
            state_ref[hd] = state_t[hd] * decay_last[:, sl] + kv[hd]
        scores = [jnp.where(causal, s, 0.0).astype(BF16) for s in scores]
        outs = [jnp.dot(s, v[:, sl], preferred_element_type=F32) + oi
                for s, sl, oi in zip(scores, heads, o_inter)]
        for o, sl in zip(outs, heads):
            o = o * lax.rsqrt(jnp.mean(o * o, axis=-1, keepdims=True) + EPS) * gn[:, sl]
            y_ref[rows, sl] = (o * gate[:, sl]).astype(BF16)
        return carry

    lax.fori_loop(0, tm // CHUNK, chunk_step, 0, unroll=True)
    o_ref[...] = x + jnp.dot(y_ref[...], w_out_ref[...], preferred_element_type=F32)


def _hgrn_mixer(h, g, w_in, lb_logits, gn_g, w_out, *, layer):
    bsz, seq, d = h.shape
    d_hg = w_out.shape[0]
    dk = d_hg // HG_HEADS
    tm = TOKEN_TILE
    resident = (w_in.size + w_out.size) * 2 + 4 * tm * d * 4 + tm * 4 * d_hg * 4 * 2 \
        + tm * d_hg * 2 + HG_HEADS * dk * dk * 4
    return pl.pallas_call(
        functools.partial(_hgrn_kernel, tm=tm, layer=layer),
        grid=(bsz, seq // tm),
        in_specs=[
            pl.BlockSpec((None, tm, d), lambda b, t: (b, t, 0)),
            _resident((1, d), 2),
            _resident(w_in.shape, 2),
            _resident(lb_logits.shape, 2),
            _resident((1, d_hg), 2),
            _resident(w_out.shape, 2),
        ],
        out_specs=pl.BlockSpec((None, tm, d), lambda b, t: (b, t, 0)),
        out_shape=jax.ShapeDtypeStruct((bsz, seq, d), F32),
        scratch_shapes=[
            pltpu.VMEM((tm, 4 * d_hg), F32),
            pltpu.VMEM((tm, d_hg), BF16),
            pltpu.VMEM((HG_HEADS, dk, dk), F32),
        ],
        compiler_params=pltpu.CompilerParams(
            dimension_semantics=("arbitrary", "arbitrary"),
            vmem_limit_bytes=_vmem_limit(resident)),
        name="hgrn_mixer",
    )(h, g, w_in, lb_logits, gn_g, w_out)


def kernel(x, norm_mix_g, norm_ffn_g, final_g, cp_w_in, cp_dw_w, cp_dw_b, cp_ln_g, cp_ln_b, cp_pool_w,
           cp_pool_scale, cp_w_out, hg_w_in, hg_lb_logits, hg_gn_g, hg_w_out, ffn_w1, ffn_w3, ffn_w2):
    bsz, seq, d = x.shape
    depth = norm_mix_g.shape[0]
    row = lambda a: a.reshape(1, -1).astype(F32)
    h = x
    for layer in range(depth):
        j = layer // 2
        if layer % 2 == 0:
            h = _convpool_mixer(
                h, row(norm_mix_g[layer]), cp_w_in[j].astype(BF16), cp_dw_w[j], row(cp_dw_b[j]),
                row(cp_ln_g[j]), row(cp_ln_b[j]), cp_pool_w[j].astype(BF16), row(cp_pool_scale[j]),
                cp_w_out[j].astype(BF16))
        else:
            h = _hgrn_mixer(
                h, row(norm_mix_g[layer]), hg_w_in[j].astype(BF16), hg_lb_logits.astype(F32),
                row(hg_gn_g[j]), hg_w_out[j].astype(BF16), layer=layer)
        last = layer == depth - 1
        h = _ffn_block(
            h.reshape(bsz * seq, d), row(norm_ffn_g[layer]), ffn_w1[layer].astype(BF16),
            ffn_w3[layer].astype(BF16), ffn_w2[layer].astype(BF16), row(final_g),
            final_norm=last).reshape(bsz, seq, d)
    return h
```

```python
import functools

import jax
import jax.numpy as jnp
from jax import lax
from jax.experimental import pallas as pl
from jax.experimental.pallas import tpu as pltpu

F32 = jnp.float32
BF16 = jnp.bfloat16

EPS = 1e-6
CHUNK = 64
CONV_WIDTH = 31
POOL_WINDOWS = (2, 4, 8, 16)
HG_HEADS = 8

V7X_VMEM_BYTES = 64 * 1024 * 1024
SUBLANES = 8
LANES = 128

TOKEN_TILE = 512
CONV_HALO = 32
POOL_HALO = 16
TIME_STRIDE = 4
TIME_BLOCK = SUBLANES * TIME_STRIDE
CONV_GROUP = 4
FFN_CHUNK = 512
FFN_LEAD_STAGES = 4


def _interleaved(start):
    return pl.ds(start, SUBLANES, stride=TIME_STRIDE)


def _vmem_limit(resident_bytes):
    return int(min(resident_bytes * 1.5 + (8 << 20), V7X_VMEM_BYTES - (6 << 20)))


def _rmsnorm(x, g):
    return x * lax.rsqrt(jnp.mean(x * x, axis=-1, keepdims=True) + EPS) * g


def _resident(shape, ngrid):
    zeros = (0,) * len(shape)
    if ngrid == 1:
        index_map = lambda i: zeros
    else:
        index_map = lambda b, t: zeros
    return pl.BlockSpec(shape, index_map, pipeline_mode=pl.Buffered(1))


def _ffn_kernel(h_ref, g_ref, w1_ref, w3_ref, w2_ref, fg_ref, o_ref, *, final_norm):
    h = h_ref[...]
    n = _rmsnorm(h, g_ref[...]).astype(BF16)
    a = jnp.dot(n, w1_ref[...], preferred_element_type=F32)
    b = jnp.dot(n, w3_ref[...], preferred_element_type=F32)
    gated = (a * jax.nn.sigmoid(a) * b).astype(BF16)
    out = h + jnp.dot(gated, w2_ref[...], preferred_element_type=F32)
    if final_norm:
        out = _rmsnorm(out, fg_ref[...])
    o_ref[...] = out


def _ffn_block(h2d, g, w1, w3, w2, final_g, *, final_norm):
    n_tok, d = h2d.shape
    d_ff = w1.shape[1]
    tm = TOKEN_TILE
    resident = 3 * d * d_ff * 2 + 4 * tm * d * 4 + 3 * tm * d_ff * 4
    return pl.pallas_call(
        functools.partial(_ffn_kernel, final_norm=final_norm),
        grid=(n_tok // tm,),
        in_specs=[
            pl.BlockSpec((tm, d), lambda i: (i, 0)),
            _resident((1, d), 1),
            _resident((d, d_ff), 1),
            _resident((d, d_ff), 1),
            _resident((d_ff, d), 1),
            _resident((1, d), 1),
        ],
        out_specs=pl.BlockSpec((tm, d), lambda i: (i, 0)),
        out_shape=jax.ShapeDtypeStruct((n_tok, d), F32),
        compiler_params=pltpu.CompilerParams(
            dimension_semantics=("arbitrary",),
            vmem_limit_bytes=_vmem_limit(resident)),
        name="ffn_block",
    )(h2d, g, w1, w3, w2, final_g)


def _layer0_kernel(x_ref, g_ref, w_in_ref, dw_w_ref, dw_b_ref, ln_g_ref, ln_b_ref, pool_w_ref, pool_s_ref,
                   w_out_ref, gf_ref, w1_ref, w3_ref, w2_ref, o_ref,
                   a_ext, u_ext, conv_buf, dlt_buf, h_prev, *, tm, tiles_per_seq, n_tiles):
    step = pl.program_id(0)
    t = lax.rem(jnp.minimum(step, n_tiles - 1), tiles_per_seq)
    n_conv = a_ext.shape[0]
    n_pool = u_ext.shape[0]
    d_conv = n_conv * LANES
    d_ff = w1_ref.shape[1]

    @pl.when(step == 0)
    def _():
        h_prev[...] = jnp.zeros(h_prev.shape, F32)
        a_ext[:, tm:tm + CONV_HALO, :] = jnp.zeros((n_conv, CONV_HALO, LANES), F32)
        u_ext[:, tm:tm + POOL_HALO, :] = jnp.zeros((n_pool, POOL_HALO, LANES), F32)

    x = x_ref[...]
    n = _rmsnorm(x, g_ref[...]).astype(BF16)
    z = jnp.dot(n, w_in_ref[:, :2 * d_conv], preferred_element_type=F32)

    inside = t > 0
    a_ext[:, 0:CONV_HALO, :] = jnp.where(inside, a_ext[:, tm:tm + CONV_HALO, :], 0.0)
    u_ext[:, 0:POOL_HALO, :] = jnp.where(inside, u_ext[:, tm:tm + POOL_HALO, :], 0.0)

    glu = z[:, :d_conv] * jax.nn.sigmoid(z[:, d_conv:2 * d_conv])
    for s in range(n_conv):
        a_ext[s, CONV_HALO:CONV_HALO + tm, :] = glu[:, s * LANES:(s + 1) * LANES]

    def pool_in():
        u = jnp.dot(n, w_in_ref[:, 2 * d_conv:], preferred_element_type=F32)
        for s in range(n_pool):
            u_ext[s, POOL_HALO:POOL_HALO + tm, :] = u[:, s * LANES:(s + 1) * LANES]

    hp = h_prev[...]
    nf = _rmsnorm(hp, gf_ref[...]).astype(BF16)
    ff_chunks = [(c0, min(c0 + FFN_CHUNK, d_ff)) for c0 in range(0, d_ff, FFN_CHUNK)]
    ab = {}

    def ffn_in(j):
        c0, c1 = ff_chunks[j]
        ab[j] = (jnp.dot(nf, w1_ref[:, c0:c1], preferred_element_type=F32),
                 jnp.dot(nf, w3_ref[:, c0:c1], preferred_element_type=F32))

    def ffn_out(j):
        c0, c1 = ff_chunks[j]
        a, b = ab.pop(j)
        part = jnp.dot((a * jax.nn.sigmoid(a) * b).astype(BF16), w2_ref[c0:c1, :], preferred_element_type=F32)
        if j == 0:
            o_ref[...] = hp + part
        else:
            o_ref[...] += part

    ffn_stages = [functools.partial(ffn_in, 0)]
    for j in range(1, len(ff_chunks)):
        ffn_stages += [functools.partial(ffn_in, j), functools.partial(ffn_out, j - 1)]
    ffn_stages.append(functools.partial(ffn_out, len(ff_chunks) - 1))

    base = CONV_HALO - (CONV_WIDTH - 1)

    def conv_group(s, g0):
        lanes = slice(s * LANES, (s + 1) * LANES)
        starts = [g0 + bi * TIME_BLOCK + m for bi in range(CONV_GROUP) for m in range(TIME_STRIDE)]
        accs = [jnp.broadcast_to(dw_b_ref[:, lanes], (SUBLANES, LANES))] * len(starts)
        loaded = {}
        for k in range(CONV_WIDTH):
            wk = jnp.broadcast_to(dw_w_ref[k:k + 1, lanes], (SUBLANES, LANES))
            for i, r in enumerate(starts):
                src = r + base + k
                if src not in loaded:
                    loaded[src] = a_ext[s, _interleaved(src), :]
                accs[i] = accs[i] + loaded[src] * wk
        for i, r in enumerate(starts):
            conv_buf[s, _interleaved(r), :] = accs[i]

    def pool_group(gi, w):
        for t0 in range(0, tm, TIME_BLOCK):
            loaded = {}
            for m in range(TIME_STRIDE):
                for j in range(w):
                    src = t0 + POOL_HALO + m - j
                    if src not in loaded:
                        loaded[src] = u_ext[gi, _interleaved(src), :]
                tok = loaded[t0 + POOL_HALO + m]
                tot = functools.reduce(jnp.add, [loaded[t0 + POOL_HALO + m - j] for j in range(w)])
                if t0 < w - 1:
                    frame = t * tm + t0 + m + TIME_STRIDE * lax.broadcasted_iota(jnp.int32, (SUBLANES, LANES), 0)
                    mean = tot / jnp.minimum(frame + 1, w).astype(F32)
                elif w & (w - 1) == 0:
                    mean = tot * (1.0 / w)
                else:
                    mean = tot / float(w)
                dlt_buf[gi, _interleaved(t0 + m), :] = mean - tok

    mixer_stages = [functools.partial(conv_group, s, g0)
                    for s in range(n_conv) for g0 in range(0, tm, CONV_GROUP * TIME_BLOCK)]
    mixer_stages.append(pool_in)
    mixer_stages += [functools.partial(pool_group, gi, w) for gi, w in enumerate(POOL_WINDOWS)]

    ffn_iter = iter(ffn_stages)
    per = -(-len(mixer_stages) // FFN_LEAD_STAGES)
    for i in range(FFN_LEAD_STAGES):
        next(ffn_iter)()
        for stage in mixer_stages[i * per:(i + 1) * per]:
            stage()
    next(ffn_iter)()

    cs = [conv_buf[s] for s in range(n_conv)]
    mu = jnp.sum(functools.reduce(jnp.add, cs), axis=-1, keepdims=True) / d_conv
    devs = [c - mu for c in cs]
    var = jnp.sum(functools.reduce(jnp.add, [dv * dv for dv in devs]), axis=-1, keepdims=True) / d_conv
    inv = lax.rsqrt(var + EPS)
    a_out = []
    for s in range(n_conv):
        lanes = slice(s * LANES, (s + 1) * LANES)
        c = devs[s] * inv * ln_g_ref[:, lanes] + ln_b_ref[:, lanes]
        a_out.append((c * jax.nn.sigmoid(c)).astype(BF16))
    next(ffn_iter)()
    pooled = [jnp.dot(dlt_buf[gi].astype(BF16), pool_w_ref[gi], preferred_element_type=F32)
              for gi in range(n_pool)]
    p_out = (jnp.concatenate(pooled, axis=-1) * pool_s_ref[...]).astype(BF16)
    cat = jnp.concatenate(a_out + [p_out], axis=-1)
    h_new = x + jnp.dot(cat, w_out_ref[...], preferred_element_type=F32)

    for stage in ffn_iter:
        stage()
    h_prev[...] = h_new


def _layer0(x2d, g, w_in, dw_w, dw_b, ln_g, ln_b, pool_w, pool_s, w_out, gf, w1, w3, w2, *, seq):
    n_tok, d = x2d.shape
    d_conv = dw_w.shape[1]
    d_pool = pool_s.shape[1]
    d_ff = w1.shape[1]
    tm = TOKEN_TILE
    n_tiles = n_tok // tm
    assert d_pool == len(POOL_WINDOWS) * LANES and d_conv % LANES == 0
    assert tm % (CONV_GROUP * TIME_BLOCK) == 0 and seq % tm == 0 and n_tok % seq == 0
    resident = (w_in.size + w_out.size + pool_w.size + 3 * d * d_ff) * 2 + 6 * tm * d * 4 \
        + tm * w_in.shape[1] * 4 + (2 * tm + CONV_HALO) * d_conv * 4 + (2 * tm + POOL_HALO) * d_pool * 4 \
        + 3 * tm * FFN_CHUNK * 4
    return pl.pallas_call(
        functools.partial(_layer0_kernel, tm=tm, tiles_per_seq=seq // tm, n_tiles=n_tiles),
        grid=(n_tiles + 1,),
        in_specs=[
            pl.BlockSpec((tm, d), lambda s: (jnp.minimum(s, n_tiles - 1), 0)),
            _resident((1, d), 1),
            _resident(w_in.shape, 1),
            _resident(dw_w.shape, 1),
            _resident((1, d_conv), 1),
            _resident((1, d_conv), 1),
            _resident((1, d_conv), 1),
            _resident(pool_w.shape, 1),
            _resident((1, d_pool), 1),
            _resident(w_out.shape, 1),
            _resident((1, d), 1),
            _resident(w1.shape, 1),
            _resident(w3.shape, 1),
            _resident(w2.shape, 1),
        ],
        out_specs=pl.BlockSpec((tm, d), lambda s: (jnp.maximum(s - 1, 0), 0)),
        out_shape=jax.ShapeDtypeStruct((n_tok, d), F32),
        scratch_shapes=[
            pltpu.VMEM((d_conv // LANES, CONV_HALO + tm, LANES), F32),
            pltpu.VMEM((d_pool // LANES, POOL_HALO + tm, LANES), F32),
            pltpu.VMEM((d_conv // LANES, tm, LANES), F32),
            pltpu.VMEM((d_pool // LANES, tm, LANES), F32),
            pltpu.VMEM((tm, d), F32),
        ],
        compiler_params=pltpu.CompilerParams(
            dimension_semantics=("arbitrary",),
            vmem_limit_bytes=_vmem_limit(resident)),
        name="layer0_convpool_ffn",
    )(x2d, g, w_in, dw_w, dw_b, ln_g, ln_b, pool_w, pool_s, w_out, gf, w1, w3, w2)


def _hgrn_kernel(h_ref, g_ref, w_in_ref, lb_logits_ref, gn_ref, w_out_ref, o_ref,
                 z_ref, y_ref, state_ref, *, tm, layer):
    t = pl.program_id(1)
    d_hg = y_ref.shape[1]
    dk = d_hg // HG_HEADS

    @pl.when(t == 0)
    def _():
        state_ref[...] = jnp.zeros(state_ref.shape, F32)

    x = h_ref[...]
    n = _rmsnorm(x, g_ref[...]).astype(BF16)
    z_ref[...] = jnp.dot(n, w_in_ref[...], preferred_element_type=F32)

    logits = lb_logits_ref[...]
    e = jnp.exp(logits - jnp.max(logits, axis=0, keepdims=True))
    p = e / jnp.sum(e, axis=0, keepdims=True)
    lb = jnp.sum(p[0:layer + 1], axis=0, keepdims=True) - p[0:1]

    row = lax.broadcasted_iota(jnp.int32, (CHUNK, CHUNK), 0)
    col = lax.broadcasted_iota(jnp.int32, (CHUNK, CHUNK), 1)
    causal = row >= col
    row3 = lax.broadcasted_iota(jnp.int32, (CHUNK, 3 * CHUNK), 0)
    col3 = lax.broadcasted_iota(jnp.int32, (CHUNK, 3 * CHUNK), 1) & (CHUNK - 1)
    tril3 = (row3 >= col3).astype(BF16)
    gn = gn_ref[...]
    mid = CHUNK // 2 - 1

    def chunk_step(c, carry):
        r0 = pl.multiple_of(c * CHUNK, CHUNK)
        rows = pl.ds(r0, CHUNK)
        q = z_ref[rows, 0:d_hg]
        f = lb + (1.0 - lb) * jax.nn.sigmoid(z_ref[rows, d_hg:2 * d_hg])
        log_f = jnp.log(f)
        k = 1.0 - f
        v = z_ref[rows, 2 * d_hg:3 * d_hg].astype(BF16)
        gate = z_ref[rows, 3 * d_hg:4 * d_hg]
        gate = gate * jax.nn.sigmoid(gate)

        hi = log_f.astype(BF16)
        rem = log_f - hi.astype(F32)
        md = rem.astype(BF16)
        lo = (rem - md.astype(F32)).astype(BF16)
        b = jnp.dot(tril3, jnp.concatenate([hi, md, lo], axis=0),
                    preferred_element_type=F32)

        b_mid = b[mid:mid + 1, :]
        b_last = b[CHUNK - 1:CHUNK, :]
        q_rel = q * jnp.exp(b - b_mid)
        k_rel = k * jnp.exp(b_mid - b)
        q_in = (q_rel * jnp.exp(b_mid)).astype(BF16)
        k_up = (k_rel * jnp.exp(b_last - b_mid)).astype(BF16)
        q_rel = q_rel.astype(BF16)
        k_rel = k_rel.astype(BF16)
        decay_last = jnp.exp(b_last)

        heads = [slice(hd * dk, (hd + 1) * dk) for hd in range(HG_HEADS)]
        nt = (((1,), (1,)), ((), ()))
        tn = (((0,), (0,)), ((), ()))
        scores = [lax.dot_general(q_rel[:, sl], k_rel[:, sl], nt, preferred_element_type=F32)
                  for sl in heads]
        state_t = [state_ref[hd] for hd in range(HG_HEADS)]
        o_inter = [lax.dot_general(q_in[:, sl], st.astype(BF16), nt, preferred_element_type=F32)
                   for sl, st in zip(heads, state_t)]
        kv = [lax.dot_general(v[:, sl], k_up[:, sl], tn, preferred_element_type=F32) for sl in heads]
        for hd, sl in enumerate(heads):
            state_ref[hd] = state_t[hd] * decay_last[:, sl] + kv[hd]
        scores = [jnp.where(causal, s, 0.0).astype(BF16) for s in scores]
        outs = [jnp.dot(s, v[:, sl], preferred_element_type=F32) + oi
                for s, sl, oi in zip(scores, heads, o_inter)]
        for o, sl in zip(outs, heads):
            o = o * lax.rsqrt(jnp.mean(o * o, axis=-1, keepdims=True) + EPS) * gn[:, sl]
            y_ref[rows, sl] = (o * gate[:, sl]).astype(BF16)
        return carry

    lax.fori_loop(0, tm // CHUNK, chunk_step, 0, unroll=True)
    o_ref[...] = x + jnp.dot(y_ref[...], w_out_ref[...], preferred_element_type=F32)


def _hgrn_mixer(h, g, w_in, lb_logits, gn_g, w_out, *, layer):
    bsz, seq, d = h.shape
    d_hg = w_out.shape[0]
    dk = d_hg // HG_HEADS
    tm = TOKEN_TILE
    resident = (w_in.size + w_out.size) * 2 + 4 * tm * d * 4 + tm * 4 * d_hg * 4 * 2 \
        + tm * d_hg * 2 + HG_HEADS * dk * dk * 4
    return pl.pallas_call(
        functools.partial(_hgrn_kernel, tm=tm, layer=layer),
        grid=(bsz, seq // tm),
        in_specs=[
            pl.BlockSpec((None, tm, d), lambda b, t: (b, t, 0)),
            _resident((1, d), 2),
            _resident(w_in.shape, 2),
            _resident(lb_logits.shape, 2),
            _resident((1, d_hg), 2),
            _resident(w_out.shape, 2),
        ],
        out_specs=pl.BlockSpec((None, tm, d), lambda b, t: (b, t, 0)),
        out_shape=jax.ShapeDtypeStruct((bsz, seq, d), F32),
        scratch_shapes=[
            pltpu.VMEM((tm, 4 * d_hg), F32),
            pltpu.VMEM((tm, d_hg), BF16),
            pltpu.VMEM((HG_HEADS, dk, dk), F32),
        ],
        compiler_params=pltpu.CompilerParams(
            dimension_semantics=("arbitrary", "arbitrary"),
            vmem_limit_bytes=_vmem_limit(resident)),
        name="hgrn_mixer",
    )(h, g, w_in, lb_logits, gn_g, w_out)


def kernel(x, norm_mix_g, norm_ffn_g, final_g, cp_w_in, cp_dw_w, cp_dw_b, cp_ln_g, cp_ln_b, cp_pool_w,
           cp_pool_scale, cp_w_out, hg_w_in, hg_lb_logits, hg_gn_g, hg_w_out, ffn_w1, ffn_w3, ffn_w2):
    bsz, seq, d = x.shape
    depth = norm_mix_g.shape[0]
    row = lambda a: a.reshape(1, -1).astype(F32)
    h = x
    for layer in range(depth):
        j = layer // 2
        last = layer == depth - 1
        w1, w3, w2 = (w[layer].astype(BF16) for w in (ffn_w1, ffn_w3, ffn_w2))
        if layer % 2 == 0:
            assert not last, "the fused conv/pool + FFN layer kernel has no final norm"
            h = _layer0(
                h.reshape(bsz * seq, d), row(norm_mix_g[layer]), cp_w_in[j].astype(BF16), cp_dw_w[j],
                row(cp_dw_b[j]), row(cp_ln_g[j]), row(cp_ln_b[j]), cp_pool_w[j].astype(BF16),
                row(cp_pool_scale[j]), cp_w_out[j].astype(BF16), row(norm_ffn_g[layer]), w1, w3, w2,
                seq=seq).reshape(bsz, seq, d)
        else:
            h = _hgrn_mixer(
                h, row(norm_mix_g[layer]), hg_w_in[j].astype(BF16), hg_lb_logits.astype(F32),
                row(hg_gn_g[j]), hg_w_out[j].astype(BF16), layer=layer)
            h = _ffn_block(
                h.reshape(bsz * seq, d), row(norm_ffn_g[layer]), w1, w3, w2, row(final_g),
                final_norm=last).reshape(bsz, seq, d)
    return h
```

```python
import functools

import jax
import jax.numpy as jnp
from jax import lax
from jax.experimental import pallas as pl
from jax.experimental.pallas import tpu as pltpu

F32 = jnp.float32
BF16 = jnp.bfloat16

EPS = 1e-6
CHUNK = 64
CONV_WIDTH = 31
POOL_WINDOWS = (2, 4, 8, 16)
HG_HEADS = 8

V7X_VMEM_BYTES = 64 * 1024 * 1024
SUBLANES = 8
LANES = 128

TOKEN_TILE = 512
CONV_HALO = 32
POOL_HALO = 16
TIME_STRIDE = 4
TIME_BLOCK = SUBLANES * TIME_STRIDE
CONV_GROUP = 4
FFN_CHUNK = 512
STAGE_BYTES = 1 << 19
FFN_LEAD_STAGES = 4


def _interleaved(start):
    return pl.ds(start, SUBLANES, stride=TIME_STRIDE)


def _vmem_limit(resident_bytes):
    return int(min(resident_bytes * 1.5 + (8 << 20), V7X_VMEM_BYTES - (6 << 20)))


def _rmsnorm(x, g):
    return x * lax.rsqrt(jnp.mean(x * x, axis=-1, keepdims=True) + EPS) * g


def _resident(shape, ngrid):
    zeros = (0,) * len(shape)
    if ngrid == 1:
        index_map = lambda i: zeros
    else:
        index_map = lambda b, t: zeros
    return pl.BlockSpec(shape, index_map, pipeline_mode=pl.Buffered(1))


def _stage_rows(n_rows, n_cols):
    pack = 2 * SUBLANES
    rows = max(pack, (STAGE_BYTES // (4 * n_cols)) // pack * pack)
    while n_rows % rows:
        rows -= pack
    return rows


def _stage_shape(w):
    return (2, _stage_rows(*w.shape), w.shape[1])


def _fetch_weights_as_bf16(jobs):
    chunks = []
    for src, dst, stage, sem in jobs:
        rows = stage.shape[1]
        for i in range(dst.shape[0] // rows):
            slot = len([c for c in chunks if c[2] is stage]) % 2
            chunks.append((src, dst, stage, sem, i * rows, rows, slot))

    def copy(c):
        src, _, stage, sem, r0, rows, slot = c
        return pltpu.make_async_copy(src.at[pl.ds(r0, rows)], stage.at[slot], sem.at[slot])

    copy(chunks[0]).start()
    for k, c in enumerate(chunks):
        if k + 1 < len(chunks):
            copy(chunks[k + 1]).start()
        copy(c).wait()
        _, dst, stage, _, r0, rows, slot = c
        dst[pl.ds(r0, rows), :] = stage[slot].astype(BF16)


def _ffn_kernel(h_ref, g_ref, w1_hbm, w3_hbm, w2_hbm, fg_ref, o_ref,
                w1_ref, w3_ref, w2_ref, stage_in, stage_out, sem_in, sem_out, *, layer, final_norm):
    @pl.when(pl.program_id(0) == 0)
    def _():
        _fetch_weights_as_bf16([(w1_hbm.at[layer], w1_ref, stage_in, sem_in),
                                (w3_hbm.at[layer], w3_ref, stage_in, sem_in),
                                (w2_hbm.at[layer], w2_ref, stage_out, sem_out)])

    h = h_ref[...]
    n = _rmsnorm(h, g_ref[...]).astype(BF16)
    a = jnp.dot(n, w1_ref[...], preferred_element_type=F32)
    b = jnp.dot(n, w3_ref[...], preferred_element_type=F32)
    gated = (a * jax.nn.sigmoid(a) * b).astype(BF16)
    out = h + jnp.dot(gated, w2_ref[...], preferred_element_type=F32)
    if final_norm:
        out = _rmsnorm(out, fg_ref[...])
    o_ref[...] = out


def _ffn_block(h2d, g, w1_all, w3_all, w2_all, final_g, *, layer, final_norm):
    n_tok, d = h2d.shape
    w1, w3, w2 = (jax.ShapeDtypeStruct(w.shape[1:], w.dtype) for w in (w1_all, w3_all, w2_all))
    d_ff = w1.shape[1]
    tm = TOKEN_TILE
    resident = 3 * d * d_ff * 2 + 4 * tm * d * 4 + 3 * tm * d_ff * 4 + 4 * STAGE_BYTES
    hbm = pl.BlockSpec(memory_space=pl.ANY)
    return pl.pallas_call(
        functools.partial(_ffn_kernel, layer=layer, final_norm=final_norm),
        grid=(n_tok // tm,),
        in_specs=[
            pl.BlockSpec((tm, d), lambda i: (i, 0)),
            _resident((1, d), 1),
            hbm, hbm, hbm,
            _resident((1, d), 1),
        ],
        out_specs=pl.BlockSpec((tm, d), lambda i: (i, 0)),
        out_shape=jax.ShapeDtypeStruct((n_tok, d), F32),
        scratch_shapes=[
            pltpu.VMEM(w1.shape, BF16),
            pltpu.VMEM(w3.shape, BF16),
            pltpu.VMEM(w2.shape, BF16),
            pltpu.VMEM(_stage_shape(w1), F32),
            pltpu.VMEM(_stage_shape(w2), F32),
            pltpu.SemaphoreType.DMA((2,)),
            pltpu.SemaphoreType.DMA((2,)),
        ],
        compiler_params=pltpu.CompilerParams(
            dimension_semantics=("arbitrary",),
            vmem_limit_bytes=_vmem_limit(resident)),
        name="ffn_block",
    )(h2d, g, w1_all, w3_all, w2_all, final_g)


def _layer0_kernel(x_ref, g_ref, w_in_hbm, dw_w_ref, dw_b_ref, ln_g_ref, ln_b_ref, pool_w_ref, pool_s_ref,
                   w_out_hbm, gf_ref, w1_hbm, w3_hbm, w2_hbm, o_ref,
                   a_ext, u_ext, conv_buf, dlt_buf, h_prev, w_in_ref, w_out_ref, w1_ref, w3_ref, w2_ref,
                   stage_in, stage_d, stage_ff, sem_in, sem_d, sem_ff,
                   *, tm, tiles_per_seq, n_tiles, layer, mixer_index):
    step = pl.program_id(0)
    t = lax.rem(jnp.minimum(step, n_tiles - 1), tiles_per_seq)
    n_conv = a_ext.shape[0]
    n_pool = u_ext.shape[0]
    d_conv = n_conv * LANES
    d_ff = w1_ref.shape[1]

    @pl.when(step == 0)
    def _():
        h_prev[...] = jnp.zeros(h_prev.shape, F32)
        a_ext[:, tm:tm + CONV_HALO, :] = jnp.zeros((n_conv, CONV_HALO, LANES), F32)
        u_ext[:, tm:tm + POOL_HALO, :] = jnp.zeros((n_pool, POOL_HALO, LANES), F32)
        _fetch_weights_as_bf16([(w_in_hbm.at[mixer_index], w_in_ref, stage_in, sem_in),
                                (w_out_hbm.at[mixer_index], w_out_ref, stage_d, sem_d),
                                (w2_hbm.at[layer], w2_ref, stage_d, sem_d),
                                (w1_hbm.at[layer], w1_ref, stage_ff, sem_ff),
                                (w3_hbm.at[layer], w3_ref, stage_ff, sem_ff)])

    x = x_ref[...]
    n = _rmsnorm(x, g_ref[...]).astype(BF16)
    z = jnp.dot(n, w_in_ref[:, :2 * d_conv], preferred_element_type=F32)

    inside = t > 0
    a_ext[:, 0:CONV_HALO, :] = jnp.where(inside, a_ext[:, tm:tm + CONV_HALO, :], 0.0)
    u_ext[:, 0:POOL_HALO, :] = jnp.where(inside, u_ext[:, tm:tm + POOL_HALO, :], 0.0)

    glu = z[:, :d_conv] * jax.nn.sigmoid(z[:, d_conv:2 * d_conv])
    for s in range(n_conv):
        a_ext[s, CONV_HALO:CONV_HALO + tm, :] = glu[:, s * LANES:(s + 1) * LANES]

    def pool_in():
        u = jnp.dot(n, w_in_ref[:, 2 * d_conv:], preferred_element_type=F32)
        for s in range(n_pool):
            u_ext[s, POOL_HALO:POOL_HALO + tm, :] = u[:, s * LANES:(s + 1) * LANES]

    hp = h_prev[...]
    nf = _rmsnorm(hp, gf_ref[...]).astype(BF16)
    ff_chunks = [(c0, min(c0 + FFN_CHUNK, d_ff)) for c0 in range(0, d_ff, FFN_CHUNK)]
    ab = {}

    def ffn_in(j):
        c0, c1 = ff_chunks[j]
        ab[j] = (jnp.dot(nf, w1_ref[:, c0:c1], preferred_element_type=F32),
                 jnp.dot(nf, w3_ref[:, c0:c1], preferred_element_type=F32))

    def ffn_out(j):
        c0, c1 = ff_chunks[j]
        a, b = ab.pop(j)
        part = jnp.dot((a * jax.nn.sigmoid(a) * b).astype(BF16), w2_ref[c0:c1, :], preferred_element_type=F32)
        if j == 0:
            o_ref[...] = hp + part
        else:
            o_ref[...] += part

    ffn_stages = [functools.partial(ffn_in, 0)]
    for j in range(1, len(ff_chunks)):
        ffn_stages += [functools.partial(ffn_in, j), functools.partial(ffn_out, j - 1)]
    ffn_stages.append(functools.partial(ffn_out, len(ff_chunks) - 1))

    base = CONV_HALO - (CONV_WIDTH - 1)

    def conv_group(s, g0):
        lanes = slice(s * LANES, (s + 1) * LANES)
        starts = [g0 + bi * TIME_BLOCK + m for bi in range(CONV_GROUP) for m in range(TIME_STRIDE)]
        accs = [jnp.broadcast_to(dw_b_ref[:, lanes], (SUBLANES, LANES))] * len(starts)
        loaded = {}
        for k in range(CONV_WIDTH):
            wk = jnp.broadcast_to(dw_w_ref[k:k + 1, lanes], (SUBLANES, LANES))
            for i, r in enumerate(starts):
                src = r + base + k
                if src not in loaded:
                    loaded[src] = a_ext[s, _interleaved(src), :]
                accs[i] = accs[i] + loaded[src] * wk
        for i, r in enumerate(starts):
            conv_buf[s, _interleaved(r), :] = accs[i]

    def pool_group(gi, w):
        for t0 in range(0, tm, TIME_BLOCK):
            loaded = {}
            for m in range(TIME_STRIDE):
                for j in range(w):
                    src = t0 + POOL_HALO + m - j
                    if src not in loaded:
                        loaded[src] = u_ext[gi, _interleaved(src), :]
                tok = loaded[t0 + POOL_HALO + m]
                tot = functools.reduce(jnp.add, [loaded[t0 + POOL_HALO + m - j] for j in range(w)])
                if t0 < w - 1:
                    frame = t * tm + t0 + m + TIME_STRIDE * lax.broadcasted_iota(jnp.int32, (SUBLANES, LANES), 0)
                    mean = tot / jnp.minimum(frame + 1, w).astype(F32)
                elif w & (w - 1) == 0:
                    mean = tot * (1.0 / w)
                else:
                    mean = tot / float(w)
                dlt_buf[gi, _interleaved(t0 + m), :] = mean - tok

    mixer_stages = [functools.partial(conv_group, s, g0)
                    for s in range(n_conv) for g0 in range(0, tm, CONV_GROUP * TIME_BLOCK)]
    mixer_stages.append(pool_in)
    mixer_stages += [functools.partial(pool_group, gi, w) for gi, w in enumerate(POOL_WINDOWS)]

    ffn_iter = iter(ffn_stages)
    per = -(-len(mixer_stages) // FFN_LEAD_STAGES)
    for i in range(FFN_LEAD_STAGES):
        next(ffn_iter)()
        for stage in mixer_stages[i * per:(i + 1) * per]:
            stage()
    next(ffn_iter)()

    cs = [conv_buf[s] for s in range(n_conv)]
    mu = jnp.sum(functools.reduce(jnp.add, cs), axis=-1, keepdims=True) / d_conv
    devs = [c - mu for c in cs]
    var = jnp.sum(functools.reduce(jnp.add, [dv * dv for dv in devs]), axis=-1, keepdims=True) / d_conv
    inv = lax.rsqrt(var + EPS)
    a_out = []
    for s in range(n_conv):
        lanes = slice(s * LANES, (s + 1) * LANES)
        c = devs[s] * inv * ln_g_ref[:, lanes] + ln_b_ref[:, lanes]
        a_out.append((c * jax.nn.sigmoid(c)).astype(BF16))
    next(ffn_iter)()
    pooled = [jnp.dot(dlt_buf[gi].astype(BF16), pool_w_ref[gi], preferred_element_type=F32)
              for gi in range(n_pool)]
    p_out = (jnp.concatenate(pooled, axis=-1) * pool_s_ref[...]).astype(BF16)
    cat = jnp.concatenate(a_out + [p_out], axis=-1)
    h_new = x + jnp.dot(cat, w_out_ref[...], preferred_element_type=F32)

    for stage in ffn_iter:
        stage()
    h_prev[...] = h_new


def _layer0(x2d, g, w_in_all, dw_w, dw_b, ln_g, ln_b, pool_w, pool_s, w_out_all, gf, w1_all, w3_all, w2_all,
            *, seq, layer, mixer_index):
    n_tok, d = x2d.shape
    w_in, w_out, w1, w3, w2 = (jax.ShapeDtypeStruct(w.shape[1:], w.dtype)
                               for w in (w_in_all, w_out_all, w1_all, w3_all, w2_all))
    d_conv = dw_w.shape[1]
    d_pool = pool_s.shape[1]
    d_ff = w1.shape[1]
    tm = TOKEN_TILE
    n_tiles = n_tok // tm
    assert d_pool == len(POOL_WINDOWS) * LANES and d_conv % LANES == 0
    assert tm % (CONV_GROUP * TIME_BLOCK) == 0 and seq % tm == 0 and n_tok % seq == 0
    assert _stage_shape(w_out) == _stage_shape(w2) and _stage_shape(w1) == _stage_shape(w3)
    resident = (w_in.size + w_out.size + pool_w.size + 3 * d * d_ff) * 2 + 6 * tm * d * 4 \
        + tm * w_in.shape[1] * 4 + (2 * tm + CONV_HALO) * d_conv * 4 + (2 * tm + POOL_HALO) * d_pool * 4 \
        + 3 * tm * FFN_CHUNK * 4 + 6 * STAGE_BYTES
    hbm = pl.BlockSpec(memory_space=pl.ANY)
    return pl.pallas_call(
        functools.partial(_layer0_kernel, tm=tm, tiles_per_seq=seq // tm, n_tiles=n_tiles,
                          layer=layer, mixer_index=mixer_index),
        grid=(n_tiles + 1,),
        in_specs=[
            pl.BlockSpec((tm, d), lambda s: (jnp.minimum(s, n_tiles - 1), 0)),
            _resident((1, d), 1),
            hbm,
            _resident(dw_w.shape, 1),
            _resident((1, d_conv), 1),
            _resident((1, d_conv), 1),
            _resident((1, d_conv), 1),
            _resident(pool_w.shape, 1),
            _resident((1, d_pool), 1),
            hbm,
            _resident((1, d), 1),
            hbm, hbm, hbm,
        ],
        out_specs=pl.BlockSpec((tm, d), lambda s: (jnp.maximum(s - 1, 0), 0)),
        out_shape=jax.ShapeDtypeStruct((n_tok, d), F32),
        scratch_shapes=[
            pltpu.VMEM((d_conv // LANES, CONV_HALO + tm, LANES), F32),
            pltpu.VMEM((d_pool // LANES, POOL_HALO + tm, LANES), F32),
            pltpu.VMEM((d_conv // LANES, tm, LANES), F32),
            pltpu.VMEM((d_pool // LANES, tm, LANES), F32),
            pltpu.VMEM((tm, d), F32),
            pltpu.VMEM(w_in.shape, BF16),
            pltpu.VMEM(w_out.shape, BF16),
            pltpu.VMEM(w1.shape, BF16),
            pltpu.VMEM(w3.shape, BF16),
            pltpu.VMEM(w2.shape, BF16),
            pltpu.VMEM(_stage_shape(w_in), F32),
            pltpu.VMEM(_stage_shape(w_out), F32),
            pltpu.VMEM(_stage_shape(w1), F32),
            pltpu.SemaphoreType.DMA((2,)),
            pltpu.SemaphoreType.DMA((2,)),
            pltpu.SemaphoreType.DMA((2,)),
        ],
        compiler_params=pltpu.CompilerParams(
            dimension_semantics=("arbitrary",),
            vmem_limit_bytes=_vmem_limit(resident)),
        name="layer0_convpool_ffn",
    )(x2d, g, w_in_all, dw_w, dw_b, ln_g, ln_b, pool_w, pool_s, w_out_all, gf, w1_all, w3_all, w2_all)


def _hgrn_kernel(h_ref, g_ref, w_in_hbm, lb_logits_ref, gn_ref, w_out_hbm, o_ref,
                 z_ref, y_ref, state_ref, w_in_ref, w_out_ref, stage_in, stage_out, sem_in, sem_out,
                 *, tm, layer, mixer_index):
    t = pl.program_id(1)
    d_hg = y_ref.shape[1]
    dk = d_hg // HG_HEADS

    @pl.when((pl.program_id(0) == 0) & (t == 0))
    def _():
        _fetch_weights_as_bf16([(w_in_hbm.at[mixer_index], w_in_ref, stage_in, sem_in),
                                (w_out_hbm.at[mixer_index], w_out_ref, stage_out, sem_out)])

    @pl.when(t == 0)
    def _():
        state_ref[...] = jnp.zeros(state_ref.shape, F32)

    x = h_ref[...]
    n = _rmsnorm(x, g_ref[...]).astype(BF16)
    z_ref[...] = jnp.dot(n, w_in_ref[...], preferred_element_type=F32)

    logits = lb_logits_ref[...]
    e = jnp.exp(logits - jnp.max(logits, axis=0, keepdims=True))
    p = e / jnp.sum(e, axis=0, keepdims=True)
    lb = jnp.sum(p[0:layer + 1], axis=0, keepdims=True) - p[0:1]

    row = lax.broadcasted_iota(jnp.int32, (CHUNK, CHUNK), 0)
    col = lax.broadcasted_iota(jnp.int32, (CHUNK, CHUNK), 1)
    causal = row >= col
    row3 = lax.broadcasted_iota(jnp.int32, (CHUNK, 3 * CHUNK), 0)
    col3 = lax.broadcasted_iota(jnp.int32, (CHUNK, 3 * CHUNK), 1) & (CHUNK - 1)
    tril3 = (row3 >= col3).astype(BF16)
    gn = gn_ref[...]
    mid = CHUNK // 2 - 1

    def chunk_step(c, carry):
        r0 = pl.multiple_of(c * CHUNK, CHUNK)
        rows = pl.ds(r0, CHUNK)
        q = z_ref[rows, 0:d_hg]
        f = lb + (1.0 - lb) * jax.nn.sigmoid(z_ref[rows, d_hg:2 * d_hg])
        log_f = jnp.log(f)
        k = 1.0 - f
        v = z_ref[rows, 2 * d_hg:3 * d_hg].astype(BF16)
        gate = z_ref[rows, 3 * d_hg:4 * d_hg]
        gate = gate * jax.nn.sigmoid(gate)

        hi = log_f.astype(BF16)
        rem = log_f - hi.astype(F32)
        md = rem.astype(BF16)
        lo = (rem - md.astype(F32)).astype(BF16)
        b = jnp.dot(tril3, jnp.concatenate([hi, md, lo], axis=0),
                    preferred_element_type=F32)

        b_mid = b[mid:mid + 1, :]
        b_last = b[CHUNK - 1:CHUNK, :]
        q_rel = q * jnp.exp(b - b_mid)
        k_rel = k * jnp.exp(b_mid - b)
        q_in = (q_rel * jnp.exp(b_mid)).astype(BF16)
        k_up = (k_rel * jnp.exp(b_last - b_mid)).astype(BF16)
        q_rel = q_rel.astype(BF16)
        k_rel = k_rel.astype(BF16)
        decay_last = jnp.exp(b_last)

        heads = [slice(hd * dk, (hd + 1) * dk) for hd in range(HG_HEADS)]
        nt = (((1,), (1,)), ((), ()))
        tn = (((0,), (0,)), ((), ()))
        scores = [lax.dot_general(q_rel[:, sl], k_rel[:, sl], nt, preferred_element_type=F32)
                  for sl in heads]
        state_t = [state_ref[hd] for hd in range(HG_HEADS)]
        o_inter = [lax.dot_general(q_in[:, sl], st.astype(BF16), nt, preferred_element_type=F32)
                   for sl, st in zip(heads, state_t)]
        kv = [lax.dot_general(v[:, sl], k_up[:, sl], tn, preferred_element_type=F32) for sl in heads]
        for hd, sl in enumerate(heads):
            state_ref[hd] = state_t[hd] * decay_last[:, sl] + kv[hd]
        scores = [jnp.where(causal, s, 0.0).astype(BF16) for s in scores]
        outs = [jnp.dot(s, v[:, sl], preferred_element_type=F32) + oi
                for s, sl, oi in zip(scores, heads, o_inter)]
        for o, sl in zip(outs, heads):
            o = o * lax.rsqrt(jnp.mean(o * o, axis=-1, keepdims=True) + EPS) * gn[:, sl]
            y_ref[rows, sl] = (o * gate[:, sl]).astype(BF16)
        return carry

    lax.fori_loop(0, tm // CHUNK, chunk_step, 0, unroll=True)
    o_ref[...] = x + jnp.dot(y_ref[...], w_out_ref[...], preferred_element_type=F32)


def _hgrn_mixer(h, g, w_in_all, lb_logits, gn_g, w_out_all, *, layer, mixer_index):
    bsz, seq, d = h.shape
    w_in, w_out = (jax.ShapeDtypeStruct(w.shape[1:], w.dtype) for w in (w_in_all, w_out_all))
    d_hg = w_out.shape[0]
    dk = d_hg // HG_HEADS
    tm = TOKEN_TILE
    resident = (w_in.size + w_out.size) * 2 + 4 * tm * d * 4 + tm * 4 * d_hg * 4 * 2 \
        + tm * d_hg * 2 + HG_HEADS * dk * dk * 4 + 4 * STAGE_BYTES
    hbm = pl.BlockSpec(memory_space=pl.ANY)
    return pl.pallas_call(
        functools.partial(_hgrn_kernel, tm=tm, layer=layer, mixer_index=mixer_index),
        grid=(bsz, seq // tm),
        in_specs=[
            pl.BlockSpec((None, tm, d), lambda b, t: (b, t, 0)),
            _resident((1, d), 2),
            hbm,
            _resident(lb_logits.shape, 2),
            _resident((1, d_hg), 2),
            hbm,
        ],
        out_specs=pl.BlockSpec((None, tm, d), lambda b, t: (b, t, 0)),
        out_shape=jax.ShapeDtypeStruct((bsz, seq, d), F32),
        scratch_shapes=[
            pltpu.VMEM((tm, 4 * d_hg), F32),
            pltpu.VMEM((tm, d_hg), BF16),
            pltpu.VMEM((HG_HEADS, dk, dk), F32),
            pltpu.VMEM(w_in.shape, BF16),
            pltpu.VMEM(w_out.shape, BF16),
            pltpu.VMEM(_stage_shape(w_in), F32),
            pltpu.VMEM(_stage_shape(w_out), F32),
            pltpu.SemaphoreType.DMA((2,)),
            pltpu.SemaphoreType.DMA((2,)),
        ],
        compiler_params=pltpu.CompilerParams(
            dimension_semantics=("arbitrary", "arbitrary"),
            vmem_limit_bytes=_vmem_limit(resident)),
        name="hgrn_mixer",
    )(h, g, w_in_all, lb_logits, gn_g, w_out_all)


def kernel(x, norm_mix_g, norm_ffn_g, final_g, cp_w_in, cp_dw_w, cp_dw_b, cp_ln_g, cp_ln_b, cp_pool_w,
           cp_pool_scale, cp_w_out, hg_w_in, hg_lb_logits, hg_gn_g, hg_w_out, ffn_w1, ffn_w3, ffn_w2):
    bsz, seq, d = x.shape
    depth = norm_mix_g.shape[0]
    row = lambda a: a.reshape(1, -1).astype(F32)
    h = x
    for layer in range(depth):
        j = layer // 2
        last = layer == depth - 1
        if layer % 2 == 0:
            assert not last, "the fused conv/pool + FFN layer kernel has no final norm"
            h = _layer0(
                h.reshape(bsz * seq, d), row(norm_mix_g[layer]), cp_w_in, cp_dw_w[j],
                row(cp_dw_b[j]), row(cp_ln_g[j]), row(cp_ln_b[j]), cp_pool_w[j].astype(BF16),
                row(cp_pool_scale[j]), cp_w_out, row(norm_ffn_g[layer]), ffn_w1, ffn_w3, ffn_w2,
                seq=seq, layer=layer, mixer_index=j).reshape(bsz, seq, d)
        else:
            h = _hgrn_mixer(
                h, row(norm_mix_g[layer]), hg_w_in, hg_lb_logits.astype(F32),
                row(hg_gn_g[j]), hg_w_out, layer=layer, mixer_index=j)
            h = _ffn_block(
                h.reshape(bsz * seq, d), row(norm_ffn_g[layer]), ffn_w1, ffn_w3, ffn_w2, row(final_g),
                layer=layer, final_norm=last).reshape(bsz, seq, d)
    return h
```

```python
import functools
import itertools

import jax
import jax.numpy as jnp
from jax import lax
from jax.experimental import pallas as pl
from jax.experimental.pallas import tpu as pltpu

F32 = jnp.float32
BF16 = jnp.bfloat16

EPS = 1e-6
CHUNK = 64
CONV_WIDTH = 31
POOL_WINDOWS = (2, 4, 8, 16)
HG_HEADS = 8

V7X_VMEM_BYTES = 64 * 1024 * 1024
SUBLANES = 8
LANES = 128

TOKEN_TILE = 512
CONV_HALO = 32
POOL_HALO = 16
TIME_STRIDE = 4
TIME_BLOCK = SUBLANES * TIME_STRIDE
CONV_GROUP = 4
FFN_CHUNK = 512
STAGE_SLOTS = 3
FFN_STAGE_BYTES = 2 << 20
HGRN_STAGE_BYTES = 1 << 20
LAYER0_STAGE_BYTES = 1 << 19
FFN_LEAD_STAGES = 4


def _interleaved(start):
    return pl.ds(start, SUBLANES, stride=TIME_STRIDE)


def _vmem_limit(resident_bytes):
    return int(min(resident_bytes * 1.5 + (8 << 20), V7X_VMEM_BYTES - (6 << 20)))


def _rmsnorm(x, g):
    return x * lax.rsqrt(jnp.mean(x * x, axis=-1, keepdims=True) + EPS) * g


def _resident(shape, ngrid):
    zeros = (0,) * len(shape)
    if ngrid == 1:
        index_map = lambda i: zeros
    else:
        index_map = lambda b, t: zeros
    return pl.BlockSpec(shape, index_map, pipeline_mode=pl.Buffered(1))


def _stage_shape(w, stage_bytes):
    n_rows, n_cols = w.shape
    pack = 2 * SUBLANES
    rows = max(pack, (stage_bytes // (4 * n_cols)) // pack * pack)
    while n_rows % rows:
        rows -= pack
    return (STAGE_SLOTS, rows, n_cols)


def _fetch_weights_as_bf16(jobs):
    queues = {}
    for src, dst, stage, sem in jobs:
        queue = queues.setdefault(id(stage), [])
        rows = stage.shape[1]
        for i in range(dst.shape[0] // rows):
            queue.append((src, dst, stage, sem, i * rows, rows, len(queue) % stage.shape[0]))
    chunks = [c for group in itertools.zip_longest(*queues.values()) for c in group if c is not None]

    def copy(c):
        src, _, stage, sem, r0, rows, slot = c
        return pltpu.make_async_copy(src.at[pl.ds(r0, rows)], stage.at[slot], sem.at[slot])

    starts_after = [[] for _ in chunks]
    last_in_slot = {}
    for k, c in enumerate(chunks):
        key = (id(c[2]), c[6])
        if key in last_in_slot:
            starts_after[last_in_slot[key]].append(k)
        else:
            copy(c).start()
        last_in_slot[key] = k
    for k, c in enumerate(chunks):
        copy(c).wait()
        _, dst, stage, _, r0, rows, slot = c
        dst[pl.ds(r0, rows), :] = stage[slot].astype(BF16)
        for nxt in starts_after[k]:
            copy(chunks[nxt]).start()


def _ffn_kernel(h_ref, g_ref, w1_hbm, w3_hbm, w2_hbm, fg_ref, o_ref,
                w1_ref, w3_ref, w2_ref, stage_in, stage_out, sem_in, sem_out, *, layer, final_norm):
    @pl.when(pl.program_id(0) == 0)
    def _():
        _fetch_weights_as_bf16([(w1_hbm.at[layer], w1_ref, stage_in, sem_in),
                                (w3_hbm.at[layer], w3_ref, stage_in, sem_in),
                                (w2_hbm.at[layer], w2_ref, stage_out, sem_out)])

    h = h_ref[...]
    n = _rmsnorm(h, g_ref[...]).astype(BF16)
    a = jnp.dot(n, w1_ref[...], preferred_element_type=F32)
    b = jnp.dot(n, w3_ref[...], preferred_element_type=F32)
    gated = (a * jax.nn.sigmoid(a) * b).astype(BF16)
    out = h + jnp.dot(gated, w2_ref[...], preferred_element_type=F32)
    if final_norm:
        out = _rmsnorm(out, fg_ref[...])
    o_ref[...] = out


def _ffn_block(h2d, g, w1_all, w3_all, w2_all, final_g, *, layer, final_norm):
    n_tok, d = h2d.shape
    w1, w3, w2 = (jax.ShapeDtypeStruct(w.shape[1:], w.dtype) for w in (w1_all, w3_all, w2_all))
    d_ff = w1.shape[1]
    tm = TOKEN_TILE
    stages = [_stage_shape(w1, FFN_STAGE_BYTES), _stage_shape(w2, FFN_STAGE_BYTES)]
    assert stages[0] == _stage_shape(w3, FFN_STAGE_BYTES)
    resident = 3 * d * d_ff * 2 + 4 * tm * d * 4 + 3 * tm * d_ff * 4 + 2 * STAGE_SLOTS * FFN_STAGE_BYTES
    hbm = pl.BlockSpec(memory_space=pl.ANY)
    return pl.pallas_call(
        functools.partial(_ffn_kernel, layer=layer, final_norm=final_norm),
        grid=(n_tok // tm,),
        in_specs=[
            pl.BlockSpec((tm, d), lambda i: (i, 0)),
            _resident((1, d), 1),
            hbm, hbm, hbm,
            _resident((1, d), 1),
        ],
        out_specs=pl.BlockSpec((tm, d), lambda i: (i, 0)),
        out_shape=jax.ShapeDtypeStruct((n_tok, d), F32),
        scratch_shapes=[
            pltpu.VMEM(w1.shape, BF16),
            pltpu.VMEM(w3.shape, BF16),
            pltpu.VMEM(w2.shape, BF16),
            pltpu.VMEM(stages[0], F32),
            pltpu.VMEM(stages[1], F32),
            pltpu.SemaphoreType.DMA((STAGE_SLOTS,)),
            pltpu.SemaphoreType.DMA((STAGE_SLOTS,)),
        ],
        compiler_params=pltpu.CompilerParams(
            dimension_semantics=("arbitrary",),
            vmem_limit_bytes=_vmem_limit(resident)),
        name="ffn_block",
    )(h2d, g, w1_all, w3_all, w2_all, final_g)


def _layer0_kernel(x_ref, g_ref, w_in_hbm, dw_w_ref, dw_b_ref, ln_g_ref, ln_b_ref, pool_w_ref, pool_s_ref,
                   w_out_hbm, gf_ref, w1_hbm, w3_hbm, w2_hbm, o_ref,
                   a_ext, u_ext, conv_buf, dlt_buf, h_prev, w_in_ref, w_out_ref, w1_ref, w3_ref, w2_ref,
                   stage_in, stage_d, stage_ff, sem_in, sem_d, sem_ff,
                   *, tm, tiles_per_seq, n_tiles, layer, mixer_index):
    step = pl.program_id(0)
    t = lax.rem(jnp.minimum(step, n_tiles - 1), tiles_per_seq)
    n_conv = a_ext.shape[0]
    n_pool = u_ext.shape[0]
    d_conv = n_conv * LANES
    d_ff = w1_ref.shape[1]

    @pl.when(step == 0)
    def _():
        h_prev[...] = jnp.zeros(h_prev.shape, F32)
        a_ext[:, tm:tm + CONV_HALO, :] = jnp.zeros((n_conv, CONV_HALO, LANES), F32)
        u_ext[:, tm:tm + POOL_HALO, :] = jnp.zeros((n_pool, POOL_HALO, LANES), F32)
        _fetch_weights_as_bf16([(w_in_hbm.at[mixer_index], w_in_ref, stage_in, sem_in),
                                (w_out_hbm.at[mixer_index], w_out_ref, stage_d, sem_d),
                                (w2_hbm.at[layer], w2_ref, stage_d, sem_d),
                                (w1_hbm.at[layer], w1_ref, stage_ff, sem_ff),
                                (w3_hbm.at[layer], w3_ref, stage_ff, sem_ff)])

    x = x_ref[...]
    n = _rmsnorm(x, g_ref[...]).astype(BF16)
    z = jnp.dot(n, w_in_ref[:, :2 * d_conv], preferred_element_type=F32)

    inside = t > 0
    a_ext[:, 0:CONV_HALO, :] = jnp.where(inside, a_ext[:, tm:tm + CONV_HALO, :], 0.0)
    u_ext[:, 0:POOL_HALO, :] = jnp.where(inside, u_ext[:, tm:tm + POOL_HALO, :], 0.0)

    glu = z[:, :d_conv] * jax.nn.sigmoid(z[:, d_conv:2 * d_conv])
    for s in range(n_conv):
        a_ext[s, CONV_HALO:CONV_HALO + tm, :] = glu[:, s * LANES:(s + 1) * LANES]

    def pool_in():
        u = jnp.dot(n, w_in_ref[:, 2 * d_conv:], preferred_element_type=F32)
        for s in range(n_pool):
            u_ext[s, POOL_HALO:POOL_HALO + tm, :] = u[:, s * LANES:(s + 1) * LANES]

    hp = h_prev[...]
    nf = _rmsnorm(hp, gf_ref[...]).astype(BF16)
    ff_chunks = [(c0, min(c0 + FFN_CHUNK, d_ff)) for c0 in range(0, d_ff, FFN_CHUNK)]
    ab = {}

    def ffn_in(j):
        c0, c1 = ff_chunks[j]
        ab[j] = (jnp.dot(nf, w1_ref[:, c0:c1], preferred_element_type=F32),
                 jnp.dot(nf, w3_ref[:, c0:c1], preferred_element_type=F32))

    def ffn_out(j):
        c0, c1 = ff_chunks[j]
        a, b = ab.pop(j)
        part = jnp.dot((a * jax.nn.sigmoid(a) * b).astype(BF16), w2_ref[c0:c1, :], preferred_element_type=F32)
        if j == 0:
            o_ref[...] = hp + part
        else:
            o_ref[...] += part

    ffn_stages = [functools.partial(ffn_in, 0)]
    for j in range(1, len(ff_chunks)):
        ffn_stages += [functools.partial(ffn_in, j), functools.partial(ffn_out, j - 1)]
    ffn_stages.append(functools.partial(ffn_out, len(ff_chunks) - 1))

    base = CONV_HALO - (CONV_WIDTH - 1)

    def conv_group(s, g0):
        lanes = slice(s * LANES, (s + 1) * LANES)
        starts = [g0 + bi * TIME_BLOCK + m for bi in range(CONV_GROUP) for m in range(TIME_STRIDE)]
        accs = [jnp.broadcast_to(dw_b_ref[:, lanes], (SUBLANES, LANES))] * len(starts)
        loaded = {}
        for k in range(CONV_WIDTH):
            wk = jnp.broadcast_to(dw_w_ref[k:k + 1, lanes], (SUBLANES, LANES))
            for i, r in enumerate(starts):
                src = r + base + k
                if src not in loaded:
                    loaded[src] = a_ext[s, _interleaved(src), :]
                accs[i] = accs[i] + loaded[src] * wk
        for i, r in enumerate(starts):
            conv_buf[s, _interleaved(r), :] = accs[i]

    def pool_group(gi, w):
        for t0 in range(0, tm, TIME_BLOCK):
            loaded = {}
            for m in range(TIME_STRIDE):
                for j in range(w):
                    src = t0 + POOL_HALO + m - j
                    if src not in loaded:
                        loaded[src] = u_ext[gi, _interleaved(src), :]
                tok = loaded[t0 + POOL_HALO + m]
                tot = functools.reduce(jnp.add, [loaded[t0 + POOL_HALO + m - j] for j in range(w)])
                if t0 < w - 1:
                    frame = t * tm + t0 + m + TIME_STRIDE * lax.broadcasted_iota(jnp.int32, (SUBLANES, LANES), 0)
                    mean = tot / jnp.minimum(frame + 1, w).astype(F32)
                elif w & (w - 1) == 0:
                    mean = tot * (1.0 / w)
                else:
                    mean = tot / float(w)
                dlt_buf[gi, _interleaved(t0 + m), :] = mean - tok

    mixer_stages = [functools.partial(conv_group, s, g0)
                    for s in range(n_conv) for g0 in range(0, tm, CONV_GROUP * TIME_BLOCK)]
    mixer_stages.append(pool_in)
    mixer_stages += [functools.partial(pool_group, gi, w) for gi, w in enumerate(POOL_WINDOWS)]

    ffn_iter = iter(ffn_stages)
    per = -(-len(mixer_stages) // FFN_LEAD_STAGES)
    for i in range(FFN_LEAD_STAGES):
        next(ffn_iter)()
        for stage in mixer_stages[i * per:(i + 1) * per]:
            stage()
    next(ffn_iter)()

    cs = [conv_buf[s] for s in range(n_conv)]
    mu = jnp.sum(functools.reduce(jnp.add, cs), axis=-1, keepdims=True) / d_conv
    devs = [c - mu for c in cs]
    var = jnp.sum(functools.reduce(jnp.add, [dv * dv for dv in devs]), axis=-1, keepdims=True) / d_conv
    inv = lax.rsqrt(var + EPS)
    a_out = []
    for s in range(n_conv):
        lanes = slice(s * LANES, (s + 1) * LANES)
        c = devs[s] * inv * ln_g_ref[:, lanes] + ln_b_ref[:, lanes]
        a_out.append((c * jax.nn.sigmoid(c)).astype(BF16))
    next(ffn_iter)()
    pooled = [jnp.dot(dlt_buf[gi].astype(BF16), pool_w_ref[gi], preferred_element_type=F32)
              for gi in range(n_pool)]
    p_out = (jnp.concatenate(pooled, axis=-1) * pool_s_ref[...]).astype(BF16)
    cat = jnp.concatenate(a_out + [p_out], axis=-1)
    h_new = x + jnp.dot(cat, w_out_ref[...], preferred_element_type=F32)

    for stage in ffn_iter:
        stage()
    h_prev[...] = h_new


def _layer0(x2d, g, w_in_all, dw_w, dw_b, ln_g, ln_b, pool_w, pool_s, w_out_all, gf, w1_all, w3_all, w2_all,
            *, seq, layer, mixer_index):
    n_tok, d = x2d.shape
    w_in, w_out, w1, w3, w2 = (jax.ShapeDtypeStruct(w.shape[1:], w.dtype)
                               for w in (w_in_all, w_out_all, w1_all, w3_all, w2_all))
    d_conv = dw_w.shape[1]
    d_pool = pool_s.shape[1]
    d_ff = w1.shape[1]
    tm = TOKEN_TILE
    n_tiles = n_tok // tm
    assert d_pool == len(POOL_WINDOWS) * LANES and d_conv % LANES == 0
    assert tm % (CONV_GROUP * TIME_BLOCK) == 0 and seq % tm == 0 and n_tok % seq == 0
    stages = [_stage_shape(w, LAYER0_STAGE_BYTES) for w in (w_in, w_out, w1)]
    assert stages[1] == _stage_shape(w2, LAYER0_STAGE_BYTES) and stages[2] == _stage_shape(w3, LAYER0_STAGE_BYTES)
    resident = (w_in.size + w_out.size + pool_w.size + 3 * d * d_ff) * 2 + 6 * tm * d * 4 \
        + tm * w_in.shape[1] * 4 + (2 * tm + CONV_HALO) * d_conv * 4 + (2 * tm + POOL_HALO) * d_pool * 4 \
        + 3 * tm * FFN_CHUNK * 4 + 3 * STAGE_SLOTS * LAYER0_STAGE_BYTES
    hbm = pl.BlockSpec(memory_space=pl.ANY)
    return pl.pallas_call(
        functools.partial(_layer0_kernel, tm=tm, tiles_per_seq=seq // tm, n_tiles=n_tiles,
                          layer=layer, mixer_index=mixer_index),
        grid=(n_tiles + 1,),
        in_specs=[
            pl.BlockSpec((tm, d), lambda s: (jnp.minimum(s, n_tiles - 1), 0)),
            _resident((1, d), 1),
            hbm,
            _resident(dw_w.shape, 1),
            _resident((1, d_conv), 1),
            _resident((1, d_conv), 1),
            _resident((1, d_conv), 1),
            _resident(pool_w.shape, 1),
            _resident((1, d_pool), 1),
            hbm,
            _resident((1, d), 1),
            hbm, hbm, hbm,
        ],
        out_specs=pl.BlockSpec((tm, d), lambda s: (jnp.maximum(s - 1, 0), 0)),
        out_shape=jax.ShapeDtypeStruct((n_tok, d), F32),
        scratch_shapes=[
            pltpu.VMEM((d_conv // LANES, CONV_HALO + tm, LANES), F32),
            pltpu.VMEM((d_pool // LANES, POOL_HALO + tm, LANES), F32),
            pltpu.VMEM((d_conv // LANES, tm, LANES), F32),
            pltpu.VMEM((d_pool // LANES, tm, LANES), F32),
            pltpu.VMEM((tm, d), F32),
            pltpu.VMEM(w_in.shape, BF16),
            pltpu.VMEM(w_out.shape, BF16),
            pltpu.VMEM(w1.shape, BF16),
            pltpu.VMEM(w3.shape, BF16),
            pltpu.VMEM(w2.shape, BF16),
            pltpu.VMEM(stages[0], F32),
            pltpu.VMEM(stages[1], F32),
            pltpu.VMEM(stages[2], F32),
            pltpu.SemaphoreType.DMA((STAGE_SLOTS,)),
            pltpu.SemaphoreType.DMA((STAGE_SLOTS,)),
            pltpu.SemaphoreType.DMA((STAGE_SLOTS,)),
        ],
        compiler_params=pltpu.CompilerParams(
            dimension_semantics=("arbitrary",),
            vmem_limit_bytes=_vmem_limit(resident)),
        name="layer0_convpool_ffn",
    )(x2d, g, w_in_all, dw_w, dw_b, ln_g, ln_b, pool_w, pool_s, w_out_all, gf, w1_all, w3_all, w2_all)


def _hgrn_kernel(h_ref, g_ref, w_in_hbm, lb_logits_ref, gn_ref, w_out_hbm, o_ref,
                 z_ref, y_ref, state_ref, w_in_ref, w_out_ref, stage_in, stage_out, sem_in, sem_out,
                 *, tm, layer, mixer_index):
    t = pl.program_id(1)
    d_hg = y_ref.shape[1]
    dk = d_hg // HG_HEADS

    @pl.when((pl.program_id(0) == 0) & (t == 0))
    def _():
        _fetch_weights_as_bf16([(w_in_hbm.at[mixer_index], w_in_ref, stage_in, sem_in),
                                (w_out_hbm.at[mixer_index], w_out_ref, stage_out, sem_out)])

    @pl.when(t == 0)
    def _():
        state_ref[...] = jnp.zeros(state_ref.shape, F32)

    x = h_ref[...]
    n = _rmsnorm(x, g_ref[...]).astype(BF16)
    z_ref[...] = jnp.dot(n, w_in_ref[...], preferred_element_type=F32)

    logits = lb_logits_ref[...]
    e = jnp.exp(logits - jnp.max(logits, axis=0, keepdims=True))
    p = e / jnp.sum(e, axis=0, keepdims=True)
    lb = jnp.sum(p[0:layer + 1], axis=0, keepdims=True) - p[0:1]

    row = lax.broadcasted_iota(jnp.int32, (CHUNK, CHUNK), 0)
    col = lax.broadcasted_iota(jnp.int32, (CHUNK, CHUNK), 1)
    causal = row >= col
    row3 = lax.broadcasted_iota(jnp.int32, (CHUNK, 3 * CHUNK), 0)
    col3 = lax.broadcasted_iota(jnp.int32, (CHUNK, 3 * CHUNK), 1) & (CHUNK - 1)
    tril3 = (row3 >= col3).astype(BF16)
    gn = gn_ref[...]
    mid = CHUNK // 2 - 1

    def chunk_step(c, carry):
        r0 = pl.multiple_of(c * CHUNK, CHUNK)
        rows = pl.ds(r0, CHUNK)
        q = z_ref[rows, 0:d_hg]
        f = lb + (1.0 - lb) * jax.nn.sigmoid(z_ref[rows, d_hg:2 * d_hg])
        log_f = jnp.log(f)
        k = 1.0 - f
        v = z_ref[rows, 2 * d_hg:3 * d_hg].astype(BF16)
        gate = z_ref[rows, 3 * d_hg:4 * d_hg]
        gate = gate * jax.nn.sigmoid(gate)

        hi = log_f.astype(BF16)
        rem = log_f - hi.astype(F32)
        md = rem.astype(BF16)
        lo = (rem - md.astype(F32)).astype(BF16)
        b = jnp.dot(tril3, jnp.concatenate([hi, md, lo], axis=0),
                    preferred_element_type=F32)

        b_mid = b[mid:mid + 1, :]
        b_last = b[CHUNK - 1:CHUNK, :]
        q_rel = q * jnp.exp(b - b_mid)
        k_rel = k * jnp.exp(b_mid - b)
        q_in = (q_rel * jnp.exp(b_mid)).astype(BF16)
        k_up = (k_rel * jnp.exp(b_last - b_mid)).astype(BF16)
        q_rel = q_rel.astype(BF16)
        k_rel = k_rel.astype(BF16)
        decay_last = jnp.exp(b_last)

        heads = [slice(hd * dk, (hd + 1) * dk) for hd in range(HG_HEADS)]
        nt = (((1,), (1,)), ((), ()))
        tn = (((0,), (0,)), ((), ()))
        scores = [lax.dot_general(q_rel[:, sl], k_rel[:, sl], nt, preferred_element_type=F32)
                  for sl in heads]
        state_t = [state_ref[hd] for hd in range(HG_HEADS)]
        o_inter = [lax.dot_general(q_in[:, sl], st.astype(BF16), nt, preferred_element_type=F32)
                   for sl, st in zip(heads, state_t)]
        kv = [lax.dot_general(v[:, sl], k_up[:, sl], tn, preferred_element_type=F32) for sl in heads]
        for hd, sl in enumerate(heads):
            state_ref[hd] = state_t[hd] * decay_last[:, sl] + kv[hd]
        scores = [jnp.where(causal, s, 0.0).astype(BF16) for s in scores]
        outs = [jnp.dot(s, v[:, sl], preferred_element_type=F32) + oi
                for s, sl, oi in zip(scores, heads, o_inter)]
        for o, sl in zip(outs, heads):
            o = o * lax.rsqrt(jnp.mean(o * o, axis=-1, keepdims=True) + EPS) * gn[:, sl]
            y_ref[rows, sl] = (o * gate[:, sl]).astype(BF16)
        return carry

    lax.fori_loop(0, tm // CHUNK, chunk_step, 0, unroll=True)
    o_ref[...] = x + jnp.dot(y_ref[...], w_out_ref[...], preferred_element_type=F32)


def _hgrn_mixer(h, g, w_in_all, lb_logits, gn_g, w_out_all, *, layer, mixer_index):
    bsz, seq, d = h.shape
    w_in, w_out = (jax.ShapeDtypeStruct(w.shape[1:], w.dtype) for w in (w_in_all, w_out_all))
    d_hg = w_out.shape[0]
    dk = d_hg // HG_HEADS
    tm = TOKEN_TILE
    resident = (w_in.size + w_out.size) * 2 + 4 * tm * d * 4 + tm * 4 * d_hg * 4 * 2 \
        + tm * d_hg * 2 + HG_HEADS * dk * dk * 4 + 2 * STAGE_SLOTS * HGRN_STAGE_BYTES
    hbm = pl.BlockSpec(memory_space=pl.ANY)
    return pl.pallas_call(
        functools.partial(_hgrn_kernel, tm=tm, layer=layer, mixer_index=mixer_index),
        grid=(bsz, seq // tm),
        in_specs=[
            pl.BlockSpec((None, tm, d), lambda b, t: (b, t, 0)),
            _resident((1, d), 2),
            hbm,
            _resident(lb_logits.shape, 2),
            _resident((1, d_hg), 2),
            hbm,
        ],
        out_specs=pl.BlockSpec((None, tm, d), lambda b, t: (b, t, 0)),
        out_shape=jax.ShapeDtypeStruct((bsz, seq, d), F32),
        scratch_shapes=[
            pltpu.VMEM((tm, 4 * d_hg), F32),
            pltpu.VMEM((tm, d_hg), BF16),
            pltpu.VMEM((HG_HEADS, dk, dk), F32),
            pltpu.VMEM(w_in.shape, BF16),
            pltpu.VMEM(w_out.shape, BF16),
            pltpu.VMEM(_stage_shape(w_in, HGRN_STAGE_BYTES), F32),
            pltpu.VMEM(_stage_shape(w_out, HGRN_STAGE_BYTES), F32),
            pltpu.SemaphoreType.DMA((STAGE_SLOTS,)),
            pltpu.SemaphoreType.DMA((STAGE_SLOTS,)),
        ],
        compiler_params=pltpu.CompilerParams(
            dimension_semantics=("arbitrary", "arbitrary"),
            vmem_limit_bytes=_vmem_limit(resident)),
        name="hgrn_mixer",
    )(h, g, w_in_all, lb_logits, gn_g, w_out_all)


def kernel(x, norm_mix_g, norm_ffn_g, final_g, cp_w_in, cp_dw_w, cp_dw_b, cp_ln_g, cp_ln_b, cp_pool_w,
           cp_pool_scale, cp_w_out, hg_w_in, hg_lb_logits, hg_gn_g, hg_w_out, ffn_w1, ffn_w3, ffn_w2):
    bsz, seq, d = x.shape
    depth = norm_mix_g.shape[0]
    row = lambda a: a.reshape(1, -1).astype(F32)
    h = x
    for layer in range(depth):
        j = layer // 2
        last = layer == depth - 1
        if layer % 2 == 0:
            assert not last, "the fused conv/pool + FFN layer kernel has no final norm"
            h = _layer0(
                h.reshape(bsz * seq, d), row(norm_mix_g[layer]), cp_w_in, cp_dw_w[j],
                row(cp_dw_b[j]), row(cp_ln_g[j]), row(cp_ln_b[j]), cp_pool_w[j].astype(BF16),
                row(cp_pool_scale[j]), cp_w_out, row(norm_ffn_g[layer]), ffn_w1, ffn_w3, ffn_w2,
                seq=seq, layer=layer, mixer_index=j).reshape(bsz, seq, d)
        else:
            h = _hgrn_mixer(
                h, row(norm_mix_g[layer]), hg_w_in, hg_lb_logits.astype(F32),
                row(hg_gn_g[j]), hg_w_out, layer=layer, mixer_index=j)
            h = _ffn_block(
                h.reshape(bsz * seq, d), row(norm_ffn_g[layer]), ffn_w1, ffn_w3, ffn_w2, row(final_g),
                layer=layer, final_norm=last).reshape(bsz, seq, d)
    return h
```

```python
import functools
import itertools

import jax
import jax.numpy as jnp
from jax import lax
from jax.experimental import pallas as pl
from jax.experimental.pallas import tpu as pltpu

F32 = jnp.float32
BF16 = jnp.bfloat16

EPS = 1e-6
CHUNK = 64
CONV_WIDTH = 31
POOL_WINDOWS = (2, 4, 8, 16)
HG_HEADS = 8

V7X_VMEM_BYTES = 64 * 1024 * 1024
SUBLANES = 8
LANES = 128

TOKEN_TILE = 512
CONV_HALO = 32
POOL_HALO = 16
TIME_STRIDE = 4
TIME_BLOCK = SUBLANES * TIME_STRIDE
CONV_ACCS = 2
FFN_CHUNK = 512
STAGE_SLOTS = 3
FFN_STAGE_BYTES = 2 << 20
HGRN_STAGE_BYTES = 1 << 20
LAYER0_STAGE_BYTES = 1 << 19


def _interleaved(start):
    return pl.ds(start, SUBLANES, stride=TIME_STRIDE)


def _vmem_limit(resident_bytes):
    return int(min(resident_bytes * 1.5 + (8 << 20), V7X_VMEM_BYTES - (6 << 20)))


def _rmsnorm(x, g):
    return x * lax.rsqrt(jnp.mean(x * x, axis=-1, keepdims=True) + EPS) * g


def _resident(shape, ngrid):
    zeros = (0,) * len(shape)
    if ngrid == 1:
        index_map = lambda i: zeros
    else:
        index_map = lambda b, t: zeros
    return pl.BlockSpec(shape, index_map, pipeline_mode=pl.Buffered(1))


def _stage_shape(w, stage_bytes):
    n_rows, n_cols = w.shape
    pack = 2 * SUBLANES
    rows = max(pack, (stage_bytes // (4 * n_cols)) // pack * pack)
    while n_rows % rows:
        rows -= pack
    return (STAGE_SLOTS, rows, n_cols)


def _fetch_weights_as_bf16(jobs):
    queues = {}
    for src, dst, stage, sem in jobs:
        queue = queues.setdefault(id(stage), [])
        rows = stage.shape[1]
        for i in range(dst.shape[0] // rows):
            queue.append((src, dst, stage, sem, i * rows, rows, len(queue) % stage.shape[0]))
    chunks = [c for group in itertools.zip_longest(*queues.values()) for c in group if c is not None]

    def copy(c):
        src, _, stage, sem, r0, rows, slot = c
        return pltpu.make_async_copy(src.at[pl.ds(r0, rows)], stage.at[slot], sem.at[slot])

    starts_after = [[] for _ in chunks]
    last_in_slot = {}
    for k, c in enumerate(chunks):
        key = (id(c[2]), c[6])
        if key in last_in_slot:
            starts_after[last_in_slot[key]].append(k)
        else:
            copy(c).start()
        last_in_slot[key] = k
    for k, c in enumerate(chunks):
        copy(c).wait()
        _, dst, stage, _, r0, rows, slot = c
        dst[pl.ds(r0, rows), :] = stage[slot].astype(BF16)
        for nxt in starts_after[k]:
            copy(chunks[nxt]).start()


def _ffn_kernel(h_ref, g_ref, w1_hbm, w3_hbm, w2_hbm, fg_ref, o_ref,
                w1_ref, w3_ref, w2_ref, stage_in, stage_out, sem_in, sem_out, *, layer, final_norm):
    @pl.when(pl.program_id(0) == 0)
    def _():
        _fetch_weights_as_bf16([(w1_hbm.at[layer], w1_ref, stage_in, sem_in),
                                (w3_hbm.at[layer], w3_ref, stage_in, sem_in),
                                (w2_hbm.at[layer], w2_ref, stage_out, sem_out)])

    h = h_ref[...]
    n = _rmsnorm(h, g_ref[...]).astype(BF16)
    a = jnp.dot(n, w1_ref[...], preferred_element_type=F32)
    b = jnp.dot(n, w3_ref[...], preferred_element_type=F32)
    gated = (a * jax.nn.sigmoid(a) * b).astype(BF16)
    out = h + jnp.dot(gated, w2_ref[...], preferred_element_type=F32)
    if final_norm:
        out = _rmsnorm(out, fg_ref[...])
    o_ref[...] = out


def _ffn_block(h2d, g, w1_all, w3_all, w2_all, final_g, *, layer, final_norm):
    n_tok, d = h2d.shape
    w1, w3, w2 = (jax.ShapeDtypeStruct(w.shape[1:], w.dtype) for w in (w1_all, w3_all, w2_all))
    d_ff = w1.shape[1]
    tm = TOKEN_TILE
    stages = [_stage_shape(w1, FFN_STAGE_BYTES), _stage_shape(w2, FFN_STAGE_BYTES)]
    assert stages[0] == _stage_shape(w3, FFN_STAGE_BYTES)
    resident = 3 * d * d_ff * 2 + 4 * tm * d * 4 + 3 * tm * d_ff * 4 + 2 * STAGE_SLOTS * FFN_STAGE_BYTES
    hbm = pl.BlockSpec(memory_space=pl.ANY)
    return pl.pallas_call(
        functools.partial(_ffn_kernel, layer=layer, final_norm=final_norm),
        grid=(n_tok // tm,),
        in_specs=[
            pl.BlockSpec((tm, d), lambda i: (i, 0)),
            _resident((1, d), 1),
            hbm, hbm, hbm,
            _resident((1, d), 1),
        ],
        out_specs=pl.BlockSpec((tm, d), lambda i: (i, 0)),
        out_shape=jax.ShapeDtypeStruct((n_tok, d), F32),
        scratch_shapes=[
            pltpu.VMEM(w1.shape, BF16),
            pltpu.VMEM(w3.shape, BF16),
            pltpu.VMEM(w2.shape, BF16),
            pltpu.VMEM(stages[0], F32),
            pltpu.VMEM(stages[1], F32),
            pltpu.SemaphoreType.DMA((STAGE_SLOTS,)),
            pltpu.SemaphoreType.DMA((STAGE_SLOTS,)),
        ],
        compiler_params=pltpu.CompilerParams(
            dimension_semantics=("arbitrary",),
            vmem_limit_bytes=_vmem_limit(resident)),
        name="ffn_block",
    )(h2d, g, w1_all, w3_all, w2_all, final_g)


def _layer0_kernel(x_ref, g_ref, w_in_hbm, dw_w_ref, dw_b_ref, ln_g_ref, ln_b_ref, pool_w_ref, pool_s_ref,
                   w_out_hbm, gf_ref, w1_hbm, w3_hbm, w2_hbm, o_ref,
                   a_ext, u_ext, conv_buf, dlt_buf, h_prev, gated_ref, w_in_ref, w_out_ref, w1_ref, w3_ref, w2_ref,
                   stage_in, stage_d, stage_ff, sem_in, sem_d, sem_ff,
                   *, tm, tiles_per_seq, n_tiles, layer, mixer_index):
    step = pl.program_id(0)
    t = lax.rem(jnp.minimum(step, n_tiles - 1), tiles_per_seq)
    n_conv = a_ext.shape[0]
    n_pool = u_ext.shape[0]
    d_conv = n_conv * LANES
    d_ff = w1_ref.shape[1]

    @pl.when(step == 0)
    def _():
        h_prev[...] = jnp.zeros(h_prev.shape, F32)
        a_ext[:, tm:tm + CONV_HALO, :] = jnp.zeros((n_conv, CONV_HALO, LANES), F32)
        u_ext[:, tm:tm + POOL_HALO, :] = jnp.zeros((n_pool, POOL_HALO, LANES), F32)
        _fetch_weights_as_bf16([(w_in_hbm.at[mixer_index], w_in_ref, stage_in, sem_in),
                                (w_out_hbm.at[mixer_index], w_out_ref, stage_d, sem_d),
                                (w2_hbm.at[layer], w2_ref, stage_d, sem_d),
                                (w1_hbm.at[layer], w1_ref, stage_ff, sem_ff),
                                (w3_hbm.at[layer], w3_ref, stage_ff, sem_ff)])

    hp = h_prev[...]
    nf = _rmsnorm(hp, gf_ref[...]).astype(BF16)
    ff_chunks = [(c0, min(c0 + FFN_CHUNK, d_ff)) for c0 in range(0, d_ff, FFN_CHUNK)]
    gate_in, up_in = {}, {}

    def ffn_gate(j):
        c0, c1 = ff_chunks[j]
        gate_in[j] = jnp.dot(nf, w1_ref[:, c0:c1], preferred_element_type=F32)

    def ffn_up(j):
        c0, c1 = ff_chunks[j]
        up_in[j] = jnp.dot(nf, w3_ref[:, c0:c1], preferred_element_type=F32)

    def ffn_act(j):
        c0, c1 = ff_chunks[j]
        a, b = gate_in.pop(j), up_in.pop(j)
        gated_ref[:, c0:c1] = (a * jax.nn.sigmoid(a) * b).astype(BF16)

    def ffn_out():
        o_ref[...] = hp + jnp.dot(gated_ref[...], w2_ref[...], preferred_element_type=F32)

    ffn_stages = []
    for j in range(len(ff_chunks)):
        ffn_stages += [functools.partial(ffn_gate, j), functools.partial(ffn_up, j), functools.partial(ffn_act, j)]
    ffn_stages.append(ffn_out)
    ffn_iter = iter(ffn_stages)
    next(ffn_iter)()

    x = x_ref[...]
    n = _rmsnorm(x, g_ref[...]).astype(BF16)
    z = jnp.dot(n, w_in_ref[:, :2 * d_conv], preferred_element_type=F32)

    inside = t > 0
    a_ext[:, 0:CONV_HALO, :] = jnp.where(inside, a_ext[:, tm:tm + CONV_HALO, :], 0.0)
    u_ext[:, 0:POOL_HALO, :] = jnp.where(inside, u_ext[:, tm:tm + POOL_HALO, :], 0.0)

    glu = z[:, :d_conv] * jax.nn.sigmoid(z[:, d_conv:2 * d_conv])
    for s in range(n_conv):
        a_ext[s, CONV_HALO:CONV_HALO + tm, :] = glu[:, s * LANES:(s + 1) * LANES]

    def pool_in():
        u = jnp.dot(n, w_in_ref[:, 2 * d_conv:], preferred_element_type=F32)
        for s in range(n_pool):
            u_ext[s, POOL_HALO:POOL_HALO + tm, :] = u[:, s * LANES:(s + 1) * LANES]

    base = CONV_HALO - (CONV_WIDTH - 1)

    always = step >= 0
    chain = []

    def conv_group(s, starts):
        lanes = slice(s * LANES, (s + 1) * LANES)
        bias = jnp.broadcast_to(dw_b_ref[:, lanes], (SUBLANES, LANES))
        if chain:
            bias = jnp.where(always, bias, chain[-1])
        accs = [bias] * len(starts)
        loaded = {}
        for k in range(CONV_WIDTH):
            wk = jnp.broadcast_to(dw_w_ref[k:k + 1, lanes], (SUBLANES, LANES))
            for i, r in enumerate(starts):
                src = r + base + k
                if src not in loaded:
                    loaded[src] = a_ext[s, _interleaved(src), :]
                accs[i] = accs[i] + loaded[src] * wk
        for i, r in enumerate(starts):
            conv_buf[s, _interleaved(r), :] = accs[i]
        chain.append(accs[-1])

    def pool_group(gi, w):
        for t0 in range(0, tm, TIME_BLOCK):
            loaded = {}
            for m in range(TIME_STRIDE):
                for j in range(w):
                    src = t0 + POOL_HALO + m - j
                    if src not in loaded:
                        loaded[src] = u_ext[gi, _interleaved(src), :]
                tok = loaded[t0 + POOL_HALO + m]
                tot = functools.reduce(jnp.add, [loaded[t0 + POOL_HALO + m - j] for j in range(w)])
                if t0 < w - 1:
                    frame = t * tm + t0 + m + TIME_STRIDE * lax.broadcasted_iota(jnp.int32, (SUBLANES, LANES), 0)
                    mean = tot / jnp.minimum(frame + 1, w).astype(F32)
                elif w & (w - 1) == 0:
                    mean = tot * (1.0 / w)
                else:
                    mean = tot / float(w)
                dlt_buf[gi, _interleaved(t0 + m), :] = mean - tok

    vreg_starts = [t0 + m for t0 in range(0, tm, TIME_BLOCK) for m in range(TIME_STRIDE)]
    mixer_stages = [functools.partial(conv_group, s, vreg_starts[i:i + CONV_ACCS])
                    for s in range(n_conv) for i in range(0, len(vreg_starts), CONV_ACCS)]
    mixer_stages.append(pool_in)
    mixer_stages += [functools.partial(pool_group, gi, w) for gi, w in enumerate(POOL_WINDOWS)]

    ffn_rest = list(ffn_iter)
    per = -(-len(mixer_stages) // len(ffn_rest))
    for i, ffn_stage in enumerate(ffn_rest):
        for stage in mixer_stages[i * per:(i + 1) * per]:
            stage()
        ffn_stage()

    cs = [conv_buf[s] for s in range(n_conv)]
    mu = jnp.sum(functools.reduce(jnp.add, cs), axis=-1, keepdims=True) / d_conv
    devs = [c - mu for c in cs]
    var = jnp.sum(functools.reduce(jnp.add, [dv * dv for dv in devs]), axis=-1, keepdims=True) / d_conv
    inv = lax.rsqrt(var + EPS)
    a_out = []
    for s in range(n_conv):
        lanes = slice(s * LANES, (s + 1) * LANES)
        c = devs[s] * inv * ln_g_ref[:, lanes] + ln_b_ref[:, lanes]
        a_out.append((c * jax.nn.sigmoid(c)).astype(BF16))
    pooled =[jnp.dot(dlt_buf[gi].astype(BF16), pool_w_ref[gi], preferred_element_type=F32)
              for gi in range(n_pool)]
    p_out = (jnp.concatenate(pooled, axis=-1) * pool_s_ref[...]).astype(BF16)
    cat = jnp.concatenate(a_out + [p_out], axis=-1)
    h_prev[...] = x + jnp.dot(cat, w_out_ref[...], preferred_element_type=F32)


def _layer0(x2d, g, w_in_all, dw_w, dw_b, ln_g, ln_b, pool_w, pool_s, w_out_all, gf, w1_all, w3_all, w2_all,
            *, seq, layer, mixer_index):
    n_tok, d = x2d.shape
    w_in, w_out, w1, w3, w2 = (jax.ShapeDtypeStruct(w.shape[1:], w.dtype)
                               for w in (w_in_all, w_out_all, w1_all, w3_all, w2_all))
    d_conv = dw_w.shape[1]
    d_pool = pool_s.shape[1]
    d_ff = w1.shape[1]
    tm = TOKEN_TILE
    n_tiles = n_tok // tm
    assert d_pool == len(POOL_WINDOWS) * LANES and d_conv % LANES == 0
    assert tm % TIME_BLOCK == 0 and TIME_STRIDE % CONV_ACCS == 0 and seq % tm == 0 and n_tok % seq == 0
    stages = [_stage_shape(w, LAYER0_STAGE_BYTES) for w in (w_in, w_out, w1)]
    assert stages[1] == _stage_shape(w2, LAYER0_STAGE_BYTES) and stages[2] == _stage_shape(w3, LAYER0_STAGE_BYTES)
    resident = (w_in.size + w_out.size + pool_w.size + 3 * d * d_ff) * 2 + 6 * tm * d * 4 \
        + tm * w_in.shape[1] * 4 + (2 * tm + CONV_HALO) * d_conv * 4 + (2 * tm + POOL_HALO) * d_pool * 4 \
        + 3 * tm * FFN_CHUNK * 4 + 3 * STAGE_SLOTS * LAYER0_STAGE_BYTES
    hbm = pl.BlockSpec(memory_space=pl.ANY)
    return pl.pallas_call(
        functools.partial(_layer0_kernel, tm=tm, tiles_per_seq=seq // tm, n_tiles=n_tiles,
                          layer=layer, mixer_index=mixer_index),
        grid=(n_tiles + 1,),
        in_specs=[
            pl.BlockSpec((tm, d), lambda s: (jnp.minimum(s, n_tiles - 1), 0)),
            _resident((1, d), 1),
            hbm,
            _resident(dw_w.shape, 1),
            _resident((1, d_conv), 1),
            _resident((1, d_conv), 1),
            _resident((1, d_conv), 1),
            _resident(pool_w.shape, 1),
            _resident((1, d_pool), 1),
            hbm,
            _resident((1, d), 1),
            hbm, hbm, hbm,
        ],
        out_specs=pl.BlockSpec((tm, d), lambda s: (jnp.maximum(s - 1, 0), 0)),
        out_shape=jax.ShapeDtypeStruct((n_tok, d), F32),
        scratch_shapes=[
            pltpu.VMEM((d_conv // LANES, CONV_HALO + tm, LANES), F32),
            pltpu.VMEM((d_pool // LANES, POOL_HALO + tm, LANES), F32),
            pltpu.VMEM((d_conv // LANES, tm, LANES), F32),
            pltpu.VMEM((d_pool // LANES, tm, LANES), F32),
            pltpu.VMEM((tm, d), F32),
            pltpu.VMEM((tm, d_ff), BF16),
            pltpu.VMEM(w_in.shape, BF16),
            pltpu.VMEM(w_out.shape, BF16),
            pltpu.VMEM(w1.shape, BF16),
            pltpu.VMEM(w3.shape, BF16),
            pltpu.VMEM(w2.shape, BF16),
            pltpu.VMEM(stages[0], F32),
            pltpu.VMEM(stages[1], F32),
            pltpu.VMEM(stages[2], F32),
            pltpu.SemaphoreType.DMA((STAGE_SLOTS,)),
            pltpu.SemaphoreType.DMA((STAGE_SLOTS,)),
            pltpu.SemaphoreType.DMA((STAGE_SLOTS,)),
        ],
        compiler_params=pltpu.CompilerParams(
            dimension_semantics=("arbitrary",),
            vmem_limit_bytes=_vmem_limit(resident)),
        name="layer0_convpool_ffn",
    )(x2d, g, w_in_all, dw_w, dw_b, ln_g, ln_b, pool_w, pool_s, w_out_all, gf, w1_all, w3_all, w2_all)


def _hgrn_kernel(h_ref, g_ref, w_in_hbm, lb_logits_ref, gn_ref, w_out_hbm, o_ref,
                 z_ref, y_ref, state_ref, w_in_ref, w_out_ref, stage_in, stage_out, sem_in, sem_out,
                 *, tm, layer, mixer_index):
    t = pl.program_id(1)
    d_hg = y_ref.shape[1]
    dk = d_hg // HG_HEADS

    @pl.when((pl.program_id(0) == 0) & (t == 0))
    def _():
        _fetch_weights_as_bf16([(w_in_hbm.at[mixer_index], w_in_ref, stage_in, sem_in),
                                (w_out_hbm.at[mixer_index], w_out_ref, stage_out, sem_out)])

    @pl.when(t == 0)
    def _():
        state_ref[...] = jnp.zeros(state_ref.shape, F32)

    x = h_ref[...]
    n = _rmsnorm(x, g_ref[...]).astype(BF16)
    z_ref[...] = jnp.dot(n, w_in_ref[...], preferred_element_type=F32)

    logits = lb_logits_ref[...]
    e = jnp.exp(logits - jnp.max(logits, axis=0, keepdims=True))
    p = e / jnp.sum(e, axis=0, keepdims=True)
    lb = jnp.sum(p[0:layer + 1], axis=0, keepdims=True) - p[0:1]

    row = lax.broadcasted_iota(jnp.int32, (CHUNK, CHUNK), 0)
    col = lax.broadcasted_iota(jnp.int32, (CHUNK, CHUNK), 1)
    causal = row >= col
    row3 = lax.broadcasted_iota(jnp.int32, (CHUNK, 3 * CHUNK), 0)
    col3 = lax.broadcasted_iota(jnp.int32, (CHUNK, 3 * CHUNK), 1) & (CHUNK - 1)
    tril3 = (row3 >= col3).astype(BF16)
    gn = gn_ref[...]
    mid = CHUNK // 2 - 1

    def chunk_step(c, carry):
        r0 = pl.multiple_of(c * CHUNK, CHUNK)
        rows = pl.ds(r0, CHUNK)
        q = z_ref[rows, 0:d_hg]
        f = lb + (1.0 - lb) * jax.nn.sigmoid(z_ref[rows, d_hg:2 * d_hg])
        log_f = jnp.log(f)
        k = 1.0 - f
        v = z_ref[rows, 2 * d_hg:3 * d_hg].astype(BF16)
        gate = z_ref[rows, 3 * d_hg:4 * d_hg]
        gate = gate * jax.nn.sigmoid(gate)

        hi = log_f.astype(BF16)
        rem = log_f - hi.astype(F32)
        md = rem.astype(BF16)
        lo = (rem - md.astype(F32)).astype(BF16)
        b = jnp.dot(tril3, jnp.concatenate([hi, md, lo], axis=0),
                    preferred_element_type=F32)

        b_mid = b[mid:mid + 1, :]
        b_last = b[CHUNK - 1:CHUNK, :]
        q_rel = q * jnp.exp(b - b_mid)
        k_rel = k * jnp.exp(b_mid - b)
        q_in = (q_rel * jnp.exp(b_mid)).astype(BF16)
        k_up = (k_rel * jnp.exp(b_last - b_mid)).astype(BF16)
        q_rel = q_rel.astype(BF16)
        k_rel = k_rel.astype(BF16)
        decay_last = jnp.exp(b_last)

        heads = [slice(hd * dk, (hd + 1) * dk) for hd in range(HG_HEADS)]
        nt = (((1,), (1,)), ((), ()))
        tn = (((0,), (0,)), ((), ()))
        scores = [lax.dot_general(q_rel[:, sl], k_rel[:, sl], nt, preferred_element_type=F32)
                  for sl in heads]
        state_t = [state_ref[hd] for hd in range(HG_HEADS)]
        o_inter = [lax.dot_general(q_in[:, sl], st.astype(BF16), nt, preferred_element_type=F32)
                   for sl, st in zip(heads, state_t)]
        kv = [lax.dot_general(v[:, sl], k_up[:, sl], tn, preferred_element_type=F32) for sl in heads]
        for hd, sl in enumerate(heads):
            state_ref[hd] = state_t[hd] * decay_last[:, sl] + kv[hd]
        scores = [jnp.where(causal, s, 0.0).astype(BF16) for s in scores]
        outs = [jnp.dot(s, v[:, sl], preferred_element_type=F32) + oi
                for s, sl, oi in zip(scores, heads, o_inter)]
        for o, sl in zip(outs, heads):
            o = o * lax.rsqrt(jnp.mean(o * o, axis=-1, keepdims=True) + EPS) * gn[:, sl]
            y_ref[rows, sl] = (o * gate[:, sl]).astype(BF16)
        return carry

    lax.fori_loop(0, tm // CHUNK, chunk_step, 0, unroll=True)
    o_ref[...] = x + jnp.dot(y_ref[...], w_out_ref[...], preferred_element_type=F32)


def _hgrn_mixer(h, g, w_in_all, lb_logits, gn_g, w_out_all, *, layer, mixer_index):
    bsz, seq, d = h.shape
    w_in, w_out = (jax.ShapeDtypeStruct(w.shape[1:], w.dtype) for w in (w_in_all, w_out_all))
    d_hg = w_out.shape[0]
    dk = d_hg // HG_HEADS
    tm = TOKEN_TILE
    resident = (w_in.size + w_out.size) * 2 + 4 * tm * d * 4 + tm * 4 * d_hg * 4 * 2 \
        + tm * d_hg * 2 + HG_HEADS * dk * dk * 4 + 2 * STAGE_SLOTS * HGRN_STAGE_BYTES
    hbm = pl.BlockSpec(memory_space=pl.ANY)
    return pl.pallas_call(
        functools.partial(_hgrn_kernel, tm=tm, layer=layer, mixer_index=mixer_index),
        grid=(bsz, seq // tm),
        in_specs=[
            pl.BlockSpec((None, tm, d), lambda b, t: (b, t, 0)),
            _resident((1, d), 2),
            hbm,
            _resident(lb_logits.shape, 2),
            _resident((1, d_hg), 2),
            hbm,
        ],
        out_specs=pl.BlockSpec((None, tm, d), lambda b, t: (b, t, 0)),
        out_shape=jax.ShapeDtypeStruct((bsz, seq, d), F32),
        scratch_shapes=[
            pltpu.VMEM((tm, 4 * d_hg), F32),
            pltpu.VMEM((tm, d_hg), BF16),
            pltpu.VMEM((HG_HEADS, dk, dk), F32),
            pltpu.VMEM(w_in.shape, BF16),
            pltpu.VMEM(w_out.shape, BF16),
            pltpu.VMEM(_stage_shape(w_in, HGRN_STAGE_BYTES), F32),
            pltpu.VMEM(_stage_shape(w_out, HGRN_STAGE_BYTES), F32),
            pltpu.SemaphoreType.DMA((STAGE_SLOTS,)),
            pltpu.SemaphoreType.DMA((STAGE_SLOTS,)),
        ],
        compiler_params=pltpu.CompilerParams(
            dimension_semantics=("arbitrary", "arbitrary"),
            vmem_limit_bytes=_vmem_limit(resident)),
        name="hgrn_mixer",
    )(h, g, w_in_all, lb_logits, gn_g, w_out_all)


def kernel(x, norm_mix_g, norm_ffn_g, final_g, cp_w_in, cp_dw_w, cp_dw_b, cp_ln_g, cp_ln_b, cp_pool_w,
           cp_pool_scale, cp_w_out, hg_w_in, hg_lb_logits, hg_gn_g, hg_w_out, ffn_w1, ffn_w3, ffn_w2):
    bsz, seq, d = x.shape
    depth = norm_mix_g.shape[0]
    row = lambda a: a.reshape(1, -1).astype(F32)
    h = x
    for layer in range(depth):
        j = layer // 2
        last = layer == depth - 1
        if layer % 2 == 0:
            assert not last, "the fused conv/pool + FFN layer kernel has no final norm"
            h = _layer0(
                h.reshape(bsz * seq, d), row(norm_mix_g[layer]), cp_w_in, cp_dw_w[j],
                row(cp_dw_b[j]), row(cp_ln_g[j]), row(cp_ln_b[j]), cp_pool_w[j].astype(BF16),
                row(cp_pool_scale[j]), cp_w_out, row(norm_ffn_g[layer]), ffn_w1, ffn_w3, ffn_w2,
                seq=seq, layer=layer, mixer_index=j).reshape(bsz, seq, d)
        else:
            h = _hgrn_mixer(
                h, row(norm_mix_g[layer]), hg_w_in, hg_lb_logits.astype(F32),
                row(hg_gn_g[j]), hg_w_out, layer=layer, mixer_index=j)
            h = _ffn_block(
                h.reshape(bsz * seq, d), row(norm_ffn_g[layer]), ffn_w1, ffn_w3, ffn_w2, row(final_g),
                layer=layer, final_norm=last).reshape(bsz, seq, d)
    return h
```

```python
import functools
import itertools

import jax
import jax.numpy as jnp
from jax import lax
from jax.experimental import pallas as pl
from jax.experimental.pallas import tpu as pltpu

F32 = jnp.float32
BF16 = jnp.bfloat16

EPS = 1e-6
CHUNK = 64
CONV_WIDTH = 31
POOL_WINDOWS = (2, 4, 8, 16)
HG_HEADS = 8

V7X_VMEM_BYTES = 64 * 1024 * 1024
SUBLANES = 8
LANES = 128

TOKEN_TILE = 512
CONV_HALO = 32
POOL_HALO = 16
TIME_STRIDE = 4
TIME_BLOCK = SUBLANES * TIME_STRIDE
CONV_ACCS = 2
FFN_CHUNK = 512
STAGE_SLOTS = 3
FFN_STAGE_BYTES = 2 << 20
HGRN_STAGE_BYTES = 1 << 20
LAYER0_STAGE_BYTES = 1 << 19


def _interleaved(start):
    return pl.ds(start, SUBLANES, stride=TIME_STRIDE)


def _vmem_limit(resident_bytes):
    return int(min(resident_bytes * 1.5 + (8 << 20), V7X_VMEM_BYTES - (6 << 20)))


def _rmsnorm(x, g):
    return x * lax.rsqrt(jnp.mean(x * x, axis=-1, keepdims=True) + EPS) * g


def _resident(shape, ngrid):
    zeros = (0,) * len(shape)
    if ngrid == 1:
        index_map = lambda i: zeros
    else:
        index_map = lambda b, t: zeros
    return pl.BlockSpec(shape, index_map, pipeline_mode=pl.Buffered(1))


def _stage_shape(w, stage_bytes):
    n_rows, n_cols = w.shape
    pack = 2 * SUBLANES
    rows = max(pack, (stage_bytes // (4 * n_cols)) // pack * pack)
    while n_rows % rows:
        rows -= pack
    return (STAGE_SLOTS, rows, n_cols)


def _fetch_weights_as_bf16(jobs):
    queues = {}
    for src, dst, stage, sem in jobs:
        queue = queues.setdefault(id(stage), [])
        rows = stage.shape[1]
        for i in range(dst.shape[0] // rows):
            queue.append((src, dst, stage, sem, i * rows, rows, len(queue) % stage.shape[0]))
    chunks = [c for group in itertools.zip_longest(*queues.values()) for c in group if c is not None]

    def copy(c):
        src, _, stage, sem, r0, rows, slot = c
        return pltpu.make_async_copy(src.at[pl.ds(r0, rows)], stage.at[slot], sem.at[slot])

    starts_after = [[] for _ in chunks]
    last_in_slot = {}
    for k, c in enumerate(chunks):
        key = (id(c[2]), c[6])
        if key in last_in_slot:
            starts_after[last_in_slot[key]].append(k)
        else:
            copy(c).start()
        last_in_slot[key] = k
    for k, c in enumerate(chunks):
        copy(c).wait()
        _, dst, stage, _, r0, rows, slot = c
        dst[pl.ds(r0, rows), :] = stage[slot].astype(BF16)
        for nxt in starts_after[k]:
            copy(chunks[nxt]).start()


def _ffn_kernel(h_ref, g_ref, w1_hbm, w3_hbm, w2_hbm, fg_ref, o_ref,
                w1_ref, w3_ref, w2_ref, stage_in, stage_out, sem_in, sem_out, *, layer, final_norm):
    @pl.when(pl.program_id(0) == 0)
    def _():
        _fetch_weights_as_bf16([(w1_hbm.at[layer], w1_ref, stage_in, sem_in),
                                (w3_hbm.at[layer], w3_ref, stage_in, sem_in),
                                (w2_hbm.at[layer], w2_ref, stage_out, sem_out)])

    h = h_ref[...]
    n = _rmsnorm(h, g_ref[...]).astype(BF16)
    a = jnp.dot(n, w1_ref[...], preferred_element_type=F32)
    b = jnp.dot(n, w3_ref[...], preferred_element_type=F32)
    gated = (a * jax.nn.sigmoid(a) * b).astype(BF16)
    out = h + jnp.dot(gated, w2_ref[...], preferred_element_type=F32)
    if final_norm:
        out = _rmsnorm(out, fg_ref[...])
    o_ref[...] = out


def _ffn_block(h2d, g, w1_all, w3_all, w2_all, final_g, *, layer, final_norm):
    n_tok, d = h2d.shape
    w1, w3, w2 = (jax.ShapeDtypeStruct(w.shape[1:], w.dtype) for w in (w1_all, w3_all, w2_all))
    d_ff = w1.shape[1]
    tm = TOKEN_TILE
    stages = [_stage_shape(w1, FFN_STAGE_BYTES), _stage_shape(w2, FFN_STAGE_BYTES)]
    assert stages[0] == _stage_shape(w3, FFN_STAGE_BYTES)
    resident = 3 * d * d_ff * 2 + 4 * tm * d * 4 + 3 * tm * d_ff * 4 + 2 * STAGE_SLOTS * FFN_STAGE_BYTES
    hbm = pl.BlockSpec(memory_space=pl.ANY)
    return pl.pallas_call(
        functools.partial(_ffn_kernel, layer=layer, final_norm=final_norm),
        grid=(n_tok // tm,),
        in_specs=[
            pl.BlockSpec((tm, d), lambda i: (i, 0)),
            _resident((1, d), 1),
            hbm, hbm, hbm,
            _resident((1, d), 1),
        ],
        out_specs=pl.BlockSpec((tm, d), lambda i: (i, 0)),
        out_shape=jax.ShapeDtypeStruct((n_tok, d), F32),
        scratch_shapes=[
            pltpu.VMEM(w1.shape, BF16),
            pltpu.VMEM(w3.shape, BF16),
            pltpu.VMEM(w2.shape, BF16),
            pltpu.VMEM(stages[0], F32),
            pltpu.VMEM(stages[1], F32),
            pltpu.SemaphoreType.DMA((STAGE_SLOTS,)),
            pltpu.SemaphoreType.DMA((STAGE_SLOTS,)),
        ],
        compiler_params=pltpu.CompilerParams(
            dimension_semantics=("arbitrary",),
            vmem_limit_bytes=_vmem_limit(resident)),
        name="ffn_block",
    )(h2d, g, w1_all, w3_all, w2_all, final_g)


def _layer0_kernel(x_ref, g_ref, w_in_hbm, dw_w_ref, dw_b_ref, ln_g_ref, ln_b_ref, pool_w_ref, pool_s_ref,
                   w_out_hbm, gf_ref, w1_hbm, w3_hbm, w2_hbm, o_ref,
                   a_ext, u_ext, conv_buf, dlt_buf, h_prev, gated_ref, w_in_ref, w_out_ref, w1_ref, w3_ref, w2_ref,
                   stage_in, stage_d, stage_ff, sem_in, sem_d, sem_ff,
                   *, tm, tiles_per_seq, n_tiles, layer, mixer_index):
    step = pl.program_id(0)
    t = lax.rem(jnp.minimum(step, n_tiles - 1), tiles_per_seq)
    n_conv = a_ext.shape[0]
    n_pool = u_ext.shape[0]
    d_conv = n_conv * LANES
    d_ff = w1_ref.shape[1]

    @pl.when(step == 0)
    def _():
        h_prev[...] = jnp.zeros(h_prev.shape, F32)
        a_ext[:, tm:tm + CONV_HALO, :] = jnp.zeros((n_conv, CONV_HALO, LANES), F32)
        u_ext[:, tm:tm + POOL_HALO, :] = jnp.zeros((n_pool, POOL_HALO, LANES), F32)
        _fetch_weights_as_bf16([(w_in_hbm.at[mixer_index], w_in_ref, stage_in, sem_in),
                                (w_out_hbm.at[mixer_index], w_out_ref, stage_d, sem_d),
                                (w2_hbm.at[layer], w2_ref, stage_d, sem_d),
                                (w1_hbm.at[layer], w1_ref, stage_ff, sem_ff),
                                (w3_hbm.at[layer], w3_ref, stage_ff, sem_ff)])

    hp = h_prev[...]
    nf = _rmsnorm(hp, gf_ref[...]).astype(BF16)
    ff_chunks = [(c0, min(c0 + FFN_CHUNK, d_ff)) for c0 in range(0, d_ff, FFN_CHUNK)]
    gate_in, up_in = {}, {}

    def ffn_gate(j):
        c0, c1 = ff_chunks[j]
        gate_in[j] = jnp.dot(nf, w1_ref[:, c0:c1], preferred_element_type=F32)

    def ffn_up(j):
        c0, c1 = ff_chunks[j]
        up_in[j] = jnp.dot(nf, w3_ref[:, c0:c1], preferred_element_type=F32)

    def ffn_act(j):
        c0, c1 = ff_chunks[j]
        a, b = gate_in.pop(j), up_in.pop(j)
        gated_ref[:, c0:c1] = (a * jax.nn.sigmoid(a) * b).astype(BF16)

    def ffn_out():
        o_ref[...] = hp + jnp.dot(gated_ref[...], w2_ref[...], preferred_element_type=F32)

    ffn_stages = []
    for j in range(len(ff_chunks)):
        ffn_stages += [functools.partial(ffn_gate, j), functools.partial(ffn_up, j), functools.partial(ffn_act, j)]
    ffn_stages.append(ffn_out)
    ffn_iter = iter(ffn_stages)
    next(ffn_iter)()

    x = x_ref[...]
    n = _rmsnorm(x, g_ref[...]).astype(BF16)
    z = jnp.dot(n, w_in_ref[:, :2 * d_conv], preferred_element_type=F32)

    inside = t > 0
    a_ext[:, 0:CONV_HALO, :] = jnp.where(inside, a_ext[:, tm:tm + CONV_HALO, :], 0.0)
    u_ext[:, 0:POOL_HALO, :] = jnp.where(inside, u_ext[:, tm:tm + POOL_HALO, :], 0.0)

    glu = z[:, :d_conv] * jax.nn.sigmoid(z[:, d_conv:2 * d_conv])
    for s in range(n_conv):
        a_ext[s, CONV_HALO:CONV_HALO + tm, :] = glu[:, s * LANES:(s + 1) * LANES]

    def pool_in():
        u = jnp.dot(n, w_in_ref[:, 2 * d_conv:], preferred_element_type=F32)
        for s in range(n_pool):
            u_ext[s, POOL_HALO:POOL_HALO + tm, :] = u[:, s * LANES:(s + 1) * LANES]

    base = CONV_HALO - (CONV_WIDTH - 1)

    always = step >= 0
    chain = []

    def conv_group(s, starts):
        lanes = slice(s * LANES, (s + 1) * LANES)
        bias = jnp.broadcast_to(dw_b_ref[:, lanes], (SUBLANES, LANES))
        if chain:
            bias = jnp.where(always, bias, chain[-1])
        accs = [bias] * len(starts)
        loaded = {}
        for k in range(CONV_WIDTH):
            wk = jnp.broadcast_to(dw_w_ref[k:k + 1, lanes], (SUBLANES, LANES))
            for i, r in enumerate(starts):
                src = r + base + k
                if src not in loaded:
                    loaded[src] = a_ext[s, _interleaved(src), :]
                accs[i] = accs[i] + loaded[src] * wk
        for i, r in enumerate(starts):
            conv_buf[s, _interleaved(r), :] = accs[i]
        chain.append(accs[-1])

    def pool_group(gi, w):
        for t0 in range(0, tm, TIME_BLOCK):
            loaded = {}
            for m in range(TIME_STRIDE):
                for j in range(w):
                    src = t0 + POOL_HALO + m - j
                    if src not in loaded:
                        loaded[src] = u_ext[gi, _interleaved(src), :]
                tok = loaded[t0 + POOL_HALO + m]
                tot = functools.reduce(jnp.add, [loaded[t0 + POOL_HALO + m - j] for j in range(w)])
                if t0 < w - 1:
                    frame = t * tm + t0 + m + TIME_STRIDE * lax.broadcasted_iota(jnp.int32, (SUBLANES, LANES), 0)
                    mean = tot / jnp.minimum(frame + 1, w).astype(F32)
                elif w & (w - 1) == 0:
                    mean = tot * (1.0 / w)
                else:
                    mean = tot / float(w)
                dlt_buf[gi, _interleaved(t0 + m), :] = mean - tok

    vreg_starts = [t0 + m for t0 in range(0, tm, TIME_BLOCK) for m in range(TIME_STRIDE)]
    mixer_stages = [functools.partial(conv_group, s, vreg_starts[i:i + CONV_ACCS])
                    for s in range(n_conv) for i in range(0, len(vreg_starts), CONV_ACCS)]
    mixer_stages.append(pool_in)
    mixer_stages += [functools.partial(pool_group, gi, w) for gi, w in enumerate(POOL_WINDOWS)]

    ffn_rest = list(ffn_iter)
    per = -(-len(mixer_stages) // len(ffn_rest))
    for i, ffn_stage in enumerate(ffn_rest):
        for stage in mixer_stages[i * per:(i + 1) * per]:
            stage()
        ffn_stage()

    cs = [conv_buf[s] for s in range(n_conv)]
    mu = jnp.sum(functools.reduce(jnp.add, cs), axis=-1, keepdims=True) / d_conv
    devs = [c - mu for c in cs]
    var = jnp.sum(functools.reduce(jnp.add, [dv * dv for dv in devs]), axis=-1, keepdims=True) / d_conv
    inv = lax.rsqrt(var + EPS)
    a_out = []
    for s in range(n_conv):
        lanes = slice(s * LANES, (s + 1) * LANES)
        c = devs[s] * inv * ln_g_ref[:, lanes] + ln_b_ref[:, lanes]
        a_out.append((c * jax.nn.sigmoid(c)).astype(BF16))
    pooled =[jnp.dot(dlt_buf[gi].astype(BF16), pool_w_ref[gi], preferred_element_type=F32)
              for gi in range(n_pool)]
    p_out = (jnp.concatenate(pooled, axis=-1) * pool_s_ref[...]).astype(BF16)
    cat = jnp.concatenate(a_out + [p_out], axis=-1)
    h_prev[...] = x + jnp.dot(cat, w_out_ref[...], preferred_element_type=F32)


def _layer0(x2d, g, w_in_all, dw_w, dw_b, ln_g, ln_b, pool_w, pool_s, w_out_all, gf, w1_all, w3_all, w2_all,
            *, seq, layer, mixer_index):
    n_tok, d = x2d.shape
    w_in, w_out, w1, w3, w2 = (jax.ShapeDtypeStruct(w.shape[1:], w.dtype)
                               for w in (w_in_all, w_out_all, w1_all, w3_all, w2_all))
    d_conv = dw_w.shape[1]
    d_pool = pool_s.shape[1]
    d_ff = w1.shape[1]
    tm = TOKEN_TILE
    n_tiles = n_tok // tm
    assert d_pool == len(POOL_WINDOWS) * LANES and d_conv % LANES == 0
    assert tm % TIME_BLOCK == 0 and TIME_STRIDE % CONV_ACCS == 0 and seq % tm == 0 and n_tok % seq == 0
    stages = [_stage_shape(w, LAYER0_STAGE_BYTES) for w in (w_in, w_out, w1)]
    assert stages[1] == _stage_shape(w2, LAYER0_STAGE_BYTES) and stages[2] == _stage_shape(w3, LAYER0_STAGE_BYTES)
    resident = (w_in.size + w_out.size + pool_w.size + 3 * d * d_ff) * 2 + 6 * tm * d * 4 \
        + tm * w_in.shape[1] * 4 + (2 * tm + CONV_HALO) * d_conv * 4 + (2 * tm + POOL_HALO) * d_pool * 4 \
        + 3 * tm * FFN_CHUNK * 4 + 3 * STAGE_SLOTS * LAYER0_STAGE_BYTES
    hbm = pl.BlockSpec(memory_space=pl.ANY)
    return pl.pallas_call(
        functools.partial(_layer0_kernel, tm=tm, tiles_per_seq=seq // tm, n_tiles=n_tiles,
                          layer=layer, mixer_index=mixer_index),
        grid=(n_tiles + 1,),
        in_specs=[
            pl.BlockSpec((tm, d), lambda s: (jnp.minimum(s, n_tiles - 1), 0)),
            _resident((1, d), 1),
            hbm,
            _resident(dw_w.shape, 1),
            _resident((1, d_conv), 1),
            _resident((1, d_conv), 1),
            _resident((1, d_conv), 1),
            _resident(pool_w.shape, 1),
            _resident((1, d_pool), 1),
            hbm,
            _resident((1, d), 1),
            hbm, hbm, hbm,
        ],
        out_specs=pl.BlockSpec((tm, d), lambda s: (jnp.maximum(s - 1, 0), 0)),
        out_shape=jax.ShapeDtypeStruct((n_tok, d), F32),
        scratch_shapes=[
            pltpu.VMEM((d_conv // LANES, CONV_HALO + tm, LANES), F32),
            pltpu.VMEM((d_pool // LANES, POOL_HALO + tm, LANES), F32),
            pltpu.VMEM((d_conv // LANES, tm, LANES), F32),
            pltpu.VMEM((d_pool // LANES, tm, LANES), F32),
            pltpu.VMEM((tm, d), F32),
            pltpu.VMEM((tm, d_ff), BF16),
            pltpu.VMEM(w_in.shape, BF16),
            pltpu.VMEM(w_out.shape, BF16),
            pltpu.VMEM(w1.shape, BF16),
            pltpu.VMEM(w3.shape, BF16),
            pltpu.VMEM(w2.shape, BF16),
            pltpu.VMEM(stages[0], F32),
            pltpu.VMEM(stages[1], F32),
            pltpu.VMEM(stages[2], F32),
            pltpu.SemaphoreType.DMA((STAGE_SLOTS,)),
            pltpu.SemaphoreType.DMA((STAGE_SLOTS,)),
            pltpu.SemaphoreType.DMA((STAGE_SLOTS,)),
        ],
        compiler_params=pltpu.CompilerParams(
            dimension_semantics=("arbitrary",),
            vmem_limit_bytes=_vmem_limit(resident)),
        name="layer0_convpool_ffn",
    )(x2d, g, w_in_all, dw_w, dw_b, ln_g, ln_b, pool_w, pool_s, w_out_all, gf, w1_all, w3_all, w2_all)


def _hgrn_kernel(h_ref, g_ref, w_in_hbm, lb_logits_ref, gn_ref, w_out_hbm, o_ref,
                 z_ref, y_ref, state_ref, w_in_ref, w_out_ref, stage_in, stage_out, sem_in, sem_out,
                 *, tm, layer, mixer_index):
    t = pl.program_id(1)
    d_hg = y_ref.shape[1]
    dk = d_hg // HG_HEADS

    @pl.when((pl.program_id(0) == 0) & (t == 0))
    def _():
        _fetch_weights_as_bf16([(w_in_hbm.at[mixer_index], w_in_ref, stage_in, sem_in),
                                (w_out_hbm.at[mixer_index], w_out_ref, stage_out, sem_out)])

    @pl.when(t == 0)
    def _():
        state_ref[...] = jnp.zeros(state_ref.shape, F32)

    x = h_ref[...]
    n = _rmsnorm(x, g_ref[...]).astype(BF16)
    z_ref[...] = jnp.dot(n, w_in_ref[...], preferred_element_type=F32)

    logits = lb_logits_ref[...]
    e = jnp.exp(logits - jnp.max(logits, axis=0, keepdims=True))
    p = e / jnp.sum(e, axis=0, keepdims=True)
    lb = jnp.sum(p[0:layer + 1], axis=0, keepdims=True) - p[0:1]

    row = lax.broadcasted_iota(jnp.int32, (CHUNK, CHUNK), 0)
    col = lax.broadcasted_iota(jnp.int32, (CHUNK, CHUNK), 1)
    causal = row >= col
    sublane = lax.broadcasted_iota(jnp.int32, (SUBLANES, LANES), 0)
    gn = gn_ref[...]
    mid = CHUNK // 2 - 1

    def chunk_step(c, carry):
        r0 = pl.multiple_of(c * CHUNK, CHUNK)
        rows = pl.ds(r0, CHUNK)
        q = z_ref[rows, 0:d_hg]
        f = lb + (1.0 - lb) * jax.nn.sigmoid(z_ref[rows, d_hg:2 * d_hg])
        log_f = jnp.log(f)
        k = 1.0 - f
        v = z_ref[rows, 2 * d_hg:3 * d_hg].astype(BF16)
        gate = z_ref[rows, 3 * d_hg:4 * d_hg]
        gate = gate * jax.nn.sigmoid(gate)

        b_tiles = []
        for lt in range(d_hg // LANES):
            running = None
            column = []
            for rt in range(CHUNK // SUBLANES):
                x = log_f[rt * SUBLANES:(rt + 1) * SUBLANES, lt * LANES:(lt + 1) * LANES]
                for s in (1, 2, 4):
                    x = x + jnp.where(sublane >= s, pltpu.roll(x, s, axis=0), 0.0)
                if running is not None:
                    x = x + running
                running = jnp.broadcast_to(x[SUBLANES - 1:SUBLANES, :], (SUBLANES, LANES))
                column.append(x)
            b_tiles.append(jnp.concatenate(column, axis=0))
        b = jnp.concatenate(b_tiles, axis=1)

        b_mid = b[mid:mid + 1, :]
        b_last = b[CHUNK - 1:CHUNK, :]
        q_rel = q * jnp.exp(b - b_mid)
        k_rel = k * jnp.exp(b_mid - b)
        q_in = (q_rel * jnp.exp(b_mid)).astype(BF16)
        k_up = (k_rel * jnp.exp(b_last - b_mid)).astype(BF16)
        q_rel = q_rel.astype(BF16)
        k_rel = k_rel.astype(BF16)
        decay_last = jnp.exp(b_last)

        heads = [slice(hd * dk, (hd + 1) * dk) for hd in range(HG_HEADS)]
        nt = (((1,), (1,)), ((), ()))
        tn = (((0,), (0,)), ((), ()))
        scores = [lax.dot_general(q_rel[:, sl], k_rel[:, sl], nt, preferred_element_type=F32)
                  for sl in heads]
        state_t = [state_ref[hd] for hd in range(HG_HEADS)]
        o_inter = [lax.dot_general(q_in[:, sl], st.astype(BF16), nt, preferred_element_type=F32)
                   for sl, st in zip(heads, state_t)]
        kv = [lax.dot_general(v[:, sl], k_up[:, sl], tn, preferred_element_type=F32) for sl in heads]
        for hd, sl in enumerate(heads):
            state_ref[hd] = state_t[hd] * decay_last[:, sl] + kv[hd]
        scores = [jnp.where(causal, s, 0.0).astype(BF16) for s in scores]
        outs = [jnp.dot(s, v[:, sl], preferred_element_type=F32) + oi
                for s, sl, oi in zip(scores, heads, o_inter)]
        for o, sl in zip(outs, heads):
            o = o * lax.rsqrt(jnp.mean(o * o, axis=-1, keepdims=True) + EPS) * gn[:, sl]
            y_ref[rows, sl] = (o * gate[:, sl]).astype(BF16)
        return carry

    lax.fori_loop(0, tm // CHUNK, chunk_step, 0, unroll=True)
    o_ref[...] = x + jnp.dot(y_ref[...], w_out_ref[...], preferred_element_type=F32)


def _hgrn_mixer(h, g, w_in_all, lb_logits, gn_g, w_out_all, *, layer, mixer_index):
    bsz, seq, d = h.shape
    w_in, w_out = (jax.ShapeDtypeStruct(w.shape[1:], w.dtype) for w in (w_in_all, w_out_all))
    d_hg = w_out.shape[0]
    dk = d_hg // HG_HEADS
    tm = TOKEN_TILE
    resident = (w_in.size + w_out.size) * 2 + 4 * tm * d * 4 + tm * 4 * d_hg * 4 * 2 \
        + tm * d_hg * 2 + HG_HEADS * dk * dk * 4 + 2 * STAGE_SLOTS * HGRN_STAGE_BYTES
    hbm = pl.BlockSpec(memory_space=pl.ANY)
    return pl.pallas_call(
        functools.partial(_hgrn_kernel, tm=tm, layer=layer, mixer_index=mixer_index),
        grid=(bsz, seq // tm),
        in_specs=[
            pl.BlockSpec((None, tm, d), lambda b, t: (b, t, 0)),
            _resident((1, d), 2),
            hbm,
            _resident(lb_logits.shape, 2),
            _resident((1, d_hg), 2),
            hbm,
        ],
        out_specs=pl.BlockSpec((None, tm, d), lambda b, t: (b, t, 0)),
        out_shape=jax.ShapeDtypeStruct((bsz, seq, d), F32),
        scratch_shapes=[
            pltpu.VMEM((tm, 4 * d_hg), F32),
            pltpu.VMEM((tm, d_hg), BF16),
            pltpu.VMEM((HG_HEADS, dk, dk), F32),
            pltpu.VMEM(w_in.shape, BF16),
            pltpu.VMEM(w_out.shape, BF16),
            pltpu.VMEM(_stage_shape(w_in, HGRN_STAGE_BYTES), F32),
            pltpu.VMEM(_stage_shape(w_out, HGRN_STAGE_BYTES), F32),
            pltpu.SemaphoreType.DMA((STAGE_SLOTS,)),
            pltpu.SemaphoreType.DMA((STAGE_SLOTS,)),
        ],
        compiler_params=pltpu.CompilerParams(
            dimension_semantics=("arbitrary", "arbitrary"),
            vmem_limit_bytes=_vmem_limit(resident)),
        name="hgrn_mixer",
    )(h, g, w_in_all, lb_logits, gn_g, w_out_all)


def kernel(x, norm_mix_g, norm_ffn_g, final_g, cp_w_in, cp_dw_w, cp_dw_b, cp_ln_g, cp_ln_b, cp_pool_w,
           cp_pool_scale, cp_w_out, hg_w_in, hg_lb_logits, hg_gn_g, hg_w_out, ffn_w1, ffn_w3, ffn_w2):
    bsz, seq, d = x.shape
    depth = norm_mix_g.shape[0]
    row = lambda a: a.reshape(1, -1).astype(F32)
    h = x
    for layer in range(depth):
        j = layer // 2
        last = layer == depth - 1
        if layer % 2 == 0:
            assert not last, "the fused conv/pool + FFN layer kernel has no final norm"
            h = _layer0(
                h.reshape(bsz * seq, d), row(norm_mix_g[layer]), cp_w_in, cp_dw_w[j],
                row(cp_dw_b[j]), row(cp_ln_g[j]), row(cp_ln_b[j]), cp_pool_w[j].astype(BF16),
                row(cp_pool_scale[j]), cp_w_out, row(norm_ffn_g[layer]), ffn_w1, ffn_w3, ffn_w2,
                seq=seq, layer=layer, mixer_index=j).reshape(bsz, seq, d)
        else:
            h = _hgrn_mixer(
                h, row(norm_mix_g[layer]), hg_w_in, hg_lb_logits.astype(F32),
                row(hg_gn_g[j]), hg_w_out, layer=layer, mixer_index=j)
            h = _ffn_block(
                h.reshape(bsz * seq, d), row(norm_ffn_g[layer]), ffn_w1, ffn_w3, ffn_w2, row(final_g),
                layer=layer, final_norm=last).reshape(bsz, seq, d)
    return h
```

```python
import functools
import itertools

import jax
import jax.numpy as jnp
from jax import lax
from jax.experimental import pallas as pl
from jax.experimental.pallas import tpu as pltpu

F32 = jnp.float32
BF16 = jnp.bfloat16

EPS = 1e-6
CHUNK = 64
CONV_WIDTH = 31
POOL_WINDOWS = (2, 4, 8, 16)
HG_HEADS = 8

V7X_VMEM_BYTES = 64 * 1024 * 1024
SUBLANES = 8
LANES = 128

TOKEN_TILE = 512
CONV_HALO = 32
POOL_HALO = 16
TIME_STRIDE = 4
TIME_BLOCK = SUBLANES * TIME_STRIDE
CONV_ACCS = 2
FFN_CHUNK = 512
STAGE_SLOTS = 3
FFN_STAGE_BYTES = 2 << 20
HGRN_STAGE_BYTES = 1 << 20
LAYER0_STAGE_BYTES = 1 << 19


def _interleaved(start):
    return pl.ds(start, SUBLANES, stride=TIME_STRIDE)


def _vmem_limit(resident_bytes):
    return int(min(resident_bytes * 1.5 + (8 << 20), V7X_VMEM_BYTES - (6 << 20)))


def _rmsnorm(x, g):
    return x * lax.rsqrt(jnp.mean(x * x, axis=-1, keepdims=True) + EPS) * g


def _resident(shape, ngrid):
    zeros = (0,) * len(shape)
    if ngrid == 1:
        index_map = lambda i: zeros
    else:
        index_map = lambda b, t: zeros
    return pl.BlockSpec(shape, index_map, pipeline_mode=pl.Buffered(1))


def _stage_shape(w, stage_bytes):
    n_rows, n_cols = w.shape
    pack = 2 * SUBLANES
    rows = max(pack, (stage_bytes // (4 * n_cols)) // pack * pack)
    while n_rows % rows:
        rows -= pack
    return (STAGE_SLOTS, rows, n_cols)


def _fetch_weights_as_bf16(jobs):
    queues = {}
    for src, dst, stage, sem in jobs:
        queue = queues.setdefault(id(stage), [])
        rows = stage.shape[1]
        for i in range(dst.shape[0] // rows):
            queue.append((src, dst, stage, sem, i * rows, rows, len(queue) % stage.shape[0]))
    chunks = [c for group in itertools.zip_longest(*queues.values()) for c in group if c is not None]

    def copy(c):
        src, _, stage, sem, r0, rows, slot = c
        return pltpu.make_async_copy(src.at[pl.ds(r0, rows)], stage.at[slot], sem.at[slot])

    starts_after = [[] for _ in chunks]
    last_in_slot = {}
    for k, c in enumerate(chunks):
        key = (id(c[2]), c[6])
        if key in last_in_slot:
            starts_after[last_in_slot[key]].append(k)
        else:
            copy(c).start()
        last_in_slot[key] = k
    for k, c in enumerate(chunks):
        copy(c).wait()
        _, dst, stage, _, r0, rows, slot = c
        dst[pl.ds(r0, rows), :] = stage[slot].astype(BF16)
        for nxt in starts_after[k]:
            copy(chunks[nxt]).start()


def _ffn_kernel(h_ref, g_ref, w1_hbm, w3_hbm, w2_hbm, fg_ref, o_ref,
                w1_ref, w3_ref, w2_ref, stage_in, stage_out, sem_in, sem_out, *, layer, final_norm):
    @pl.when(pl.program_id(0) == 0)
    def _():
        _fetch_weights_as_bf16([(w1_hbm.at[layer], w1_ref, stage_in, sem_in),
                                (w3_hbm.at[layer], w3_ref, stage_in, sem_in),
                                (w2_hbm.at[layer], w2_ref, stage_out, sem_out)])

    h = h_ref[...]
    n = _rmsnorm(h, g_ref[...]).astype(BF16)
    a = jnp.dot(n, w1_ref[...], preferred_element_type=F32)
    b = jnp.dot(n, w3_ref[...], preferred_element_type=F32)
    gated = (a * jax.nn.sigmoid(a) * b).astype(BF16)
    out = h + jnp.dot(gated, w2_ref[...], preferred_element_type=F32)
    if final_norm:
        out = _rmsnorm(out, fg_ref[...])
    o_ref[...] = out


def _ffn_block(h2d, g, w1_all, w3_all, w2_all, final_g, *, layer, final_norm):
    n_tok, d = h2d.shape
    w1, w3, w2 = (jax.ShapeDtypeStruct(w.shape[1:], w.dtype) for w in (w1_all, w3_all, w2_all))
    d_ff = w1.shape[1]
    tm = TOKEN_TILE
    stages = [_stage_shape(w1, FFN_STAGE_BYTES), _stage_shape(w2, FFN_STAGE_BYTES)]
    assert stages[0] == _stage_shape(w3, FFN_STAGE_BYTES)
    resident = 3 * d * d_ff * 2 + 4 * tm * d * 4 + 3 * tm * d_ff * 4 + 2 * STAGE_SLOTS * FFN_STAGE_BYTES
    hbm = pl.BlockSpec(memory_space=pl.ANY)
    return pl.pallas_call(
        functools.partial(_ffn_kernel, layer=layer, final_norm=final_norm),
        grid=(n_tok // tm,),
        in_specs=[
            pl.BlockSpec((tm, d), lambda i: (i, 0)),
            _resident((1, d), 1),
            hbm, hbm, hbm,
            _resident((1, d), 1),
        ],
        out_specs=pl.BlockSpec((tm, d), lambda i: (i, 0)),
        out_shape=jax.ShapeDtypeStruct((n_tok, d), F32),
        scratch_shapes=[
            pltpu.VMEM(w1.shape, BF16),
            pltpu.VMEM(w3.shape, BF16),
            pltpu.VMEM(w2.shape, BF16),
            pltpu.VMEM(stages[0], F32),
            pltpu.VMEM(stages[1], F32),
            pltpu.SemaphoreType.DMA((STAGE_SLOTS,)),
            pltpu.SemaphoreType.DMA((STAGE_SLOTS,)),
        ],
        compiler_params=pltpu.CompilerParams(
            dimension_semantics=("arbitrary",),
            vmem_limit_bytes=_vmem_limit(resident)),
        name="ffn_block",
    )(h2d, g, w1_all, w3_all, w2_all, final_g)


def _layer0_kernel(x_ref, g_ref, w_in_hbm, dw_w_ref, dw_b_ref, ln_g_ref, ln_b_ref, pool_w_ref, pool_s_ref,
                   w_out_hbm, gf_ref, w1_hbm, w3_hbm, w2_hbm, o_ref,
                   a_ext, u_ext, conv_buf, dlt_buf, h_prev, gated_ref, w_in_ref, w_out_ref, w1_ref, w3_ref, w2_ref,
                   stage_in, stage_d, stage_ff, sem_in, sem_d, sem_ff,
                   *, tm, tiles_per_seq, n_tiles, layer, mixer_index):
    step = pl.program_id(0)
    t = lax.rem(jnp.minimum(step, n_tiles - 1), tiles_per_seq)
    n_conv = a_ext.shape[0]
    n_pool = u_ext.shape[0]
    d_conv = n_conv * LANES
    d_ff = w1_ref.shape[1]

    @pl.when(step == 0)
    def _():
        a_ext[:, tm:tm + CONV_HALO, :] = jnp.zeros((n_conv, CONV_HALO, LANES), F32)
        u_ext[:, tm:tm + POOL_HALO, :] = jnp.zeros((n_pool, POOL_HALO, LANES), F32)
        _fetch_weights_as_bf16([(w_in_hbm.at[mixer_index], w_in_ref, stage_in, sem_in),
                                (w_out_hbm.at[mixer_index], w_out_ref, stage_d, sem_d),
                                (w2_hbm.at[layer], w2_ref, stage_d, sem_d),
                                (w1_hbm.at[layer], w1_ref, stage_ff, sem_ff),
                                (w3_hbm.at[layer], w3_ref, stage_ff, sem_ff)])

    live = {}

    def ffn_begin():
        live["hp"] = h_prev[...]
        live["nf"] = _rmsnorm(live["hp"], gf_ref[...]).astype(BF16)

    ff_chunks = [(c0, min(c0 + FFN_CHUNK, d_ff)) for c0 in range(0, d_ff, FFN_CHUNK)]
    gate_in, up_in = {}, {}

    def ffn_gate(j):
        c0, c1 = ff_chunks[j]
        gate_in[j] = jnp.dot(live["nf"], w1_ref[:, c0:c1], preferred_element_type=F32)

    def ffn_up(j):
        c0, c1 = ff_chunks[j]
        up_in[j] = jnp.dot(live["nf"], w3_ref[:, c0:c1], preferred_element_type=F32)

    def ffn_act(j):
        c0, c1 = ff_chunks[j]
        a, b = gate_in.pop(j), up_in.pop(j)
        gated_ref[:, c0:c1] = (a * jax.nn.sigmoid(a) * b).astype(BF16)

    def ffn_out():
        o_ref[...] = live["hp"] + jnp.dot(gated_ref[...], w2_ref[...], preferred_element_type=F32)

    ffn_stages = []
    for j in range(len(ff_chunks)):
        ffn_stages += [functools.partial(ffn_gate, j), functools.partial(ffn_up, j), functools.partial(ffn_act, j)]
    ffn_stages.append(ffn_out)

    def mixer_begin():
        x = x_ref[...]
        n = _rmsnorm(x, g_ref[...]).astype(BF16)
        z = jnp.dot(n, w_in_ref[:, :2 * d_conv], preferred_element_type=F32)
        live["x"], live["n"] = x, n

        inside = t > 0
        a_ext[:, 0:CONV_HALO, :] = jnp.where(inside, a_ext[:, tm:tm + CONV_HALO, :], 0.0)
        u_ext[:, 0:POOL_HALO, :] = jnp.where(inside, u_ext[:, tm:tm + POOL_HALO, :], 0.0)

        glu = z[:, :d_conv] * jax.nn.sigmoid(z[:, d_conv:2 * d_conv])
        for s in range(n_conv):
            a_ext[s, CONV_HALO:CONV_HALO + tm, :] = glu[:, s * LANES:(s + 1) * LANES]

    def pool_in():
        u = jnp.dot(live["n"], w_in_ref[:, 2 * d_conv:], preferred_element_type=F32)
        for s in range(n_pool):
            u_ext[s, POOL_HALO:POOL_HALO + tm, :] = u[:, s * LANES:(s + 1) * LANES]

    base = CONV_HALO - (CONV_WIDTH - 1)

    always = step >= 0
    chain = []

    def conv_group(s, starts):
        lanes = slice(s * LANES, (s + 1) * LANES)
        bias = jnp.broadcast_to(dw_b_ref[:, lanes], (SUBLANES, LANES))
        if chain:
            bias = jnp.where(always, bias, chain[-1])
        accs = [bias] * len(starts)
        loaded = {}
        for k in range(CONV_WIDTH):
            wk = jnp.broadcast_to(dw_w_ref[k:k + 1, lanes], (SUBLANES, LANES))
            for i, r in enumerate(starts):
                src = r + base + k
                if src not in loaded:
                    loaded[src] = a_ext[s, _interleaved(src), :]
                accs[i] = accs[i] + loaded[src] * wk
        for i, r in enumerate(starts):
            conv_buf[s, _interleaved(r), :] = accs[i]
        chain.append(accs[-1])

    def pool_group(gi, w):
        for t0 in range(0, tm, TIME_BLOCK):
            loaded = {}
            for m in range(TIME_STRIDE):
                for j in range(w):
                    src = t0 + POOL_HALO + m - j
                    if src not in loaded:
                        loaded[src] = u_ext[gi, _interleaved(src), :]
                tok = loaded[t0 + POOL_HALO + m]
                tot = functools.reduce(jnp.add, [loaded[t0 + POOL_HALO + m - j] for j in range(w)])
                if t0 < w - 1:
                    frame = t * tm + t0 + m + TIME_STRIDE * lax.broadcasted_iota(jnp.int32, (SUBLANES, LANES), 0)
                    mean = tot / jnp.minimum(frame + 1, w).astype(F32)
                elif w & (w - 1) == 0:
                    mean = tot * (1.0 / w)
                else:
                    mean = tot / float(w)
                dlt_buf[gi, _interleaved(t0 + m), :] = mean - tok

    vreg_starts = [t0 + m for t0 in range(0, tm, TIME_BLOCK) for m in range(TIME_STRIDE)]
    mixer_stages = [functools.partial(conv_group, s, vreg_starts[i:i + CONV_ACCS])
                    for s in range(n_conv) for i in range(0, len(vreg_starts), CONV_ACCS)]
    mixer_stages.append(pool_in)
    mixer_stages += [functools.partial(pool_group, gi, w) for gi, w in enumerate(POOL_WINDOWS)]

    def mixer_end():
        cs = [conv_buf[s] for s in range(n_conv)]
        mu = jnp.sum(functools.reduce(jnp.add, cs), axis=-1, keepdims=True) / d_conv
        devs = [c - mu for c in cs]
        var = jnp.sum(functools.reduce(jnp.add, [dv * dv for dv in devs]), axis=-1, keepdims=True) / d_conv
        inv = lax.rsqrt(var + EPS)
        a_out = []
        for s in range(n_conv):
            lanes = slice(s * LANES, (s + 1) * LANES)
            c = devs[s] * inv * ln_g_ref[:, lanes] + ln_b_ref[:, lanes]
            a_out.append((c * jax.nn.sigmoid(c)).astype(BF16))
        pooled = [jnp.dot(dlt_buf[gi].astype(BF16), pool_w_ref[gi], preferred_element_type=F32)
                  for gi in range(n_pool)]
        p_out = (jnp.concatenate(pooled, axis=-1) * pool_s_ref[...]).astype(BF16)
        cat = jnp.concatenate(a_out + [p_out], axis=-1)
        h_prev[...] = live["x"] + jnp.dot(cat, w_out_ref[...], preferred_element_type=F32)

    def run(do_mixer, do_ffn):
        chain.clear()
        live.clear()
        ffn_rest = list(ffn_stages) if do_ffn else []
        if do_ffn:
            ffn_begin()
            ffn_rest.pop(0)()
        if do_mixer:
            mixer_begin()
            slots = max(len(ffn_rest), 1)
            per = -(-len(mixer_stages) // slots)
            for i in range(slots):
                for stage in mixer_stages[i * per:(i + 1) * per]:
                    stage()
                if ffn_rest:
                    ffn_rest.pop(0)()
            mixer_end()
        for ffn_stage in ffn_rest:
            ffn_stage()

    pl.when(step == 0)(functools.partial(run, True, False))
    pl.when((step > 0) & (step < n_tiles))(functools.partial(run, True, True))
    pl.when(step == n_tiles)(functools.partial(run, False, True))


def _layer0(x2d, g, w_in_all, dw_w, dw_b, ln_g, ln_b, pool_w, pool_s, w_out_all, gf, w1_all, w3_all, w2_all,
            *, seq, layer, mixer_index):
    n_tok, d = x2d.shape
    w_in, w_out, w1, w3, w2 = (jax.ShapeDtypeStruct(w.shape[1:], w.dtype)
                               for w in (w_in_all, w_out_all, w1_all, w3_all, w2_all))
    d_conv = dw_w.shape[1]
    d_pool = pool_s.shape[1]
    d_ff = w1.shape[1]
    tm = TOKEN_TILE
    n_tiles = n_tok // tm
    assert d_pool == len(POOL_WINDOWS) * LANES and d_conv % LANES == 0
    assert tm % TIME_BLOCK == 0 and TIME_STRIDE % CONV_ACCS == 0 and seq % tm == 0 and n_tok % seq == 0
    stages = [_stage_shape(w, LAYER0_STAGE_BYTES) for w in (w_in, w_out, w1)]
    assert stages[1] == _stage_shape(w2, LAYER0_STAGE_BYTES) and stages[2] == _stage_shape(w3, LAYER0_STAGE_BYTES)
    resident = (w_in.size + w_out.size + pool_w.size + 3 * d * d_ff) * 2 + 6 * tm * d * 4 \
        + tm * w_in.shape[1] * 4 + (2 * tm + CONV_HALO) * d_conv * 4 + (2 * tm + POOL_HALO) * d_pool * 4 \
        + 3 * tm * FFN_CHUNK * 4 + 3 * STAGE_SLOTS * LAYER0_STAGE_BYTES
    hbm = pl.BlockSpec(memory_space=pl.ANY)
    return pl.pallas_call(
        functools.partial(_layer0_kernel, tm=tm, tiles_per_seq=seq // tm, n_tiles=n_tiles,
                          layer=layer, mixer_index=mixer_index),
        grid=(n_tiles + 1,),
        in_specs=[
            pl.BlockSpec((tm, d), lambda s: (jnp.minimum(s, n_tiles - 1), 0)),
            _resident((1, d), 1),
            hbm,
            _resident(dw_w.shape, 1),
            _resident((1, d_conv), 1),
            _resident((1, d_conv), 1),
            _resident((1, d_conv), 1),
            _resident(pool_w.shape, 1),
            _resident((1, d_pool), 1),
            hbm,
            _resident((1, d), 1),
            hbm, hbm, hbm,
        ],
        out_specs=pl.BlockSpec((tm, d), lambda s: (jnp.maximum(s - 1, 0), 0)),
        out_shape=jax.ShapeDtypeStruct((n_tok, d), F32),
        scratch_shapes=[
            pltpu.VMEM((d_conv // LANES, CONV_HALO + tm, LANES), F32),
            pltpu.VMEM((d_pool // LANES, POOL_HALO + tm, LANES), F32),
            pltpu.VMEM((d_conv // LANES, tm, LANES), F32),
            pltpu.VMEM((d_pool // LANES, tm, LANES), F32),
            pltpu.VMEM((tm, d), F32),
            pltpu.VMEM((tm, d_ff), BF16),
            pltpu.VMEM(w_in.shape, BF16),
            pltpu.VMEM(w_out.shape, BF16),
            pltpu.VMEM(w1.shape, BF16),
            pltpu.VMEM(w3.shape, BF16),
            pltpu.VMEM(w2.shape, BF16),
            pltpu.VMEM(stages[0], F32),
            pltpu.VMEM(stages[1], F32),
            pltpu.VMEM(stages[2], F32),
            pltpu.SemaphoreType.DMA((STAGE_SLOTS,)),
            pltpu.SemaphoreType.DMA((STAGE_SLOTS,)),
            pltpu.SemaphoreType.DMA((STAGE_SLOTS,)),
        ],
        compiler_params=pltpu.CompilerParams(
            dimension_semantics=("arbitrary",),
            vmem_limit_bytes=_vmem_limit(resident)),
        name="layer0_convpool_ffn",
    )(x2d, g, w_in_all, dw_w, dw_b, ln_g, ln_b, pool_w, pool_s, w_out_all, gf, w1_all, w3_all, w2_all)


def _hgrn_kernel(h_ref, g_ref, w_in_hbm, lb_logits_ref, gn_ref, w_out_hbm, o_ref,
                 z_ref, y_ref, state_ref, w_in_ref, w_out_ref, stage_in, stage_out, sem_in, sem_out,
                 *, tm, layer, mixer_index):
    t = pl.program_id(1)
    d_hg = y_ref.shape[1]
    dk = d_hg // HG_HEADS

    @pl.when((pl.program_id(0) == 0) & (t == 0))
    def _():
        _fetch_weights_as_bf16([(w_in_hbm.at[mixer_index], w_in_ref, stage_in, sem_in),
                                (w_out_hbm.at[mixer_index], w_out_ref, stage_out, sem_out)])

    @pl.when(t == 0)
    def _():
        state_ref[...] = jnp.zeros(state_ref.shape, F32)

    x = h_ref[...]
    n = _rmsnorm(x, g_ref[...]).astype(BF16)
    z_ref[...] = jnp.dot(n, w_in_ref[...], preferred_element_type=F32)

    logits = lb_logits_ref[...]
    e = jnp.exp(logits - jnp.max(logits, axis=0, keepdims=True))
    p = e / jnp.sum(e, axis=0, keepdims=True)
    lb = jnp.sum(p[0:layer + 1], axis=0, keepdims=True) - p[0:1]

    row = lax.broadcasted_iota(jnp.int32, (CHUNK, CHUNK), 0)
    col = lax.broadcasted_iota(jnp.int32, (CHUNK, CHUNK), 1)
    causal = row >= col
    sublane = lax.broadcasted_iota(jnp.int32, (SUBLANES, LANES), 0)
    gn = gn_ref[...]
    mid = CHUNK // 2 - 1

    def chunk_step(c, carry):
        r0 = pl.multiple_of(c * CHUNK, CHUNK)
        rows = pl.ds(r0, CHUNK)
        q = z_ref[rows, 0:d_hg]
        f = lb + (1.0 - lb) * jax.nn.sigmoid(z_ref[rows, d_hg:2 * d_hg])
        log_f = jnp.log(f)
        k = 1.0 - f
        v = z_ref[rows, 2 * d_hg:3 * d_hg].astype(BF16)
        gate = z_ref[rows, 3 * d_hg:4 * d_hg]
        gate = gate * jax.nn.sigmoid(gate)

        b_tiles = []
        for lt in range(d_hg // LANES):
            running = None
            column = []
            for rt in range(CHUNK // SUBLANES):
                x = log_f[rt * SUBLANES:(rt + 1) * SUBLANES, lt * LANES:(lt + 1) * LANES]
                for s in (1, 2, 4):
                    x = x + jnp.where(sublane >= s, pltpu.roll(x, s, axis=0), 0.0)
                if running is not None:
                    x = x + running
                running = jnp.broadcast_to(x[SUBLANES - 1:SUBLANES, :], (SUBLANES, LANES))
                column.append(x)
            b_tiles.append(jnp.concatenate(column, axis=0))
        b = jnp.concatenate(b_tiles, axis=1)

        b_mid = b[mid:mid + 1, :]
        b_last = b[CHUNK - 1:CHUNK, :]
        q_rel = q * jnp.exp(b - b_mid)
        k_rel = k * jnp.exp(b_mid - b)
        q_in = (q_rel * jnp.exp(b_mid)).astype(BF16)
        k_up = (k_rel * jnp.exp(b_last - b_mid)).astype(BF16)
        q_rel = q_rel.astype(BF16)
        k_rel = k_rel.astype(BF16)
        decay_last = jnp.exp(b_last)

        heads = [slice(hd * dk, (hd + 1) * dk) for hd in range(HG_HEADS)]
        nt = (((1,), (1,)), ((), ()))
        tn = (((0,), (0,)), ((), ()))
        scores = [lax.dot_general(q_rel[:, sl], k_rel[:, sl], nt, preferred_element_type=F32)
                  for sl in heads]
        state_t = [state_ref[hd] for hd in range(HG_HEADS)]
        o_inter = [lax.dot_general(q_in[:, sl], st.astype(BF16), nt, preferred_element_type=F32)
                   for sl, st in zip(heads, state_t)]
        kv = [lax.dot_general(v[:, sl], k_up[:, sl], tn, preferred_element_type=F32) for sl in heads]
        for hd, sl in enumerate(heads):
            state_ref[hd] = state_t[hd] * decay_last[:, sl] + kv[hd]
        scores = [jnp.where(causal, s, 0.0).astype(BF16) for s in scores]
        outs = [jnp.dot(s, v[:, sl], preferred_element_type=F32) + oi
                for s, sl, oi in zip(scores, heads, o_inter)]
        for o, sl in zip(outs, heads):
            o = o * lax.rsqrt(jnp.mean(o * o, axis=-1, keepdims=True) + EPS) * gn[:, sl]
            y_ref[rows, sl] = (o * gate[:, sl]).astype(BF16)
        return carry

    lax.fori_loop(0, tm // CHUNK, chunk_step, 0, unroll=True)
    o_ref[...] = x + jnp.dot(y_ref[...], w_out_ref[...], preferred_element_type=F32)


def _hgrn_mixer(h, g, w_in_all, lb_logits, gn_g, w_out_all, *, layer, mixer_index):
    bsz, seq, d = h.shape
    w_in, w_out = (jax.ShapeDtypeStruct(w.shape[1:], w.dtype) for w in (w_in_all, w_out_all))
    d_hg = w_out.shape[0]
    dk = d_hg // HG_HEADS
    tm = TOKEN_TILE
    resident = (w_in.size + w_out.size) * 2 + 4 * tm * d * 4 + tm * 4 * d_hg * 4 * 2 \
        + tm * d_hg * 2 + HG_HEADS * dk * dk * 4 + 2 * STAGE_SLOTS * HGRN_STAGE_BYTES
    hbm = pl.BlockSpec(memory_space=pl.ANY)
    return pl.pallas_call(
        functools.partial(_hgrn_kernel, tm=tm, layer=layer, mixer_index=mixer_index),
        grid=(bsz, seq // tm),
        in_specs=[
            pl.BlockSpec((None, tm, d), lambda b, t: (b, t, 0)),
            _resident((1, d), 2),
            hbm,
            _resident(lb_logits.shape, 2),
            _resident((1, d_hg), 2),
            hbm,
        ],
        out_specs=pl.BlockSpec((None, tm, d), lambda b, t: (b, t, 0)),
        out_shape=jax.ShapeDtypeStruct((bsz, seq, d), F32),
        scratch_shapes=[
            pltpu.VMEM((tm, 4 * d_hg), F32),
            pltpu.VMEM((tm, d_hg), BF16),
            pltpu.VMEM((HG_HEADS, dk, dk), F32),
            pltpu.VMEM(w_in.shape, BF16),
            pltpu.VMEM(w_out.shape, BF16),
            pltpu.VMEM(_stage_shape(w_in, HGRN_STAGE_BYTES), F32),
            pltpu.VMEM(_stage_shape(w_out, HGRN_STAGE_BYTES), F32),
            pltpu.SemaphoreType.DMA((STAGE_SLOTS,)),
            pltpu.SemaphoreType.DMA((STAGE_SLOTS,)),
        ],
        compiler_params=pltpu.CompilerParams(
            dimension_semantics=("arbitrary", "arbitrary"),
            vmem_limit_bytes=_vmem_limit(resident)),
        name="hgrn_mixer",
    )(h, g, w_in_all, lb_logits, gn_g, w_out_all)


def kernel(x, norm_mix_g, norm_ffn_g, final_g, cp_w_in, cp_dw_w, cp_dw_b, cp_ln_g, cp_ln_b, cp_pool_w,
           cp_pool_scale, cp_w_out, hg_w_in, hg_lb_logits, hg_gn_g, hg_w_out, ffn_w1, ffn_w3, ffn_w2):
    bsz, seq, d = x.shape
    depth = norm_mix_g.shape[0]
    row = lambda a: a.reshape(1, -1).astype(F32)
    h = x
    for layer in range(depth):
        j = layer // 2
        last = layer == depth - 1
        if layer % 2 == 0:
            assert not last, "the fused conv/pool + FFN layer kernel has no final norm"
            h = _layer0(
                h.reshape(bsz * seq, d), row(norm_mix_g[layer]), cp_w_in, cp_dw_w[j],
                row(cp_dw_b[j]), row(cp_ln_g[j]), row(cp_ln_b[j]), cp_pool_w[j].astype(BF16),
                row(cp_pool_scale[j]), cp_w_out, row(norm_ffn_g[layer]), ffn_w1, ffn_w3, ffn_w2,
                seq=seq, layer=layer, mixer_index=j).reshape(bsz, seq, d)
        else:
            h = _hgrn_mixer(
                h, row(norm_mix_g[layer]), hg_w_in, hg_lb_logits.astype(F32),
                row(hg_gn_g[j]), hg_w_out, layer=layer, mixer_index=j)
            h = _ffn_block(
                h.reshape(bsz * seq, d), row(norm_ffn_g[layer]), ffn_w1, ffn_w3, ffn_w2, row(final_g),
                layer=layer, final_norm=last).reshape(bsz, seq, d)
    return h
```

```python
import functools
import itertools

import jax
import jax.numpy as jnp
from jax import lax
from jax.experimental import pallas as pl
from jax.experimental.pallas import tpu as pltpu

F32 = jnp.float32
BF16 = jnp.bfloat16

EPS = 1e-6
CHUNK = 64
CONV_WIDTH = 31
POOL_WINDOWS = (2, 4, 8, 16)
HG_HEADS = 8

V7X_VMEM_BYTES = 64 * 1024 * 1024
SUBLANES = 8
LANES = 128

TOKEN_TILE = 512
CONV_HALO = 32
POOL_HALO = 16
TIME_STRIDE = 4
TIME_BLOCK = SUBLANES * TIME_STRIDE
CONV_ACCS = 2
FFN_CHUNK = 512
STAGE_SLOTS = 3
FFN_STAGE_BYTES = 2 << 20
HGRN_STAGE_BYTES = 1 << 20
LAYER0_STAGE_BYTES = 1 << 19


def _interleaved(start):
    return pl.ds(start, SUBLANES, stride=TIME_STRIDE)


def _vmem_limit(resident_bytes):
    return int(min(resident_bytes * 1.5 + (8 << 20), V7X_VMEM_BYTES - (6 << 20)))


def _rmsnorm(x, g):
    return x * lax.rsqrt(jnp.mean(x * x, axis=-1, keepdims=True) + EPS) * g


def _resident(shape, ngrid):
    zeros = (0,) * len(shape)
    if ngrid == 1:
        index_map = lambda i: zeros
    else:
        index_map = lambda b, t: zeros
    return pl.BlockSpec(shape, index_map, pipeline_mode=pl.Buffered(1))


def _stage_shape(w, stage_bytes):
    n_rows, n_cols = w.shape
    pack = 2 * SUBLANES
    rows = max(pack, (stage_bytes // (4 * n_cols)) // pack * pack)
    while n_rows % rows:
        rows -= pack
    return (STAGE_SLOTS, rows, n_cols)


def _fetch_weights_as_bf16(jobs):
    queues = {}
    for src, dst, stage, sem in jobs:
        queue = queues.setdefault(id(stage), [])
        rows = stage.shape[1]
        for i in range(dst.shape[0] // rows):
            queue.append((src, dst, stage, sem, i * rows, rows, len(queue) % stage.shape[0]))
    chunks = [c for group in itertools.zip_longest(*queues.values()) for c in group if c is not None]

    def copy(c):
        src, _, stage, sem, r0, rows, slot = c
        return pltpu.make_async_copy(src.at[pl.ds(r0, rows)], stage.at[slot], sem.at[slot])

    starts_after = [[] for _ in chunks]
    last_in_slot = {}
    for k, c in enumerate(chunks):
        key = (id(c[2]), c[6])
        if key in last_in_slot:
            starts_after[last_in_slot[key]].append(k)
        else:
            copy(c).start()
        last_in_slot[key] = k
    for k, c in enumerate(chunks):
        copy(c).wait()
        _, dst, stage, _, r0, rows, slot = c
        dst[pl.ds(r0, rows), :] = stage[slot].astype(BF16)
        for nxt in starts_after[k]:
            copy(chunks[nxt]).start()


def _ffn_kernel(h_ref, g_ref, w1_hbm, w3_hbm, w2_hbm, fg_ref, o_ref,
                w1_ref, w3_ref, w2_ref, stage_in, stage_out, sem_in, sem_out, *, layer, final_norm):
    @pl.when(pl.program_id(0) == 0)
    def _():
        _fetch_weights_as_bf16([(w1_hbm.at[layer], w1_ref, stage_in, sem_in),
                                (w3_hbm.at[layer], w3_ref, stage_in, sem_in),
                                (w2_hbm.at[layer], w2_ref, stage_out, sem_out)])

    h = h_ref[...]
    n = _rmsnorm(h, g_ref[...]).astype(BF16)
    a = jnp.dot(n, w1_ref[...], preferred_element_type=F32)
    b = jnp.dot(n, w3_ref[...], preferred_element_type=F32)
    gated = (a * jax.nn.sigmoid(a) * b).astype(BF16)
    out = h + jnp.dot(gated, w2_ref[...], preferred_element_type=F32)
    if final_norm:
        out = _rmsnorm(out, fg_ref[...])
    o_ref[...] = out


def _ffn_block(h2d, g, w1_all, w3_all, w2_all, final_g, *, layer, final_norm):
    n_tok, d = h2d.shape
    w1, w3, w2 = (jax.ShapeDtypeStruct(w.shape[1:], w.dtype) for w in (w1_all, w3_all, w2_all))
    d_ff = w1.shape[1]
    tm = TOKEN_TILE
    stages = [_stage_shape(w1, FFN_STAGE_BYTES), _stage_shape(w2, FFN_STAGE_BYTES)]
    assert stages[0] == _stage_shape(w3, FFN_STAGE_BYTES)
    resident = 3 * d * d_ff * 2 + 4 * tm * d * 4 + 3 * tm * d_ff * 4 + 2 * STAGE_SLOTS * FFN_STAGE_BYTES
    hbm = pl.BlockSpec(memory_space=pl.ANY)
    return pl.pallas_call(
        functools.partial(_ffn_kernel, layer=layer, final_norm=final_norm),
        grid=(n_tok // tm,),
        in_specs=[
            pl.BlockSpec((tm, d), lambda i: (i, 0)),
            _resident((1, d), 1),
            hbm, hbm, hbm,
            _resident((1, d), 1),
        ],
        out_specs=pl.BlockSpec((tm, d), lambda i: (i, 0)),
        out_shape=jax.ShapeDtypeStruct((n_tok, d), F32),
        scratch_shapes=[
            pltpu.VMEM(w1.shape, BF16),
            pltpu.VMEM(w3.shape, BF16),
            pltpu.VMEM(w2.shape, BF16),
            pltpu.VMEM(stages[0], F32),
            pltpu.VMEM(stages[1], F32),
            pltpu.SemaphoreType.DMA((STAGE_SLOTS,)),
            pltpu.SemaphoreType.DMA((STAGE_SLOTS,)),
        ],
        compiler_params=pltpu.CompilerParams(
            dimension_semantics=("arbitrary",),
            vmem_limit_bytes=_vmem_limit(resident)),
        name="ffn_block",
    )(h2d, g, w1_all, w3_all, w2_all, final_g)


def _layer0_kernel(x_ref, g_ref, w_in_hbm, dw_w_ref, dw_b_ref, ln_g_ref, ln_b_ref, pool_w_ref, pool_s_ref,
                   w_out_hbm, gf_ref, w1_hbm, w3_hbm, w2_hbm, o_ref,
                   a_ext, u_ext, conv_buf, dlt_buf, h_prev, gated_ref, w_in_ref, w_out_ref, w1_ref, w3_ref, w2_ref,
                   stage_in, stage_d, stage_ff, sem_in, sem_d, sem_ff,
                   *, tm, tiles_per_seq, n_tiles, layer, mixer_index):
    step = pl.program_id(0)
    t = lax.rem(jnp.minimum(step, n_tiles - 1), tiles_per_seq)
    n_conv = a_ext.shape[0]
    n_pool = u_ext.shape[0]
    d_conv = n_conv * LANES
    d_ff = w1_ref.shape[1]

    @pl.when(step == 0)
    def _():
        h_prev[...] = jnp.zeros(h_prev.shape, F32)
        a_ext[:, tm:tm + CONV_HALO, :] = jnp.zeros((n_conv, CONV_HALO, LANES), F32)
        u_ext[:, tm:tm + POOL_HALO, :] = jnp.zeros((n_pool, POOL_HALO, LANES), F32)
        _fetch_weights_as_bf16([(w_in_hbm.at[mixer_index], w_in_ref, stage_in, sem_in),
                                (w_out_hbm.at[mixer_index], w_out_ref, stage_d, sem_d),
                                (w2_hbm.at[layer], w2_ref, stage_d, sem_d),
                                (w1_hbm.at[layer], w1_ref, stage_ff, sem_ff),
                                (w3_hbm.at[layer], w3_ref, stage_ff, sem_ff)])

    live = {}

    def ffn_begin():
        live["hp"] = h_prev[...]
        live["nf"] = _rmsnorm(live["hp"], gf_ref[...]).astype(BF16)

    ff_chunks = [(c0, min(c0 + FFN_CHUNK, d_ff)) for c0 in range(0, d_ff, FFN_CHUNK)]
    gate_in, up_in = {}, {}

    def ffn_gate(j):
        c0, c1 = ff_chunks[j]
        gate_in[j] = jnp.dot(live["nf"], w1_ref[:, c0:c1], preferred_element_type=F32)

    def ffn_up(j):
        c0, c1 = ff_chunks[j]
        up_in[j] = jnp.dot(live["nf"], w3_ref[:, c0:c1], preferred_element_type=F32)

    def ffn_act(j):
        c0, c1 = ff_chunks[j]
        a, b = gate_in.pop(j), up_in.pop(j)
        gated_ref[:, c0:c1] = (a * jax.nn.sigmoid(a) * b).astype(BF16)

    def ffn_out():
        o_ref[...] = live["hp"] + jnp.dot(gated_ref[...], w2_ref[...], preferred_element_type=F32)

    ffn_stages = []
    for j in range(len(ff_chunks)):
        ffn_stages += [functools.partial(ffn_gate, j), functools.partial(ffn_up, j), functools.partial(ffn_act, j)]
    ffn_stages.append(ffn_out)

    def mixer_begin():
        x = x_ref[...]
        n = _rmsnorm(x, g_ref[...]).astype(BF16)
        z = jnp.dot(n, w_in_ref[:, :2 * d_conv], preferred_element_type=F32)
        live["x"], live["n"] = x, n

        inside = t > 0
        a_ext[:, 0:CONV_HALO, :] = jnp.where(inside, a_ext[:, tm:tm + CONV_HALO, :], 0.0)
        u_ext[:, 0:POOL_HALO, :] = jnp.where(inside, u_ext[:, tm:tm + POOL_HALO, :], 0.0)

        glu = z[:, :d_conv] * jax.nn.sigmoid(z[:, d_conv:2 * d_conv])
        for s in range(n_conv):
            a_ext[s, CONV_HALO:CONV_HALO + tm, :] = glu[:, s * LANES:(s + 1) * LANES]

    def pool_in():
        u = jnp.dot(live["n"], w_in_ref[:, 2 * d_conv:], preferred_element_type=F32)
        for s in range(n_pool):
            u_ext[s, POOL_HALO:POOL_HALO + tm, :] = u[:, s * LANES:(s + 1) * LANES]

    base = CONV_HALO - (CONV_WIDTH - 1)

    always = step >= 0
    chain = []

    def conv_group(s, starts):
        lanes = slice(s * LANES, (s + 1) * LANES)
        bias = jnp.broadcast_to(dw_b_ref[:, lanes], (SUBLANES, LANES))
        if chain:
            bias = jnp.where(always, bias, chain[-1])
        accs = [bias] * len(starts)
        loaded = {}
        for k in range(CONV_WIDTH):
            wk = jnp.broadcast_to(dw_w_ref[k:k + 1, lanes], (SUBLANES, LANES))
            for i, r in enumerate(starts):
                src = r + base + k
                if src not in loaded:
                    loaded[src] = a_ext[s, _interleaved(src), :]
                accs[i] = accs[i] + loaded[src] * wk
        for i, r in enumerate(starts):
            conv_buf[s, _interleaved(r), :] = accs[i]
        chain.append(accs[-1])

    def pool_group(gi, w):
        for t0 in range(0, tm, TIME_BLOCK):
            loaded = {}
            for m in range(TIME_STRIDE):
                for j in range(w):
                    src = t0 + POOL_HALO + m - j
                    if src not in loaded:
                        loaded[src] = u_ext[gi, _interleaved(src), :]
                tok = loaded[t0 + POOL_HALO + m]
                tot = functools.reduce(jnp.add, [loaded[t0 + POOL_HALO + m - j] for j in range(w)])
                if t0 < w - 1:
                    frame = t * tm + t0 + m + TIME_STRIDE * lax.broadcasted_iota(jnp.int32, (SUBLANES, LANES), 0)
                    mean = tot / jnp.minimum(frame + 1, w).astype(F32)
                elif w & (w - 1) == 0:
                    mean = tot * (1.0 / w)
                else:
                    mean = tot / float(w)
                dlt_buf[gi, _interleaved(t0 + m), :] = mean - tok

    vreg_starts = [t0 + m for t0 in range(0, tm, TIME_BLOCK) for m in range(TIME_STRIDE)]
    mixer_stages = [functools.partial(conv_group, s, vreg_starts[i:i + CONV_ACCS])
                    for s in range(n_conv) for i in range(0, len(vreg_starts), CONV_ACCS)]
    mixer_stages.append(pool_in)
    mixer_stages += [functools.partial(pool_group, gi, w) for gi, w in enumerate(POOL_WINDOWS)]

    def mixer_end():
        cs = [conv_buf[s] for s in range(n_conv)]
        mu = jnp.sum(functools.reduce(jnp.add, cs), axis=-1, keepdims=True) / d_conv
        devs = [c - mu for c in cs]
        var = jnp.sum(functools.reduce(jnp.add, [dv * dv for dv in devs]), axis=-1, keepdims=True) / d_conv
        inv = lax.rsqrt(var + EPS)
        a_out = []
        for s in range(n_conv):
            lanes = slice(s * LANES, (s + 1) * LANES)
            c = devs[s] * inv * ln_g_ref[:, lanes] + ln_b_ref[:, lanes]
            a_out.append((c * jax.nn.sigmoid(c)).astype(BF16))
        pooled = [jnp.dot(dlt_buf[gi].astype(BF16), pool_w_ref[gi], preferred_element_type=F32)
                  for gi in range(n_pool)]
        p_out = (jnp.concatenate(pooled, axis=-1) * pool_s_ref[...]).astype(BF16)
        cat = jnp.concatenate(a_out + [p_out], axis=-1)
        h_prev[...] = live["x"] + jnp.dot(cat, w_out_ref[...], preferred_element_type=F32)

    ffn_begin()
    ffn_stages[0]()
    mixer_begin()
    ffn_rest = ffn_stages[1:]
    per = -(-len(mixer_stages) // len(ffn_rest))
    for i, ffn_stage in enumerate(ffn_rest):
        for stage in mixer_stages[i * per:(i + 1) * per]:
            stage()
        ffn_stage()
    mixer_end()


def _layer0(x2d, g, w_in_all, dw_w, dw_b, ln_g, ln_b, pool_w, pool_s, w_out_all, gf, w1_all, w3_all, w2_all,
            *, seq, layer, mixer_index):
    n_tok, d = x2d.shape
    w_in, w_out, w1, w3, w2 = (jax.ShapeDtypeStruct(w.shape[1:], w.dtype)
                               for w in (w_in_all, w_out_all, w1_all, w3_all, w2_all))
    d_conv = dw_w.shape[1]
    d_pool = pool_s.shape[1]
    d_ff = w1.shape[1]
    tm = TOKEN_TILE
    n_tiles = n_tok // tm
    assert d_pool == len(POOL_WINDOWS) * LANES and d_conv % LANES == 0
    assert tm % TIME_BLOCK == 0 and TIME_STRIDE % CONV_ACCS == 0 and seq % tm == 0 and n_tok % seq == 0
    stages = [_stage_shape(w, LAYER0_STAGE_BYTES) for w in (w_in, w_out, w1)]
    assert stages[1] == _stage_shape(w2, LAYER0_STAGE_BYTES) and stages[2] == _stage_shape(w3, LAYER0_STAGE_BYTES)
    resident = (w_in.size + w_out.size + pool_w.size + 3 * d * d_ff) * 2 + 6 * tm * d * 4 \
        + tm * w_in.shape[1] * 4 + (2 * tm + CONV_HALO) * d_conv * 4 + (2 * tm + POOL_HALO) * d_pool * 4 \
        + 3 * tm * FFN_CHUNK * 4 + 3 * STAGE_SLOTS * LAYER0_STAGE_BYTES
    hbm = pl.BlockSpec(memory_space=pl.ANY)
    return pl.pallas_call(
        functools.partial(_layer0_kernel, tm=tm, tiles_per_seq=seq // tm, n_tiles=n_tiles,
                          layer=layer, mixer_index=mixer_index),
        grid=(n_tiles + 1,),
        in_specs=[
            pl.BlockSpec((tm, d), lambda s: (jnp.minimum(s, n_tiles - 1), 0)),
            _resident((1, d), 1),
            hbm,
            _resident(dw_w.shape, 1),
            _resident((1, d_conv), 1),
            _resident((1, d_conv), 1),
            _resident((1, d_conv), 1),
            _resident(pool_w.shape, 1),
            _resident((1, d_pool), 1),
            hbm,
            _resident((1, d), 1),
            hbm, hbm, hbm,
        ],
        out_specs=pl.BlockSpec((tm, d), lambda s: (jnp.maximum(s - 1, 0), 0)),
        out_shape=jax.ShapeDtypeStruct((n_tok, d), F32),
        scratch_shapes=[
            pltpu.VMEM((d_conv // LANES, CONV_HALO + tm, LANES), F32),
            pltpu.VMEM((d_pool // LANES, POOL_HALO + tm, LANES), F32),
            pltpu.VMEM((d_conv // LANES, tm, LANES), F32),
            pltpu.VMEM((d_pool // LANES, tm, LANES), F32),
            pltpu.VMEM((tm, d), F32),
            pltpu.VMEM((tm, d_ff), BF16),
            pltpu.VMEM(w_in.shape, BF16),
            pltpu.VMEM(w_out.shape, BF16),
            pltpu.VMEM(w1.shape, BF16),
            pltpu.VMEM(w3.shape, BF16),
            pltpu.VMEM(w2.shape, BF16),
            pltpu.VMEM(stages[0], F32),
            pltpu.VMEM(stages[1], F32),
            pltpu.VMEM(stages[2], F32),
            pltpu.SemaphoreType.DMA((STAGE_SLOTS,)),
            pltpu.SemaphoreType.DMA((STAGE_SLOTS,)),
            pltpu.SemaphoreType.DMA((STAGE_SLOTS,)),
        ],
        compiler_params=pltpu.CompilerParams(
            dimension_semantics=("arbitrary",),
            vmem_limit_bytes=_vmem_limit(resident)),
        name="layer0_convpool_ffn",
    )(x2d, g, w_in_all, dw_w, dw_b, ln_g, ln_b, pool_w, pool_s, w_out_all, gf, w1_all, w3_all, w2_all)


def _hgrn_kernel(h_ref, g_ref, w_in_hbm, lb_logits_ref, gn_ref, w_out_hbm, o_ref,
                 z_ref, y_ref, state_ref, w_in_ref, w_out_ref, stage_in, stage_out, sem_in, sem_out,
                 *, tm, layer, mixer_index):
    t = pl.program_id(1)
    d_hg = y_ref.shape[1]
    dk = d_hg // HG_HEADS

    @pl.when((pl.program_id(0) == 0) & (t == 0))
    def _():
        _fetch_weights_as_bf16([(w_in_hbm.at[mixer_index], w_in_ref, stage_in, sem_in),
                                (w_out_hbm.at[mixer_index], w_out_ref, stage_out, sem_out)])

    @pl.when(t == 0)
    def _():
        state_ref[...] = jnp.zeros(state_ref.shape, F32)

    x = h_ref[...]
    n = _rmsnorm(x, g_ref[...]).astype(BF16)
    z_ref[...] = jnp.dot(n, w_in_ref[...], preferred_element_type=F32)

    logits = lb_logits_ref[...]
    e = jnp.exp(logits - jnp.max(logits, axis=0, keepdims=True))
    p = e / jnp.sum(e, axis=0, keepdims=True)
    lb = jnp.sum(p[0:layer + 1], axis=0, keepdims=True) - p[0:1]

    row = lax.broadcasted_iota(jnp.int32, (CHUNK, CHUNK), 0)
    col = lax.broadcasted_iota(jnp.int32, (CHUNK, CHUNK), 1)
    causal = row >= col
    sublane = lax.broadcasted_iota(jnp.int32, (SUBLANES, LANES), 0)
    gn = gn_ref[...]

    def chunk_step(c, carry):
        r0 = pl.multiple_of(c * CHUNK, CHUNK)
        rows = pl.ds(r0, CHUNK)
        q = z_ref[rows, 0:d_hg]
        f = lb + (1.0 - lb) * jax.nn.sigmoid(z_ref[rows, d_hg:2 * d_hg])
        log_f = jnp.log(f)
        k = 1.0 - f
        v = z_ref[rows, 2 * d_hg:3 * d_hg].astype(BF16)
        gate = z_ref[rows, 3 * d_hg:4 * d_hg]
        gate = gate * jax.nn.sigmoid(gate)

        b_tiles = []
        for lt in range(d_hg // LANES):
            running = None
            column = []
            for rt in range(CHUNK // SUBLANES):
                x = log_f[rt * SUBLANES:(rt + 1) * SUBLANES, lt * LANES:(lt + 1) * LANES]
                for s in (1, 2, 4):
                    x = x + jnp.where(sublane >= s, pltpu.roll(x, s, axis=0), 0.0)
                if running is not None:
                    x = x + running
                running = jnp.broadcast_to(x[SUBLANES - 1:SUBLANES, :], (SUBLANES, LANES))
                column.append(x)
            b_tiles.append(jnp.concatenate(column, axis=0))
        b = jnp.concatenate(b_tiles, axis=1)

        b_mid = 0.5 * (b[0:1, :] + b[CHUNK - 1:CHUNK, :])
        b_last = b[CHUNK - 1:CHUNK, :]
        q_rel = q * jnp.exp(b - b_mid)
        k_rel = k * jnp.exp(b_mid - b)
        q_in = (q_rel * jnp.exp(b_mid)).astype(BF16)
        k_up = (k_rel * jnp.exp(b_last - b_mid)).astype(BF16)
        q_rel = q_rel.astype(BF16)
        k_rel = k_rel.astype(BF16)
        decay_last = jnp.exp(b_last)

        heads = [slice(hd * dk, (hd + 1) * dk) for hd in range(HG_HEADS)]
        nt = (((1,), (1,)), ((), ()))
        tn = (((0,), (0,)), ((), ()))
        scores = [lax.dot_general(q_rel[:, sl], k_rel[:, sl], nt, preferred_element_type=F32)
                  for sl in heads]
        state_t = [state_ref[hd] for hd in range(HG_HEADS)]
        o_inter = [lax.dot_general(q_in[:, sl], st.astype(BF16), nt, preferred_element_type=F32)
                   for sl, st in zip(heads, state_t)]
        kv = [lax.dot_general(v[:, sl], k_up[:, sl], tn, preferred_element_type=F32) for sl in heads]
        for hd, sl in enumerate(heads):
            state_ref[hd] = state_t[hd] * decay_last[:, sl] + kv[hd]
        scores = [jnp.where(causal, s, 0.0).astype(BF16) for s in scores]
        outs = [jnp.dot(s, v[:, sl], preferred_element_type=F32) + oi
                for s, sl, oi in zip(scores, heads, o_inter)]
        for o, sl in zip(outs, heads):
            o = o * lax.rsqrt(jnp.mean(o * o, axis=-1, keepdims=True) + EPS) * gn[:, sl]
            y_ref[rows, sl] = (o * gate[:, sl]).astype(BF16)
        return carry

    lax.fori_loop(0, tm // CHUNK, chunk_step, 0, unroll=True)
    o_ref[...] = x + jnp.dot(y_ref[...], w_out_ref[...], preferred_element_type=F32)


def _hgrn_mixer(h, g, w_in_all, lb_logits, gn_g, w_out_all, *, layer, mixer_index):
    bsz, seq, d = h.shape
    w_in, w_out = (jax.ShapeDtypeStruct(w.shape[1:], w.dtype) for w in (w_in_all, w_out_all))
    d_hg = w_out.shape[0]
    dk = d_hg // HG_HEADS
    tm = TOKEN_TILE
    resident = (w_in.size + w_out.size) * 2 + 4 * tm * d * 4 + tm * 4 * d_hg * 4 * 2 \
        + tm * d_hg * 2 + HG_HEADS * dk * dk * 4 + 2 * STAGE_SLOTS * HGRN_STAGE_BYTES
    hbm = pl.BlockSpec(memory_space=pl.ANY)
    return pl.pallas_call(
        functools.partial(_hgrn_kernel, tm=tm, layer=layer, mixer_index=mixer_index),
        grid=(bsz, seq // tm),
        in_specs=[
            pl.BlockSpec((None, tm, d), lambda b, t: (b, t, 0)),
            _resident((1, d), 2),
            hbm,
            _resident(lb_logits.shape, 2),
            _resident((1, d_hg), 2),
            hbm,
        ],
        out_specs=pl.BlockSpec((None, tm, d), lambda b, t: (b, t, 0)),
        out_shape=jax.ShapeDtypeStruct((bsz, seq, d), F32),
        scratch_shapes=[
            pltpu.VMEM((tm, 4 * d_hg), F32),
            pltpu.VMEM((tm, d_hg), BF16),
            pltpu.VMEM((HG_HEADS, dk, dk), F32),
            pltpu.VMEM(w_in.shape, BF16),
            pltpu.VMEM(w_out.shape, BF16),
            pltpu.VMEM(_stage_shape(w_in, HGRN_STAGE_BYTES), F32),
            pltpu.VMEM(_stage_shape(w_out, HGRN_STAGE_BYTES), F32),
            pltpu.SemaphoreType.DMA((STAGE_SLOTS,)),
            pltpu.SemaphoreType.DMA((STAGE_SLOTS,)),
        ],
        compiler_params=pltpu.CompilerParams(
            dimension_semantics=("arbitrary", "arbitrary"),
            vmem_limit_bytes=_vmem_limit(resident)),
        name="hgrn_mixer",
    )(h, g, w_in_all, lb_logits, gn_g, w_out_all)


def kernel(x, norm_mix_g, norm_ffn_g, final_g, cp_w_in, cp_dw_w, cp_dw_b, cp_ln_g, cp_ln_b, cp_pool_w,
           cp_pool_scale, cp_w_out, hg_w_in, hg_lb_logits, hg_gn_g, hg_w_out, ffn_w1, ffn_w3, ffn_w2):
    bsz, seq, d = x.shape
    depth = norm_mix_g.shape[0]
    row = lambda a: a.reshape(1, -1).astype(F32)
    h = x
    for layer in range(depth):
        j = layer // 2
        last = layer == depth - 1
        if layer % 2 == 0:
            assert not last, "the fused conv/pool + FFN layer kernel has no final norm"
            h = _layer0(
                h.reshape(bsz * seq, d), row(norm_mix_g[layer]), cp_w_in, cp_dw_w[j],
                row(cp_dw_b[j]), row(cp_ln_g[j]), row(cp_ln_b[j]), cp_pool_w[j].astype(BF16),
                row(cp_pool_scale[j]), cp_w_out, row(norm_ffn_g[layer]), ffn_w1, ffn_w3, ffn_w2,
                seq=seq, layer=layer, mixer_index=j).reshape(bsz, seq, d)
        else:
            h = _hgrn_mixer(
                h, row(norm_mix_g[layer]), hg_w_in, hg_lb_logits.astype(F32),
                row(hg_gn_g[j]), hg_w_out, layer=layer, mixer_index=j)
            h = _ffn_block(
                h.reshape(bsz * seq, d), row(norm_ffn_g[layer]), ffn_w1, ffn_w3, ffn_w2, row(final_g),
                layer=layer, final_norm=last).reshape(bsz, seq, d)
    return h
```

```python
import functools
import itertools

import jax
import jax.numpy as jnp
from jax import lax
from jax.experimental import pallas as pl
from jax.experimental.pallas import tpu as pltpu

F32 = jnp.float32
BF16 = jnp.bfloat16

EPS = 1e-6
CHUNK = 64
CONV_WIDTH = 31
POOL_WINDOWS = (2, 4, 8, 16)
HG_HEADS = 8

V7X_VMEM_BYTES = 64 * 1024 * 1024
SUBLANES = 8
LANES = 128

TOKEN_TILE = 512
CONV_HALO = 32
POOL_HALO = 16
TIME_STRIDE = 4
TIME_BLOCK = SUBLANES * TIME_STRIDE
CONV_ACCS = 2
FFN_CHUNK = 1024
STAGE_SLOTS = 3
FFN_STAGE_BYTES = 2 << 20
HGRN_STAGE_BYTES = 1 << 20
LAYER0_STAGE_BYTES = 1 << 19


def _interleaved(start):
    return pl.ds(start, SUBLANES, stride=TIME_STRIDE)


def _vmem_limit(resident_bytes):
    return int(min(resident_bytes * 1.5 + (8 << 20), V7X_VMEM_BYTES - (6 << 20)))


def _rmsnorm(x, g):
    return x * lax.rsqrt(jnp.mean(x * x, axis=-1, keepdims=True) + EPS) * g


def _resident(shape, ngrid):
    zeros = (0,) * len(shape)
    if ngrid == 1:
        index_map = lambda i: zeros
    else:
        index_map = lambda b, t: zeros
    return pl.BlockSpec(shape, index_map, pipeline_mode=pl.Buffered(1))


def _stage_shape(w, stage_bytes):
    n_rows, n_cols = w.shape
    pack = 2 * SUBLANES
    rows = max(pack, (stage_bytes // (4 * n_cols)) // pack * pack)
    while n_rows % rows:
        rows -= pack
    return (STAGE_SLOTS, rows, n_cols)


def _fetch_weights_as_bf16(jobs):
    queues = {}
    for src, dst, stage, sem in jobs:
        queue = queues.setdefault(id(stage), [])
        rows = stage.shape[1]
        for i in range(dst.shape[0] // rows):
            queue.append((src, dst, stage, sem, i * rows, rows, len(queue) % stage.shape[0]))
    chunks = [c for group in itertools.zip_longest(*queues.values()) for c in group if c is not None]

    def copy(c):
        src, _, stage, sem, r0, rows, slot = c
        return pltpu.make_async_copy(src.at[pl.ds(r0, rows)], stage.at[slot], sem.at[slot])

    starts_after = [[] for _ in chunks]
    last_in_slot = {}
    for k, c in enumerate(chunks):
        key = (id(c[2]), c[6])
        if key in last_in_slot:
            starts_after[last_in_slot[key]].append(k)
        else:
            copy(c).start()
        last_in_slot[key] = k
    for k, c in enumerate(chunks):
        copy(c).wait()
        _, dst, stage, _, r0, rows, slot = c
        dst[pl.ds(r0, rows), :] = stage[slot].astype(BF16)
        for nxt in starts_after[k]:
            copy(chunks[nxt]).start()


def _ffn_kernel(h_ref, g_ref, w1_hbm, w3_hbm, w2_hbm, fg_ref, o_ref,
                w1_ref, w3_ref, w2_ref, stage_in, stage_out, sem_in, sem_out, *, layer, final_norm):
    @pl.when(pl.program_id(0) == 0)
    def _():
        _fetch_weights_as_bf16([(w1_hbm.at[layer], w1_ref, stage_in, sem_in),
                                (w3_hbm.at[layer], w3_ref, stage_in, sem_in),
                                (w2_hbm.at[layer], w2_ref, stage_out, sem_out)])

    h = h_ref[...]
    n = _rmsnorm(h, g_ref[...]).astype(BF16)
    a = jnp.dot(n, w1_ref[...], preferred_element_type=F32)
    b = jnp.dot(n, w3_ref[...], preferred_element_type=F32)
    gated = (a * jax.nn.sigmoid(a) * b).astype(BF16)
    out = h + jnp.dot(gated, w2_ref[...], preferred_element_type=F32)
    if final_norm:
        out = _rmsnorm(out, fg_ref[...])
    o_ref[...] = out


def _ffn_block(h2d, g, w1_all, w3_all, w2_all, final_g, *, layer, final_norm):
    n_tok, d = h2d.shape
    w1, w3, w2 = (jax.ShapeDtypeStruct(w.shape[1:], w.dtype) for w in (w1_all, w3_all, w2_all))
    d_ff = w1.shape[1]
    tm = TOKEN_TILE
    stages = [_stage_shape(w1, FFN_STAGE_BYTES), _stage_shape(w2, FFN_STAGE_BYTES)]
    assert stages[0] == _stage_shape(w3, FFN_STAGE_BYTES)
    resident = 3 * d * d_ff * 2 + 4 * tm * d * 4 + 3 * tm * d_ff * 4 + 2 * STAGE_SLOTS * FFN_STAGE_BYTES
    hbm = pl.BlockSpec(memory_space=pl.ANY)
    return pl.pallas_call(
        functools.partial(_ffn_kernel, layer=layer, final_norm=final_norm),
        grid=(n_tok // tm,),
        in_specs=[
            pl.BlockSpec((tm, d), lambda i: (i, 0)),
            _resident((1, d), 1),
            hbm, hbm, hbm,
            _resident((1, d), 1),
        ],
        out_specs=pl.BlockSpec((tm, d), lambda i: (i, 0)),
        out_shape=jax.ShapeDtypeStruct((n_tok, d), F32),
        scratch_shapes=[
            pltpu.VMEM(w1.shape, BF16),
            pltpu.VMEM(w3.shape, BF16),
            pltpu.VMEM(w2.shape, BF16),
            pltpu.VMEM(stages[0], F32),
            pltpu.VMEM(stages[1], F32),
            pltpu.SemaphoreType.DMA((STAGE_SLOTS,)),
            pltpu.SemaphoreType.DMA((STAGE_SLOTS,)),
        ],
        compiler_params=pltpu.CompilerParams(
            dimension_semantics=("arbitrary",),
            vmem_limit_bytes=_vmem_limit(resident)),
        name="ffn_block",
    )(h2d, g, w1_all, w3_all, w2_all, final_g)


def _layer0_kernel(x_ref, g_ref, w_in_hbm, dw_w_ref, dw_b_ref, ln_g_ref, ln_b_ref, pool_w_ref, pool_s_ref,
                   w_out_hbm, gf_ref, w1_hbm, w3_hbm, w2_hbm, o_ref,
                   a_ext, u_ext, conv_buf, dlt_buf, h_prev, gated_ref, w_in_ref, w_out_ref, w1_ref, w3_ref, w2_ref,
                   stage_in, stage_d, stage_ff, sem_in, sem_d, sem_ff,
                   *, tm, tiles_per_seq, n_tiles, layer, mixer_index):
    step = pl.program_id(0)
    t = lax.rem(jnp.minimum(step, n_tiles - 1), tiles_per_seq)
    n_conv = a_ext.shape[0]
    n_pool = u_ext.shape[0]
    d_conv = n_conv * LANES
    d_ff = w1_ref.shape[1]

    @pl.when(step == 0)
    def _():
        h_prev[...] = jnp.zeros(h_prev.shape, F32)
        a_ext[:, tm:tm + CONV_HALO, :] = jnp.zeros((n_conv, CONV_HALO, LANES), F32)
        u_ext[:, tm:tm + POOL_HALO, :] = jnp.zeros((n_pool, POOL_HALO, LANES), F32)
        _fetch_weights_as_bf16([(w_in_hbm.at[mixer_index], w_in_ref, stage_in, sem_in),
                                (w_out_hbm.at[mixer_index], w_out_ref, stage_d, sem_d),
                                (w2_hbm.at[layer], w2_ref, stage_d, sem_d),
                                (w1_hbm.at[layer], w1_ref, stage_ff, sem_ff),
                                (w3_hbm.at[layer], w3_ref, stage_ff, sem_ff)])

    live = {}

    def ffn_begin():
        live["hp"] = h_prev[...]
        live["nf"] = _rmsnorm(live["hp"], gf_ref[...]).astype(BF16)

    ff_chunks = [(c0, min(c0 + FFN_CHUNK, d_ff)) for c0 in range(0, d_ff, FFN_CHUNK)]
    gate_in, up_in = {}, {}

    def ffn_gate(j):
        c0, c1 = ff_chunks[j]
        gate_in[j] = jnp.dot(live["nf"], w1_ref[:, c0:c1], preferred_element_type=F32)

    def ffn_up(j):
        c0, c1 = ff_chunks[j]
        up_in[j] = jnp.dot(live["nf"], w3_ref[:, c0:c1], preferred_element_type=F32)

    def ffn_act(j):
        c0, c1 = ff_chunks[j]
        a, b = gate_in.pop(j), up_in.pop(j)
        gated_ref[:, c0:c1] = (a * jax.nn.sigmoid(a) * b).astype(BF16)

    def ffn_out():
        o_ref[...] = live["hp"] + jnp.dot(gated_ref[...], w2_ref[...], preferred_element_type=F32)

    ffn_stages = []
    for j in range(len(ff_chunks)):
        ffn_stages += [functools.partial(ffn_gate, j), functools.partial(ffn_up, j), functools.partial(ffn_act, j)]
    ffn_stages.append(ffn_out)

    def mixer_begin():
        x = x_ref[...]
        n = _rmsnorm(x, g_ref[...]).astype(BF16)
        z = jnp.dot(n, w_in_ref[:, :2 * d_conv], preferred_element_type=F32)
        live["x"], live["n"] = x, n

        inside = t > 0
        a_ext[:, 0:CONV_HALO, :] = jnp.where(inside, a_ext[:, tm:tm + CONV_HALO, :], 0.0)
        u_ext[:, 0:POOL_HALO, :] = jnp.where(inside, u_ext[:, tm:tm + POOL_HALO, :], 0.0)

        glu = z[:, :d_conv] * jax.nn.sigmoid(z[:, d_conv:2 * d_conv])
        for s in range(n_conv):
            a_ext[s, CONV_HALO:CONV_HALO + tm, :] = glu[:, s * LANES:(s + 1) * LANES]

    def pool_in():
        u = jnp.dot(live["n"], w_in_ref[:, 2 * d_conv:], preferred_element_type=F32)
        for s in range(n_pool):
            u_ext[s, POOL_HALO:POOL_HALO + tm, :] = u[:, s * LANES:(s + 1) * LANES]

    base = CONV_HALO - (CONV_WIDTH - 1)

    always = step >= 0
    chain = []

    def conv_group(s, starts):
        lanes = slice(s * LANES, (s + 1) * LANES)
        bias = jnp.broadcast_to(dw_b_ref[:, lanes], (SUBLANES, LANES))
        if chain:
            bias = jnp.where(always, bias, chain[-1])
        accs = [bias] * len(starts)
        loaded = {}
        for k in range(CONV_WIDTH):
            wk = jnp.broadcast_to(dw_w_ref[k:k + 1, lanes], (SUBLANES, LANES))
            for i, r in enumerate(starts):
                src = r + base + k
                if src not in loaded:
                    loaded[src] = a_ext[s, _interleaved(src), :]
                accs[i] = accs[i] + loaded[src] * wk
        for i, r in enumerate(starts):
            conv_buf[s, _interleaved(r), :] = accs[i]
        chain.append(accs[-1])

    def pool_group(gi, w):
        for t0 in range(0, tm, TIME_BLOCK):
            loaded = {}
            for m in range(TIME_STRIDE):
                for j in range(w):
                    src = t0 + POOL_HALO + m - j
                    if src not in loaded:
                        loaded[src] = u_ext[gi, _interleaved(src), :]
                tok = loaded[t0 + POOL_HALO + m]
                tot = functools.reduce(jnp.add, [loaded[t0 + POOL_HALO + m - j] for j in range(w)])
                if t0 < w - 1:
                    frame = t * tm + t0 + m + TIME_STRIDE * lax.broadcasted_iota(jnp.int32, (SUBLANES, LANES), 0)
                    mean = tot / jnp.minimum(frame + 1, w).astype(F32)
                elif w & (w - 1) == 0:
                    mean = tot * (1.0 / w)
                else:
                    mean = tot / float(w)
                dlt_buf[gi, _interleaved(t0 + m), :] = mean - tok

    vreg_starts = [t0 + m for t0 in range(0, tm, TIME_BLOCK) for m in range(TIME_STRIDE)]
    mixer_stages = [functools.partial(conv_group, s, vreg_starts[i:i + CONV_ACCS])
                    for s in range(n_conv) for i in range(0, len(vreg_starts), CONV_ACCS)]
    mixer_stages.append(pool_in)
    mixer_stages += [functools.partial(pool_group, gi, w) for gi, w in enumerate(POOL_WINDOWS)]

    def mixer_end():
        cs = [conv_buf[s] for s in range(n_conv)]
        mu = jnp.sum(functools.reduce(jnp.add, cs), axis=-1, keepdims=True) / d_conv
        devs = [c - mu for c in cs]
        var = jnp.sum(functools.reduce(jnp.add, [dv * dv for dv in devs]), axis=-1, keepdims=True) / d_conv
        inv = lax.rsqrt(var + EPS)
        a_out = []
        for s in range(n_conv):
            lanes = slice(s * LANES, (s + 1) * LANES)
            c = devs[s] * inv * ln_g_ref[:, lanes] + ln_b_ref[:, lanes]
            a_out.append((c * jax.nn.sigmoid(c)).astype(BF16))
        pooled = [jnp.dot(dlt_buf[gi].astype(BF16), pool_w_ref[gi], preferred_element_type=F32)
                  for gi in range(n_pool)]
        p_out = (jnp.concatenate(pooled, axis=-1) * pool_s_ref[...]).astype(BF16)
        cat = jnp.concatenate(a_out + [p_out], axis=-1)
        h_prev[...] = live["x"] + jnp.dot(cat, w_out_ref[...], preferred_element_type=F32)

    ffn_begin()
    ffn_stages[0]()
    mixer_begin()
    ffn_rest = ffn_stages[1:]
    per = -(-len(mixer_stages) // len(ffn_rest))
    for i, ffn_stage in enumerate(ffn_rest):
        for stage in mixer_stages[i * per:(i + 1) * per]:
            stage()
        ffn_stage()
    mixer_end()


def _layer0(x2d, g, w_in_all, dw_w, dw_b, ln_g, ln_b, pool_w, pool_s, w_out_all, gf, w1_all, w3_all, w2_all,
            *, seq, layer, mixer_index):
    n_tok, d = x2d.shape
    w_in, w_out, w1, w3, w2 = (jax.ShapeDtypeStruct(w.shape[1:], w.dtype)
                               for w in (w_in_all, w_out_all, w1_all, w3_all, w2_all))
    d_conv = dw_w.shape[1]
    d_pool = pool_s.shape[1]
    d_ff = w1.shape[1]
    tm = TOKEN_TILE
    n_tiles = n_tok // tm
    assert d_pool == len(POOL_WINDOWS) * LANES and d_conv % LANES == 0
    assert tm % TIME_BLOCK == 0 and TIME_STRIDE % CONV_ACCS == 0 and seq % tm == 0 and n_tok % seq == 0
    stages = [_stage_shape(w, LAYER0_STAGE_BYTES) for w in (w_in, w_out, w1)]
    assert stages[1] == _stage_shape(w2, LAYER0_STAGE_BYTES) and stages[2] == _stage_shape(w3, LAYER0_STAGE_BYTES)
    resident = (w_in.size + w_out.size + pool_w.size + 3 * d * d_ff) * 2 + 6 * tm * d * 4 \
        + tm * w_in.shape[1] * 4 + (2 * tm + CONV_HALO) * d_conv * 4 + (2 * tm + POOL_HALO) * d_pool * 4 \
        + 3 * tm * FFN_CHUNK * 4 + 3 * STAGE_SLOTS * LAYER0_STAGE_BYTES
    hbm = pl.BlockSpec(memory_space=pl.ANY)
    return pl.pallas_call(
        functools.partial(_layer0_kernel, tm=tm, tiles_per_seq=seq // tm, n_tiles=n_tiles,
                          layer=layer, mixer_index=mixer_index),
        grid=(n_tiles + 1,),
        in_specs=[
            pl.BlockSpec((tm, d), lambda s: (jnp.minimum(s, n_tiles - 1), 0)),
            _resident((1, d), 1),
            hbm,
            _resident(dw_w.shape, 1),
            _resident((1, d_conv), 1),
            _resident((1, d_conv), 1),
            _resident((1, d_conv), 1),
            _resident(pool_w.shape, 1),
            _resident((1, d_pool), 1),
            hbm,
            _resident((1, d), 1),
            hbm, hbm, hbm,
        ],
        out_specs=pl.BlockSpec((tm, d), lambda s: (jnp.maximum(s - 1, 0), 0)),
        out_shape=jax.ShapeDtypeStruct((n_tok, d), F32),
        scratch_shapes=[
            pltpu.VMEM((d_conv // LANES, CONV_HALO + tm, LANES), F32),
            pltpu.VMEM((d_pool // LANES, POOL_HALO + tm, LANES), F32),
            pltpu.VMEM((d_conv // LANES, tm, LANES), F32),
            pltpu.VMEM((d_pool // LANES, tm, LANES), F32),
            pltpu.VMEM((tm, d), F32),
            pltpu.VMEM((tm, d_ff), BF16),
            pltpu.VMEM(w_in.shape, BF16),
            pltpu.VMEM(w_out.shape, BF16),
            pltpu.VMEM(w1.shape, BF16),
            pltpu.VMEM(w3.shape, BF16),
            pltpu.VMEM(w2.shape, BF16),
            pltpu.VMEM(stages[0], F32),
            pltpu.VMEM(stages[1], F32),
            pltpu.VMEM(stages[2], F32),
            pltpu.SemaphoreType.DMA((STAGE_SLOTS,)),
            pltpu.SemaphoreType.DMA((STAGE_SLOTS,)),
            pltpu.SemaphoreType.DMA((STAGE_SLOTS,)),
        ],
        compiler_params=pltpu.CompilerParams(
            dimension_semantics=("arbitrary",),
            vmem_limit_bytes=_vmem_limit(resident)),
        name="layer0_convpool_ffn",
    )(x2d, g, w_in_all, dw_w, dw_b, ln_g, ln_b, pool_w, pool_s, w_out_all, gf, w1_all, w3_all, w2_all)


def _hgrn_kernel(h_ref, g_ref, w_in_hbm, lb_logits_ref, gn_ref, w_out_hbm, o_ref,
                 z_ref, y_ref, state_ref, w_in_ref, w_out_ref, stage_in, stage_out, sem_in, sem_out,
                 *, tm, layer, mixer_index):
    t = pl.program_id(1)
    d_hg = y_ref.shape[1]
    dk = d_hg // HG_HEADS

    @pl.when((pl.program_id(0) == 0) & (t == 0))
    def _():
        _fetch_weights_as_bf16([(w_in_hbm.at[mixer_index], w_in_ref, stage_in, sem_in),
                                (w_out_hbm.at[mixer_index], w_out_ref, stage_out, sem_out)])

    @pl.when(t == 0)
    def _():
        state_ref[...] = jnp.zeros(state_ref.shape, F32)

    x = h_ref[...]
    n = _rmsnorm(x, g_ref[...]).astype(BF16)
    z_ref[...] = jnp.dot(n, w_in_ref[...], preferred_element_type=F32)

    logits = lb_logits_ref[...]
    e = jnp.exp(logits - jnp.max(logits, axis=0, keepdims=True))
    p = e / jnp.sum(e, axis=0, keepdims=True)
    lb = jnp.sum(p[0:layer + 1], axis=0, keepdims=True) - p[0:1]

    row = lax.broadcasted_iota(jnp.int32, (CHUNK, CHUNK), 0)
    col = lax.broadcasted_iota(jnp.int32, (CHUNK, CHUNK), 1)
    causal = row >= col
    sublane = lax.broadcasted_iota(jnp.int32, (SUBLANES, LANES), 0)
    gn = gn_ref[...]

    def chunk_step(c, carry):
        r0 = pl.multiple_of(c * CHUNK, CHUNK)
        rows = pl.ds(r0, CHUNK)
        q = z_ref[rows, 0:d_hg]
        f = lb + (1.0 - lb) * jax.nn.sigmoid(z_ref[rows, d_hg:2 * d_hg])
        log_f = jnp.log(f)
        k = 1.0 - f
        v = z_ref[rows, 2 * d_hg:3 * d_hg].astype(BF16)
        gate = z_ref[rows, 3 * d_hg:4 * d_hg]
        gate = gate * jax.nn.sigmoid(gate)

        b_tiles = []
        for lt in range(d_hg // LANES):
            running = None
            column = []
            for rt in range(CHUNK // SUBLANES):
                x = log_f[rt * SUBLANES:(rt + 1) * SUBLANES, lt * LANES:(lt + 1) * LANES]
                for s in (1, 2, 4):
                    x = x + jnp.where(sublane >= s, pltpu.roll(x, s, axis=0), 0.0)
                if running is not None:
                    x = x + running
                running = jnp.broadcast_to(x[SUBLANES - 1:SUBLANES, :], (SUBLANES, LANES))
                column.append(x)
            b_tiles.append(jnp.concatenate(column, axis=0))
        b = jnp.concatenate(b_tiles, axis=1)

        b_mid = b[CHUNK // 2 - 1:CHUNK // 2, :]
        b_last = b[CHUNK - 1:CHUNK, :]
        q_rel = q * jnp.exp(b - b_mid)
        k_rel = k * jnp.exp(b_mid - b)
        q_in = (q_rel * jnp.exp(b_mid)).astype(BF16)
        k_up = (k_rel * jnp.exp(b_last - b_mid)).astype(BF16)
        q_rel = q_rel.astype(BF16)
        k_rel = k_rel.astype(BF16)
        decay_last = jnp.exp(b_last)

        heads = [slice(hd * dk, (hd + 1) * dk) for hd in range(HG_HEADS)]
        nt = (((1,), (1,)), ((), ()))
        tn = (((0,), (0,)), ((), ()))
        scores = [lax.dot_general(q_rel[:, sl], k_rel[:, sl], nt, preferred_element_type=F32)
                  for sl in heads]
        state_t = [state_ref[hd] for hd in range(HG_HEADS)]
        o_inter = [lax.dot_general(q_in[:, sl], st.astype(BF16), nt, preferred_element_type=F32)
                   for sl, st in zip(heads, state_t)]
        kv = [lax.dot_general(v[:, sl], k_up[:, sl], tn, preferred_element_type=F32) for sl in heads]
        for hd, sl in enumerate(heads):
            state_ref[hd] = state_t[hd] * decay_last[:, sl] + kv[hd]
        scores = [jnp.where(causal, s, 0.0).astype(BF16) for s in scores]
        outs = [jnp.dot(s, v[:, sl], preferred_element_type=F32) + oi
                for s, sl, oi in zip(scores, heads, o_inter)]
        for o, sl in zip(outs, heads):
            o = o * lax.rsqrt(jnp.mean(o * o, axis=-1, keepdims=True) + EPS) * gn[:, sl]
            y_ref[rows, sl] = (o * gate[:, sl]).astype(BF16)
        return carry

    lax.fori_loop(0, tm // CHUNK, chunk_step, 0, unroll=True)
    o_ref[...] = x + jnp.dot(y_ref[...], w_out_ref[...], preferred_element_type=F32)


def _hgrn_mixer(h, g, w_in_all, lb_logits, gn_g, w_out_all, *, layer, mixer_index):
    bsz, seq, d = h.shape
    w_in, w_out = (jax.ShapeDtypeStruct(w.shape[1:], w.dtype) for w in (w_in_all, w_out_all))
    d_hg = w_out.shape[0]
    dk = d_hg // HG_HEADS
    tm = TOKEN_TILE
    resident = (w_in.size + w_out.size) * 2 + 4 * tm * d * 4 + tm * 4 * d_hg * 4 * 2 \
        + tm * d_hg * 2 + HG_HEADS * dk * dk * 4 + 2 * STAGE_SLOTS * HGRN_STAGE_BYTES
    hbm = pl.BlockSpec(memory_space=pl.ANY)
    return pl.pallas_call(
        functools.partial(_hgrn_kernel, tm=tm, layer=layer, mixer_index=mixer_index),
        grid=(bsz, seq // tm),
        in_specs=[
            pl.BlockSpec((None, tm, d), lambda b, t: (b, t, 0)),
            _resident((1, d), 2),
            hbm,
            _resident(lb_logits.shape, 2),
            _resident((1, d_hg), 2),
            hbm,
        ],
        out_specs=pl.BlockSpec((None, tm, d), lambda b, t: (b, t, 0)),
        out_shape=jax.ShapeDtypeStruct((bsz, seq, d), F32),
        scratch_shapes=[
            pltpu.VMEM((tm, 4 * d_hg), F32),
            pltpu.VMEM((tm, d_hg), BF16),
            pltpu.VMEM((HG_HEADS, dk, dk), F32),
            pltpu.VMEM(w_in.shape, BF16),
            pltpu.VMEM(w_out.shape, BF16),
            pltpu.VMEM(_stage_shape(w_in, HGRN_STAGE_BYTES), F32),
            pltpu.VMEM(_stage_shape(w_out, HGRN_STAGE_BYTES), F32),
            pltpu.SemaphoreType.DMA((STAGE_SLOTS,)),
            pltpu.SemaphoreType.DMA((STAGE_SLOTS,)),
        ],
        compiler_params=pltpu.CompilerParams(
            dimension_semantics=("arbitrary", "arbitrary"),
            vmem_limit_bytes=_vmem_limit(resident)),
        name="hgrn_mixer",
    )(h, g, w_in_all, lb_logits, gn_g, w_out_all)


def kernel(x, norm_mix_g, norm_ffn_g, final_g, cp_w_in, cp_dw_w, cp_dw_b, cp_ln_g, cp_ln_b, cp_pool_w,
           cp_pool_scale, cp_w_out, hg_w_in, hg_lb_logits, hg_gn_g, hg_w_out, ffn_w1, ffn_w3, ffn_w2):
    bsz, seq, d = x.shape
    depth = norm_mix_g.shape[0]
    row = lambda a: a.reshape(1, -1).astype(F32)
    h = x
    for layer in range(depth):
        j = layer // 2
        last = layer == depth - 1
        if layer % 2 == 0:
            assert not last, "the fused conv/pool + FFN layer kernel has no final norm"
            h = _layer0(
                h.reshape(bsz * seq, d), row(norm_mix_g[layer]), cp_w_in, cp_dw_w[j],
                row(cp_dw_b[j]), row(cp_ln_g[j]), row(cp_ln_b[j]), cp_pool_w[j].astype(BF16),
                row(cp_pool_scale[j]), cp_w_out, row(norm_ffn_g[layer]), ffn_w1, ffn_w3, ffn_w2,
                seq=seq, layer=layer, mixer_index=j).reshape(bsz, seq, d)
        else:
            h = _hgrn_mixer(
                h, row(norm_mix_g[layer]), hg_w_in, hg_lb_logits.astype(F32),
                row(hg_gn_g[j]), hg_w_out, layer=layer, mixer_index=j)
            h = _ffn_block(
                h.reshape(bsz * seq, d), row(norm_ffn_g[layer]), ffn_w1, ffn_w3, ffn_w2, row(final_g),
                layer=layer, final_norm=last).reshape(bsz, seq, d)
    return h
```

```python
import functools
import itertools

import jax
import jax.numpy as jnp
from jax import lax
from jax.experimental import pallas as pl
from jax.experimental.pallas import tpu as pltpu

F32 = jnp.float32
BF16 = jnp.bfloat16

EPS = 1e-6
CHUNK = 64
CONV_WIDTH = 31
POOL_WINDOWS = (2, 4, 8, 16)
HG_HEADS = 8

V7X_VMEM_BYTES = 64 * 1024 * 1024
SUBLANES = 8
LANES = 128

TOKEN_TILE = 512
CONV_HALO = 32
POOL_HALO = 16
TIME_STRIDE = 4
TIME_BLOCK = SUBLANES * TIME_STRIDE
CONV_ACCS = 2
FFN_CHUNK = 1024
STAGE_SLOTS = 3
FFN_STAGE_BYTES = 2 << 20
HGRN_STAGE_BYTES = 1 << 20
LAYER0_STAGE_BYTES = 1 << 19


def _interleaved(start):
    return pl.ds(start, SUBLANES, stride=TIME_STRIDE)


def _vmem_limit(resident_bytes):
    return int(min(resident_bytes * 1.5 + (8 << 20), V7X_VMEM_BYTES - (6 << 20)))


def _rmsnorm(x, g):
    return x * lax.rsqrt(jnp.mean(x * x, axis=-1, keepdims=True) + EPS) * g


def _resident(shape, ngrid):
    zeros = (0,) * len(shape)
    if ngrid == 1:
        index_map = lambda i: zeros
    else:
        index_map = lambda b, t: zeros
    return pl.BlockSpec(shape, index_map, pipeline_mode=pl.Buffered(1))


def _stage_shape(w, stage_bytes):
    n_rows, n_cols = w.shape
    pack = 2 * SUBLANES
    rows = max(pack, (stage_bytes // (4 * n_cols)) // pack * pack)
    while n_rows % rows:
        rows -= pack
    return (STAGE_SLOTS, rows, n_cols)


def _fetch_weights_as_bf16(jobs):
    queues = {}
    for src, dst, stage, sem in jobs:
        queue = queues.setdefault(id(stage), [])
        rows = stage.shape[1]
        for i in range(dst.shape[0] // rows):
            queue.append((src, dst, stage, sem, i * rows, rows, len(queue) % stage.shape[0]))
    chunks = [c for group in itertools.zip_longest(*queues.values()) for c in group if c is not None]

    def copy(c):
        src, _, stage, sem, r0, rows, slot = c
        return pltpu.make_async_copy(src.at[pl.ds(r0, rows)], stage.at[slot], sem.at[slot])

    starts_after = [[] for _ in chunks]
    last_in_slot = {}
    for k, c in enumerate(chunks):
        key = (id(c[2]), c[6])
        if key in last_in_slot:
            starts_after[last_in_slot[key]].append(k)
        else:
            copy(c).start()
        last_in_slot[key] = k
    for k, c in enumerate(chunks):
        copy(c).wait()
        _, dst, stage, _, r0, rows, slot = c
        dst[pl.ds(r0, rows), :] = stage[slot].astype(BF16)
        for nxt in starts_after[k]:
            copy(chunks[nxt]).start()


def _ffn_kernel(h_ref, g_ref, w1_hbm, w3_hbm, w2_hbm, fg_ref, o_ref,
                w1_ref, w3_ref, w2_ref, stage_in, stage_out, sem_in, sem_out, *, layer, final_norm):
    @pl.when(pl.program_id(0) == 0)
    def _():
        _fetch_weights_as_bf16([(w1_hbm.at[layer], w1_ref, stage_in, sem_in),
                                (w3_hbm.at[layer], w3_ref, stage_in, sem_in),
                                (w2_hbm.at[layer], w2_ref, stage_out, sem_out)])

    half = h_ref.shape[0] // 2
    rows = [slice(0, half), slice(half, 2 * half)]
    h = [h_ref[r, :] for r in rows]
    n = [_rmsnorm(hh, g_ref[...]).astype(BF16) for hh in h]
    ab = [(jnp.dot(nn, w1_ref[...], preferred_element_type=F32),
           jnp.dot(nn, w3_ref[...], preferred_element_type=F32)) for nn in n]
    for r, hh, (a, b) in zip(rows, h, ab):
        gated = (a * jax.nn.sigmoid(a) * b).astype(BF16)
        out = hh + jnp.dot(gated, w2_ref[...], preferred_element_type=F32)
        if final_norm:
            out = _rmsnorm(out, fg_ref[...])
        o_ref[r, :] = out


def _ffn_block(h2d, g, w1_all, w3_all, w2_all, final_g, *, layer, final_norm):
    n_tok, d = h2d.shape
    w1, w3, w2 = (jax.ShapeDtypeStruct(w.shape[1:], w.dtype) for w in (w1_all, w3_all, w2_all))
    d_ff = w1.shape[1]
    tm = TOKEN_TILE
    stages = [_stage_shape(w1, FFN_STAGE_BYTES), _stage_shape(w2, FFN_STAGE_BYTES)]
    assert stages[0] == _stage_shape(w3, FFN_STAGE_BYTES)
    resident = 3 * d * d_ff * 2 + 4 * tm * d * 4 + 3 * tm * d_ff * 4 + 2 * STAGE_SLOTS * FFN_STAGE_BYTES
    hbm = pl.BlockSpec(memory_space=pl.ANY)
    return pl.pallas_call(
        functools.partial(_ffn_kernel, layer=layer, final_norm=final_norm),
        grid=(n_tok // tm,),
        in_specs=[
            pl.BlockSpec((tm, d), lambda i: (i, 0)),
            _resident((1, d), 1),
            hbm, hbm, hbm,
            _resident((1, d), 1),
        ],
        out_specs=pl.BlockSpec((tm, d), lambda i: (i, 0)),
        out_shape=jax.ShapeDtypeStruct((n_tok, d), F32),
        scratch_shapes=[
            pltpu.VMEM(w1.shape, BF16),
            pltpu.VMEM(w3.shape, BF16),
            pltpu.VMEM(w2.shape, BF16),
            pltpu.VMEM(stages[0], F32),
            pltpu.VMEM(stages[1], F32),
            pltpu.SemaphoreType.DMA((STAGE_SLOTS,)),
            pltpu.SemaphoreType.DMA((STAGE_SLOTS,)),
        ],
        compiler_params=pltpu.CompilerParams(
            dimension_semantics=("arbitrary",),
            vmem_limit_bytes=_vmem_limit(resident)),
        name="ffn_block",
    )(h2d, g, w1_all, w3_all, w2_all, final_g)


def _layer0_kernel(x_ref, g_ref, w_in_hbm, dw_w_ref, dw_b_ref, ln_g_ref, ln_b_ref, pool_w_ref, pool_s_ref,
                   w_out_hbm, gf_ref, w1_hbm, w3_hbm, w2_hbm, o_ref,
                   a_ext, u_ext, conv_buf, dlt_buf, h_prev, gated_ref, w_in_ref, w_out_ref, w1_ref, w3_ref, w2_ref,
                   stage_in, stage_d, stage_ff, sem_in, sem_d, sem_ff,
                   *, tm, tiles_per_seq, n_tiles, layer, mixer_index):
    step = pl.program_id(0)
    t = lax.rem(jnp.minimum(step, n_tiles - 1), tiles_per_seq)
    n_conv = a_ext.shape[0]
    n_pool = u_ext.shape[0]
    d_conv = n_conv * LANES
    d_ff = w1_ref.shape[1]

    @pl.when(step == 0)
    def _():
        h_prev[...] = jnp.zeros(h_prev.shape, F32)
        a_ext[:, tm:tm + CONV_HALO, :] = jnp.zeros((n_conv, CONV_HALO, LANES), F32)
        u_ext[:, tm:tm + POOL_HALO, :] = jnp.zeros((n_pool, POOL_HALO, LANES), F32)
        _fetch_weights_as_bf16([(w_in_hbm.at[mixer_index], w_in_ref, stage_in, sem_in),
                                (w_out_hbm.at[mixer_index], w_out_ref, stage_d, sem_d),
                                (w2_hbm.at[layer], w2_ref, stage_d, sem_d),
                                (w1_hbm.at[layer], w1_ref, stage_ff, sem_ff),
                                (w3_hbm.at[layer], w3_ref, stage_ff, sem_ff)])

    live = {}

    def ffn_begin():
        live["hp"] = h_prev[...]
        live["nf"] = _rmsnorm(live["hp"], gf_ref[...]).astype(BF16)

    ff_chunks = [(c0, min(c0 + FFN_CHUNK, d_ff)) for c0 in range(0, d_ff, FFN_CHUNK)]
    gate_in, up_in = {}, {}

    def ffn_gate(j):
        c0, c1 = ff_chunks[j]
        gate_in[j] = jnp.dot(live["nf"], w1_ref[:, c0:c1], preferred_element_type=F32)

    def ffn_up(j):
        c0, c1 = ff_chunks[j]
        up_in[j] = jnp.dot(live["nf"], w3_ref[:, c0:c1], preferred_element_type=F32)

    def ffn_act(j):
        c0, c1 = ff_chunks[j]
        a, b = gate_in.pop(j), up_in.pop(j)
        gated_ref[:, c0:c1] = (a * jax.nn.sigmoid(a) * b).astype(BF16)

    def ffn_out():
        o_ref[...] = live["hp"] + jnp.dot(gated_ref[...], w2_ref[...], preferred_element_type=F32)

    ffn_stages = []
    for j in range(len(ff_chunks)):
        ffn_stages += [functools.partial(ffn_gate, j), functools.partial(ffn_up, j), functools.partial(ffn_act, j)]
    ffn_stages.append(ffn_out)

    def mixer_begin():
        x = x_ref[...]
        n = _rmsnorm(x, g_ref[...]).astype(BF16)
        z = jnp.dot(n, w_in_ref[:, :2 * d_conv], preferred_element_type=F32)
        live["x"], live["n"] = x, n

        inside = t > 0
        a_ext[:, 0:CONV_HALO, :] = jnp.where(inside, a_ext[:, tm:tm + CONV_HALO, :], 0.0)
        u_ext[:, 0:POOL_HALO, :] = jnp.where(inside, u_ext[:, tm:tm + POOL_HALO, :], 0.0)

        glu = z[:, :d_conv] * jax.nn.sigmoid(z[:, d_conv:2 * d_conv])
        for s in range(n_conv):
            a_ext[s, CONV_HALO:CONV_HALO + tm, :] = glu[:, s * LANES:(s + 1) * LANES]

    def pool_in():
        u = jnp.dot(live["n"], w_in_ref[:, 2 * d_conv:], preferred_element_type=F32)
        for s in range(n_pool):
            u_ext[s, POOL_HALO:POOL_HALO + tm, :] = u[:, s * LANES:(s + 1) * LANES]

    base = CONV_HALO - (CONV_WIDTH - 1)

    always = step >= 0
    chain = []

    def conv_group(s, starts):
        lanes = slice(s * LANES, (s + 1) * LANES)
        bias = jnp.broadcast_to(dw_b_ref[:, lanes], (SUBLANES, LANES))
        if chain:
            bias = jnp.where(always, bias, chain[-1])
        accs = [bias] * len(starts)
        loaded = {}
        for k in range(CONV_WIDTH):
            wk = jnp.broadcast_to(dw_w_ref[k:k + 1, lanes], (SUBLANES, LANES))
            for i, r in enumerate(starts):
                src = r + base + k
                if src not in loaded:
                    loaded[src] = a_ext[s, _interleaved(src), :]
                accs[i] = accs[i] + loaded[src] * wk
        for i, r in enumerate(starts):
            conv_buf[s, _interleaved(r), :] = accs[i]
        chain.append(accs[-1])

    def pool_group(gi, w):
        for t0 in range(0, tm, TIME_BLOCK):
            loaded = {}
            for m in range(TIME_STRIDE):
                for j in range(w):
                    src = t0 + POOL_HALO + m - j
                    if src not in loaded:
                        loaded[src] = u_ext[gi, _interleaved(src), :]
                tok = loaded[t0 + POOL_HALO + m]
                tot = functools.reduce(jnp.add, [loaded[t0 + POOL_HALO + m - j] for j in range(w)])
                if t0 < w - 1:
                    frame = t * tm + t0 + m + TIME_STRIDE * lax.broadcasted_iota(jnp.int32, (SUBLANES, LANES), 0)
                    mean = tot / jnp.minimum(frame + 1, w).astype(F32)
                elif w & (w - 1) == 0:
                    mean = tot * (1.0 / w)
                else:
                    mean = tot / float(w)
                dlt_buf[gi, _interleaved(t0 + m), :] = mean - tok

    vreg_starts = [t0 + m for t0 in range(0, tm, TIME_BLOCK) for m in range(TIME_STRIDE)]
    mixer_stages = [functools.partial(conv_group, s, vreg_starts[i:i + CONV_ACCS])
                    for s in range(n_conv) for i in range(0, len(vreg_starts), CONV_ACCS)]
    mixer_stages.append(pool_in)
    mixer_stages += [functools.partial(pool_group, gi, w) for gi, w in enumerate(POOL_WINDOWS)]

    def mixer_end():
        cs = [conv_buf[s] for s in range(n_conv)]
        mu = jnp.sum(functools.reduce(jnp.add, cs), axis=-1, keepdims=True) / d_conv
        devs = [c - mu for c in cs]
        var = jnp.sum(functools.reduce(jnp.add, [dv * dv for dv in devs]), axis=-1, keepdims=True) / d_conv
        inv = lax.rsqrt(var + EPS)
        a_out = []
        for s in range(n_conv):
            lanes = slice(s * LANES, (s + 1) * LANES)
            c = devs[s] * inv * ln_g_ref[:, lanes] + ln_b_ref[:, lanes]
            a_out.append((c * jax.nn.sigmoid(c)).astype(BF16))
        pooled = [jnp.dot(dlt_buf[gi].astype(BF16), pool_w_ref[gi], preferred_element_type=F32)
                  for gi in range(n_pool)]
        p_out = (jnp.concatenate(pooled, axis=-1) * pool_s_ref[...]).astype(BF16)
        cat = jnp.concatenate(a_out + [p_out], axis=-1)
        h_prev[...] = live["x"] + jnp.dot(cat, w_out_ref[...], preferred_element_type=F32)

    ffn_begin()
    ffn_stages[0]()
    mixer_begin()
    ffn_rest = ffn_stages[1:]
    per = -(-len(mixer_stages) // len(ffn_rest))
    for i, ffn_stage in enumerate(ffn_rest):
        for stage in mixer_stages[i * per:(i + 1) * per]:
            stage()
        ffn_stage()
    mixer_end()


def _layer0(x2d, g, w_in_all, dw_w, dw_b, ln_g, ln_b, pool_w, pool_s, w_out_all, gf, w1_all, w3_all, w2_all,
            *, seq, layer, mixer_index):
    n_tok, d = x2d.shape
    w_in, w_out, w1, w3, w2 = (jax.ShapeDtypeStruct(w.shape[1:], w.dtype)
                               for w in (w_in_all, w_out_all, w1_all, w3_all, w2_all))
    d_conv = dw_w.shape[1]
    d_pool = pool_s.shape[1]
    d_ff = w1.shape[1]
    tm = TOKEN_TILE
    n_tiles = n_tok // tm
    assert d_pool == len(POOL_WINDOWS) * LANES and d_conv % LANES == 0
    assert tm % TIME_BLOCK == 0 and TIME_STRIDE % CONV_ACCS == 0 and seq % tm == 0 and n_tok % seq == 0
    stages = [_stage_shape(w, LAYER0_STAGE_BYTES) for w in (w_in, w_out, w1)]
    assert stages[1] == _stage_shape(w2, LAYER0_STAGE_BYTES) and stages[2] == _stage_shape(w3, LAYER0_STAGE_BYTES)
    resident = (w_in.size + w_out.size + pool_w.size + 3 * d * d_ff) * 2 + 6 * tm * d * 4 \
        + tm * w_in.shape[1] * 4 + (2 * tm + CONV_HALO) * d_conv * 4 + (2 * tm + POOL_HALO) * d_pool * 4 \
        + 3 * tm * FFN_CHUNK * 4 + 3 * STAGE_SLOTS * LAYER0_STAGE_BYTES
    hbm = pl.BlockSpec(memory_space=pl.ANY)
    return pl.pallas_call(
        functools.partial(_layer0_kernel, tm=tm, tiles_per_seq=seq // tm, n_tiles=n_tiles,
                          layer=layer, mixer_index=mixer_index),
        grid=(n_tiles + 1,),
        in_specs=[
            pl.BlockSpec((tm, d), lambda s: (jnp.minimum(s, n_tiles - 1), 0)),
            _resident((1, d), 1),
            hbm,
            _resident(dw_w.shape, 1),
            _resident((1, d_conv), 1),
            _resident((1, d_conv), 1),
            _resident((1, d_conv), 1),
            _resident(pool_w.shape, 1),
            _resident((1, d_pool), 1),
            hbm,
            _resident((1, d), 1),
            hbm, hbm, hbm,
        ],
        out_specs=pl.BlockSpec((tm, d), lambda s: (jnp.maximum(s - 1, 0), 0)),
        out_shape=jax.ShapeDtypeStruct((n_tok, d), F32),
        scratch_shapes=[
            pltpu.VMEM((d_conv // LANES, CONV_HALO + tm, LANES), F32),
            pltpu.VMEM((d_pool // LANES, POOL_HALO + tm, LANES), F32),
            pltpu.VMEM((d_conv // LANES, tm, LANES), F32),
            pltpu.VMEM((d_pool // LANES, tm, LANES), F32),
            pltpu.VMEM((tm, d), F32),
            pltpu.VMEM((tm, d_ff), BF16),
            pltpu.VMEM(w_in.shape, BF16),
            pltpu.VMEM(w_out.shape, BF16),
            pltpu.VMEM(w1.shape, BF16),
            pltpu.VMEM(w3.shape, BF16),
            pltpu.VMEM(w2.shape, BF16),
            pltpu.VMEM(stages[0], F32),
            pltpu.VMEM(stages[1], F32),
            pltpu.VMEM(stages[2], F32),
            pltpu.SemaphoreType.DMA((STAGE_SLOTS,)),
            pltpu.SemaphoreType.DMA((STAGE_SLOTS,)),
            pltpu.SemaphoreType.DMA((STAGE_SLOTS,)),
        ],
        compiler_params=pltpu.CompilerParams(
            dimension_semantics=("arbitrary",),
            vmem_limit_bytes=_vmem_limit(resident)),
        name="layer0_convpool_ffn",
    )(x2d, g, w_in_all, dw_w, dw_b, ln_g, ln_b, pool_w, pool_s, w_out_all, gf, w1_all, w3_all, w2_all)


def _hgrn_kernel(h_ref, g_ref, w_in_hbm, lb_logits_ref, gn_ref, w_out_hbm, o_ref,
                 z_ref, y_ref, state_ref, w_in_ref, w_out_ref, stage_in, stage_out, sem_in, sem_out,
                 *, tm, layer, mixer_index):
    t = pl.program_id(1)
    d_hg = y_ref.shape[1]
    dk = d_hg // HG_HEADS

    @pl.when((pl.program_id(0) == 0) & (t == 0))
    def _():
        _fetch_weights_as_bf16([(w_in_hbm.at[mixer_index], w_in_ref, stage_in, sem_in),
                                (w_out_hbm.at[mixer_index], w_out_ref, stage_out, sem_out)])

    @pl.when(t == 0)
    def _():
        state_ref[...] = jnp.zeros(state_ref.shape, F32)

    x = h_ref[...]
    n = _rmsnorm(x, g_ref[...]).astype(BF16)
    z_ref[...] = jnp.dot(n, w_in_ref[...], preferred_element_type=F32)

    logits = lb_logits_ref[...]
    e = jnp.exp(logits - jnp.max(logits, axis=0, keepdims=True))
    p = e / jnp.sum(e, axis=0, keepdims=True)
    lb = jnp.sum(p[0:layer + 1], axis=0, keepdims=True) - p[0:1]

    row = lax.broadcasted_iota(jnp.int32, (CHUNK, CHUNK), 0)
    col = lax.broadcasted_iota(jnp.int32, (CHUNK, CHUNK), 1)
    causal = row >= col
    sublane = lax.broadcasted_iota(jnp.int32, (SUBLANES, LANES), 0)
    gn = gn_ref[...]

    def chunk_step(c, carry):
        r0 = pl.multiple_of(c * CHUNK, CHUNK)
        rows = pl.ds(r0, CHUNK)
        q = z_ref[rows, 0:d_hg]
        f = lb + (1.0 - lb) * jax.nn.sigmoid(z_ref[rows, d_hg:2 * d_hg])
        log_f = jnp.log(f)
        k = 1.0 - f
        v = z_ref[rows, 2 * d_hg:3 * d_hg].astype(BF16)
        gate = z_ref[rows, 3 * d_hg:4 * d_hg]
        gate = gate * jax.nn.sigmoid(gate)

        b_tiles = []
        for lt in range(d_hg // LANES):
            running = None
            column = []
            for rt in range(CHUNK // SUBLANES):
                x = log_f[rt * SUBLANES:(rt + 1) * SUBLANES, lt * LANES:(lt + 1) * LANES]
                for s in (1, 2, 4):
                    x = x + jnp.where(sublane >= s, pltpu.roll(x, s, axis=0), 0.0)
                if running is not None:
                    x = x + running
                running = jnp.broadcast_to(x[SUBLANES - 1:SUBLANES, :], (SUBLANES, LANES))
                column.append(x)
            b_tiles.append(jnp.concatenate(column, axis=0))
        b = jnp.concatenate(b_tiles, axis=1)

        b_mid = b[CHUNK // 2 - 1:CHUNK // 2, :]
        b_last = b[CHUNK - 1:CHUNK, :]
        q_rel = q * jnp.exp(b - b_mid)
        k_rel = k * jnp.exp(b_mid - b)
        q_in = (q_rel * jnp.exp(b_mid)).astype(BF16)
        k_up = (k_rel * jnp.exp(b_last - b_mid)).astype(BF16)
        q_rel = q_rel.astype(BF16)
        k_rel = k_rel.astype(BF16)
        decay_last = jnp.exp(b_last)

        heads = [slice(hd * dk, (hd + 1) * dk) for hd in range(HG_HEADS)]
        nt = (((1,), (1,)), ((), ()))
        tn = (((0,), (0,)), ((), ()))
        scores = [lax.dot_general(q_rel[:, sl], k_rel[:, sl], nt, preferred_element_type=F32)
                  for sl in heads]
        state_t = [state_ref[hd] for hd in range(HG_HEADS)]
        o_inter = [lax.dot_general(q_in[:, sl], st.astype(BF16), nt, preferred_element_type=F32)
                   for sl, st in zip(heads, state_t)]
        kv = [lax.dot_general(v[:, sl], k_up[:, sl], tn, preferred_element_type=F32) for sl in heads]
        for hd, sl in enumerate(heads):
            state_ref[hd] = state_t[hd] * decay_last[:, sl] + kv[hd]
        scores = [jnp.where(causal, s, 0.0).astype(BF16) for s in scores]
        outs = [jnp.dot(s, v[:, sl], preferred_element_type=F32) + oi
                for s, sl, oi in zip(scores, heads, o_inter)]
        for o, sl in zip(outs, heads):
            o = o * lax.rsqrt(jnp.mean(o * o, axis=-1, keepdims=True) + EPS) * gn[:, sl]
            y_ref[rows, sl] = (o * gate[:, sl]).astype(BF16)
        return carry

    lax.fori_loop(0, tm // CHUNK, chunk_step, 0, unroll=True)
    o_ref[...] = x + jnp.dot(y_ref[...], w_out_ref[...], preferred_element_type=F32)


def _hgrn_mixer(h, g, w_in_all, lb_logits, gn_g, w_out_all, *, layer, mixer_index):
    bsz, seq, d = h.shape
    w_in, w_out = (jax.ShapeDtypeStruct(w.shape[1:], w.dtype) for w in (w_in_all, w_out_all))
    d_hg = w_out.shape[0]
    dk = d_hg // HG_HEADS
    tm = TOKEN_TILE
    resident = (w_in.size + w_out.size) * 2 + 4 * tm * d * 4 + tm * 4 * d_hg * 4 * 2 \
        + tm * d_hg * 2 + HG_HEADS * dk * dk * 4 + 2 * STAGE_SLOTS * HGRN_STAGE_BYTES
    hbm = pl.BlockSpec(memory_space=pl.ANY)
    return pl.pallas_call(
        functools.partial(_hgrn_kernel, tm=tm, layer=layer, mixer_index=mixer_index),
        grid=(bsz, seq // tm),
        in_specs=[
            pl.BlockSpec((None, tm, d), lambda b, t: (b, t, 0)),
            _resident((1, d), 2),
            hbm,
            _resident(lb_logits.shape, 2),
            _resident((1, d_hg), 2),
            hbm,
        ],
        out_specs=pl.BlockSpec((None, tm, d), lambda b, t: (b, t, 0)),
        out_shape=jax.ShapeDtypeStruct((bsz, seq, d), F32),
        scratch_shapes=[
            pltpu.VMEM((tm, 4 * d_hg), F32),
            pltpu.VMEM((tm, d_hg), BF16),
            pltpu.VMEM((HG_HEADS, dk, dk), F32),
            pltpu.VMEM(w_in.shape, BF16),
            pltpu.VMEM(w_out.shape, BF16),
            pltpu.VMEM(_stage_shape(w_in, HGRN_STAGE_BYTES), F32),
            pltpu.VMEM(_stage_shape(w_out, HGRN_STAGE_BYTES), F32),
            pltpu.SemaphoreType.DMA((STAGE_SLOTS,)),
            pltpu.SemaphoreType.DMA((STAGE_SLOTS,)),
        ],
        compiler_params=pltpu.CompilerParams(
            dimension_semantics=("arbitrary", "arbitrary"),
            vmem_limit_bytes=_vmem_limit(resident)),
        name="hgrn_mixer",
    )(h, g, w_in_all, lb_logits, gn_g, w_out_all)


def kernel(x, norm_mix_g, norm_ffn_g, final_g, cp_w_in, cp_dw_w, cp_dw_b, cp_ln_g, cp_ln_b, cp_pool_w,
           cp_pool_scale, cp_w_out, hg_w_in, hg_lb_logits, hg_gn_g, hg_w_out, ffn_w1, ffn_w3, ffn_w2):
    bsz, seq, d = x.shape
    depth = norm_mix_g.shape[0]
    row = lambda a: a.reshape(1, -1).astype(F32)
    h = x
    for layer in range(depth):
        j = layer // 2
        last = layer == depth - 1
        if layer % 2 == 0:
            assert not last, "the fused conv/pool + FFN layer kernel has no final norm"
            h = _layer0(
                h.reshape(bsz * seq, d), row(norm_mix_g[layer]), cp_w_in, cp_dw_w[j],
                row(cp_dw_b[j]), row(cp_ln_g[j]), row(cp_ln_b[j]), cp_pool_w[j].astype(BF16),
                row(cp_pool_scale[j]), cp_w_out, row(norm_ffn_g[layer]), ffn_w1, ffn_w3, ffn_w2,
                seq=seq, layer=layer, mixer_index=j).reshape(bsz, seq, d)
        else:
            h = _hgrn_mixer(
                h, row(norm_mix_g[layer]), hg_w_in, hg_lb_logits.astype(F32),
                row(hg_gn_g[j]), hg_w_out, layer=layer, mixer_index=j)
            h = _ffn_block(
                h.reshape(bsz * seq, d), row(norm_ffn_g[layer]), ffn_w1, ffn_w3, ffn_w2, row(final_g),
                layer=layer, final_norm=last).reshape(bsz, seq, d)
    return h
```

```python
import functools
import itertools

import jax
import jax.numpy as jnp
from jax import lax
from jax.experimental import pallas as pl
from jax.experimental.pallas import tpu as pltpu

F32 = jnp.float32
BF16 = jnp.bfloat16

EPS = 1e-6
CHUNK = 64
CONV_WIDTH = 31
POOL_WINDOWS = (2, 4, 8, 16)
HG_HEADS = 8

V7X_VMEM_BYTES = 64 * 1024 * 1024
SUBLANES = 8
LANES = 128

TOKEN_TILE = 512
CONV_HALO = 32
POOL_HALO = 16
TIME_STRIDE = 4
TIME_BLOCK = SUBLANES * TIME_STRIDE
CONV_ACCS = 2
FFN_CHUNK = 1024
STAGE_SLOTS = 3
FFN_STAGE_BYTES = 2 << 20
HGRN_STAGE_BYTES = 2 << 20
LAYER0_STAGE_BYTES = 1 << 20


def _interleaved(start):
    return pl.ds(start, SUBLANES, stride=TIME_STRIDE)


def _vmem_limit(resident_bytes):
    return int(min(resident_bytes * 1.5 + (8 << 20), V7X_VMEM_BYTES - (6 << 20)))


def _rmsnorm(x, g):
    return x * lax.rsqrt(jnp.mean(x * x, axis=-1, keepdims=True) + EPS) * g


def _resident(shape, ngrid):
    zeros = (0,) * len(shape)
    if ngrid == 1:
        index_map = lambda i: zeros
    else:
        index_map = lambda b, t: zeros
    return pl.BlockSpec(shape, index_map, pipeline_mode=pl.Buffered(1))


def _stage_shape(w, stage_bytes):
    n_rows, n_cols = w.shape
    pack = 2 * SUBLANES
    rows = max(pack, (stage_bytes // (4 * n_cols)) // pack * pack)
    while n_rows % rows:
        rows -= pack
    return (STAGE_SLOTS, rows, n_cols)


def _fetch_weights_as_bf16(jobs):
    queues = {}
    for src, dst, stage, sem in jobs:
        queue = queues.setdefault(id(stage), [])
        rows = stage.shape[1]
        for i in range(dst.shape[0] // rows):
            queue.append((src, dst, stage, sem, i * rows, rows, len(queue) % stage.shape[0]))
    chunks = [c for group in itertools.zip_longest(*queues.values()) for c in group if c is not None]

    def copy(c):
        src, _, stage, sem, r0, rows, slot = c
        return pltpu.make_async_copy(src.at[pl.ds(r0, rows)], stage.at[slot], sem.at[slot])

    starts_after = [[] for _ in chunks]
    last_in_slot = {}
    for k, c in enumerate(chunks):
        key = (id(c[2]), c[6])
        if key in last_in_slot:
            starts_after[last_in_slot[key]].append(k)
        else:
            copy(c).start()
        last_in_slot[key] = k
    for k, c in enumerate(chunks):
        copy(c).wait()
        _, dst, stage, _, r0, rows, slot = c
        dst[pl.ds(r0, rows), :] = stage[slot].astype(BF16)
        for nxt in starts_after[k]:
            copy(chunks[nxt]).start()


def _ffn_kernel(h_ref, g_ref, w1_hbm, w3_hbm, w2_hbm, fg_ref, o_ref,
                w1_ref, w3_ref, w2_ref, stage_in, stage_out, sem_in, sem_out, *, layer, final_norm):
    @pl.when(pl.program_id(0) == 0)
    def _():
        _fetch_weights_as_bf16([(w1_hbm.at[layer], w1_ref, stage_in, sem_in),
                                (w3_hbm.at[layer], w3_ref, stage_in, sem_in),
                                (w2_hbm.at[layer], w2_ref, stage_out, sem_out)])

    half = h_ref.shape[0] // 2
    rows = [slice(0, half), slice(half, 2 * half)]
    h = [h_ref[r, :] for r in rows]
    n = [_rmsnorm(hh, g_ref[...]).astype(BF16) for hh in h]
    ab = [(jnp.dot(nn, w1_ref[...], preferred_element_type=F32),
           jnp.dot(nn, w3_ref[...], preferred_element_type=F32)) for nn in n]
    for r, hh, (a, b) in zip(rows, h, ab):
        gated = (a * jax.nn.sigmoid(a) * b).astype(BF16)
        out = hh + jnp.dot(gated, w2_ref[...], preferred_element_type=F32)
        if final_norm:
            out = _rmsnorm(out, fg_ref[...])
        o_ref[r, :] = out


def _ffn_block(h2d, g, w1_all, w3_all, w2_all, final_g, *, layer, final_norm):
    n_tok, d = h2d.shape
    w1, w3, w2 = (jax.ShapeDtypeStruct(w.shape[1:], w.dtype) for w in (w1_all, w3_all, w2_all))
    d_ff = w1.shape[1]
    tm = TOKEN_TILE
    stages = [_stage_shape(w1, FFN_STAGE_BYTES), _stage_shape(w2, FFN_STAGE_BYTES)]
    assert stages[0] == _stage_shape(w3, FFN_STAGE_BYTES)
    resident = 3 * d * d_ff * 2 + 4 * tm * d * 4 + 3 * tm * d_ff * 4 + 2 * STAGE_SLOTS * FFN_STAGE_BYTES
    hbm = pl.BlockSpec(memory_space=pl.ANY)
    return pl.pallas_call(
        functools.partial(_ffn_kernel, layer=layer, final_norm=final_norm),
        grid=(n_tok // tm,),
        in_specs=[
            pl.BlockSpec((tm, d), lambda i: (i, 0)),
            _resident((1, d), 1),
            hbm, hbm, hbm,
            _resident((1, d), 1),
        ],
        out_specs=pl.BlockSpec((tm, d), lambda i: (i, 0)),
        out_shape=jax.ShapeDtypeStruct((n_tok, d), F32),
        scratch_shapes=[
            pltpu.VMEM(w1.shape, BF16),
            pltpu.VMEM(w3.shape, BF16),
            pltpu.VMEM(w2.shape, BF16),
            pltpu.VMEM(stages[0], F32),
            pltpu.VMEM(stages[1], F32),
            pltpu.SemaphoreType.DMA((STAGE_SLOTS,)),
            pltpu.SemaphoreType.DMA((STAGE_SLOTS,)),
        ],
        compiler_params=pltpu.CompilerParams(
            dimension_semantics=("arbitrary",),
            vmem_limit_bytes=_vmem_limit(resident)),
        name="ffn_block",
    )(h2d, g, w1_all, w3_all, w2_all, final_g)


def _layer0_kernel(x_ref, g_ref, w_in_hbm, dw_w_ref, dw_b_ref, ln_g_ref, ln_b_ref, pool_w_ref, pool_s_ref,
                   w_out_hbm, gf_ref, w1_hbm, w3_hbm, w2_hbm, o_ref,
                   a_ext, u_ext, conv_buf, dlt_buf, h_prev, gated_ref, w_in_ref, w_out_ref, w1_ref, w3_ref, w2_ref,
                   stage_in, stage_d, stage_ff, sem_in, sem_d, sem_ff,
                   *, tm, tiles_per_seq, n_tiles, layer, mixer_index):
    step = pl.program_id(0)
    t = lax.rem(jnp.minimum(step, n_tiles - 1), tiles_per_seq)
    n_conv = a_ext.shape[0]
    n_pool = u_ext.shape[0]
    d_conv = n_conv * LANES
    d_ff = w1_ref.shape[1]

    @pl.when(step == 0)
    def _():
        h_prev[...] = jnp.zeros(h_prev.shape, F32)
        a_ext[:, tm:tm + CONV_HALO, :] = jnp.zeros((n_conv, CONV_HALO, LANES), F32)
        u_ext[:, tm:tm + POOL_HALO, :] = jnp.zeros((n_pool, POOL_HALO, LANES), F32)
        _fetch_weights_as_bf16([(w_in_hbm.at[mixer_index], w_in_ref, stage_in, sem_in),
                                (w_out_hbm.at[mixer_index], w_out_ref, stage_d, sem_d),
                                (w2_hbm.at[layer], w2_ref, stage_d, sem_d),
                                (w1_hbm.at[layer], w1_ref, stage_ff, sem_ff),
                                (w3_hbm.at[layer], w3_ref, stage_ff, sem_ff)])

    live = {}

    def ffn_begin():
        live["hp"] = h_prev[...]
        live["nf"] = _rmsnorm(live["hp"], gf_ref[...]).astype(BF16)

    ff_chunks = [(c0, min(c0 + FFN_CHUNK, d_ff)) for c0 in range(0, d_ff, FFN_CHUNK)]
    gate_in, up_in = {}, {}

    def ffn_gate(j):
        c0, c1 = ff_chunks[j]
        gate_in[j] = jnp.dot(live["nf"], w1_ref[:, c0:c1], preferred_element_type=F32)

    def ffn_up(j):
        c0, c1 = ff_chunks[j]
        up_in[j] = jnp.dot(live["nf"], w3_ref[:, c0:c1], preferred_element_type=F32)

    def ffn_act(j):
        c0, c1 = ff_chunks[j]
        a, b = gate_in.pop(j), up_in.pop(j)
        gated_ref[:, c0:c1] = (a * jax.nn.sigmoid(a) * b).astype(BF16)

    def ffn_out():
        o_ref[...] = live["hp"] + jnp.dot(gated_ref[...], w2_ref[...], preferred_element_type=F32)

    ffn_stages = []
    for j in range(len(ff_chunks)):
        ffn_stages += [functools.partial(ffn_gate, j), functools.partial(ffn_up, j), functools.partial(ffn_act, j)]
    ffn_stages.append(ffn_out)

    def mixer_begin():
        x = x_ref[...]
        n = _rmsnorm(x, g_ref[...]).astype(BF16)
        z = jnp.dot(n, w_in_ref[:, :2 * d_conv], preferred_element_type=F32)
        live["x"], live["n"] = x, n

        inside = t > 0
        a_ext[:, 0:CONV_HALO, :] = jnp.where(inside, a_ext[:, tm:tm + CONV_HALO, :], 0.0)
        u_ext[:, 0:POOL_HALO, :] = jnp.where(inside, u_ext[:, tm:tm + POOL_HALO, :], 0.0)

        glu = z[:, :d_conv] * jax.nn.sigmoid(z[:, d_conv:2 * d_conv])
        for s in range(n_conv):
            a_ext[s, CONV_HALO:CONV_HALO + tm, :] = glu[:, s * LANES:(s + 1) * LANES]

    def pool_in():
        u = jnp.dot(live["n"], w_in_ref[:, 2 * d_conv:], preferred_element_type=F32)
        for s in range(n_pool):
            u_ext[s, POOL_HALO:POOL_HALO + tm, :] = u[:, s * LANES:(s + 1) * LANES]

    base = CONV_HALO - (CONV_WIDTH - 1)

    always = step >= 0
    chain = []

    def conv_group(s, starts):
        lanes = slice(s * LANES, (s + 1) * LANES)
        bias = jnp.broadcast_to(dw_b_ref[:, lanes], (SUBLANES, LANES))
        if chain:
            bias = jnp.where(always, bias, chain[-1])
        accs = [bias] * len(starts)
        loaded = {}
        for k in range(CONV_WIDTH):
            wk = jnp.broadcast_to(dw_w_ref[k:k + 1, lanes], (SUBLANES, LANES))
            for i, r in enumerate(starts):
                src = r + base + k
                if src not in loaded:
                    loaded[src] = a_ext[s, _interleaved(src), :]
                accs[i] = accs[i] + loaded[src] * wk
        for i, r in enumerate(starts):
            conv_buf[s, _interleaved(r), :] = accs[i]
        chain.append(accs[-1])

    def pool_group(gi, w):
        for t0 in range(0, tm, TIME_BLOCK):
            loaded = {}
            for m in range(TIME_STRIDE):
                for j in range(w):
                    src = t0 + POOL_HALO + m - j
                    if src not in loaded:
                        loaded[src] = u_ext[gi, _interleaved(src), :]
                tok = loaded[t0 + POOL_HALO + m]
                tot = functools.reduce(jnp.add, [loaded[t0 + POOL_HALO + m - j] for j in range(w)])
                if t0 < w - 1:
                    frame = t * tm + t0 + m + TIME_STRIDE * lax.broadcasted_iota(jnp.int32, (SUBLANES, LANES), 0)
                    mean = tot / jnp.minimum(frame + 1, w).astype(F32)
                elif w & (w - 1) == 0:
                    mean = tot * (1.0 / w)
                else:
                    mean = tot / float(w)
                dlt_buf[gi, _interleaved(t0 + m), :] = mean - tok

    vreg_starts = [t0 + m for t0 in range(0, tm, TIME_BLOCK) for m in range(TIME_STRIDE)]
    mixer_stages = [functools.partial(conv_group, s, vreg_starts[i:i + CONV_ACCS])
                    for s in range(n_conv) for i in range(0, len(vreg_starts), CONV_ACCS)]
    mixer_stages.append(pool_in)
    mixer_stages += [functools.partial(pool_group, gi, w) for gi, w in enumerate(POOL_WINDOWS)]

    def mixer_end():
        cs = [conv_buf[s] for s in range(n_conv)]
        mu = jnp.sum(functools.reduce(jnp.add, cs), axis=-1, keepdims=True) / d_conv
        devs = [c - mu for c in cs]
        var = jnp.sum(functools.reduce(jnp.add, [dv * dv for dv in devs]), axis=-1, keepdims=True) / d_conv
        inv = lax.rsqrt(var + EPS)
        a_out = []
        for s in range(n_conv):
            lanes = slice(s * LANES, (s + 1) * LANES)
            c = devs[s] * inv * ln_g_ref[:, lanes] + ln_b_ref[:, lanes]
            a_out.append((c * jax.nn.sigmoid(c)).astype(BF16))
        pooled = [jnp.dot(dlt_buf[gi].astype(BF16), pool_w_ref[gi], preferred_element_type=F32)
                  for gi in range(n_pool)]
        p_out = (jnp.concatenate(pooled, axis=-1) * pool_s_ref[...]).astype(BF16)
        cat = jnp.concatenate(a_out + [p_out], axis=-1)
        h_prev[...] = live["x"] + jnp.dot(cat, w_out_ref[...], preferred_element_type=F32)

    ffn_begin()
    ffn_stages[0]()
    mixer_begin()
    ffn_rest = ffn_stages[1:]
    per = -(-len(mixer_stages) // len(ffn_rest))
    for i, ffn_stage in enumerate(ffn_rest):
        for stage in mixer_stages[i * per:(i + 1) * per]:
            stage()
        ffn_stage()
    mixer_end()


def _layer0(x2d, g, w_in_all, dw_w, dw_b, ln_g, ln_b, pool_w, pool_s, w_out_all, gf, w1_all, w3_all, w2_all,
            *, seq, layer, mixer_index):
    n_tok, d = x2d.shape
    w_in, w_out, w1, w3, w2 = (jax.ShapeDtypeStruct(w.shape[1:], w.dtype)
                               for w in (w_in_all, w_out_all, w1_all, w3_all, w2_all))
    d_conv = dw_w.shape[1]
    d_pool = pool_s.shape[1]
    d_ff = w1.shape[1]
    tm = TOKEN_TILE
    n_tiles = n_tok // tm
    assert d_pool == len(POOL_WINDOWS) * LANES and d_conv % LANES == 0
    assert tm % TIME_BLOCK == 0 and TIME_STRIDE % CONV_ACCS == 0 and seq % tm == 0 and n_tok % seq == 0
    stages = [_stage_shape(w, LAYER0_STAGE_BYTES) for w in (w_in, w_out, w1)]
    assert stages[1] == _stage_shape(w2, LAYER0_STAGE_BYTES) and stages[2] == _stage_shape(w3, LAYER0_STAGE_BYTES)
    resident = (w_in.size + w_out.size + pool_w.size + 3 * d * d_ff) * 2 + 6 * tm * d * 4 \
        + tm * w_in.shape[1] * 4 + (2 * tm + CONV_HALO) * d_conv * 4 + (2 * tm + POOL_HALO) * d_pool * 4 \
        + 3 * tm * FFN_CHUNK * 4 + 3 * STAGE_SLOTS * LAYER0_STAGE_BYTES
    hbm = pl.BlockSpec(memory_space=pl.ANY)
    return pl.pallas_call(
        functools.partial(_layer0_kernel, tm=tm, tiles_per_seq=seq // tm, n_tiles=n_tiles,
                          layer=layer, mixer_index=mixer_index),
        grid=(n_tiles + 1,),
        in_specs=[
            pl.BlockSpec((tm, d), lambda s: (jnp.minimum(s, n_tiles - 1), 0)),
            _resident((1, d), 1),
            hbm,
            _resident(dw_w.shape, 1),
            _resident((1, d_conv), 1),
            _resident((1, d_conv), 1),
            _resident((1, d_conv), 1),
            _resident(pool_w.shape, 1),
            _resident((1, d_pool), 1),
            hbm,
            _resident((1, d), 1),
            hbm, hbm, hbm,
        ],
        out_specs=pl.BlockSpec((tm, d), lambda s: (jnp.maximum(s - 1, 0), 0)),
        out_shape=jax.ShapeDtypeStruct((n_tok, d), F32),
        scratch_shapes=[
            pltpu.VMEM((d_conv // LANES, CONV_HALO + tm, LANES), F32),
            pltpu.VMEM((d_pool // LANES, POOL_HALO + tm, LANES), F32),
            pltpu.VMEM((d_conv // LANES, tm, LANES), F32),
            pltpu.VMEM((d_pool // LANES, tm, LANES), F32),
            pltpu.VMEM((tm, d), F32),
            pltpu.VMEM((tm, d_ff), BF16),
            pltpu.VMEM(w_in.shape, BF16),
            pltpu.VMEM(w_out.shape, BF16),
            pltpu.VMEM(w1.shape, BF16),
            pltpu.VMEM(w3.shape, BF16),
            pltpu.VMEM(w2.shape, BF16),
            pltpu.VMEM(stages[0], F32),
            pltpu.VMEM(stages[1], F32),
            pltpu.VMEM(stages[2], F32),
            pltpu.SemaphoreType.DMA((STAGE_SLOTS,)),
            pltpu.SemaphoreType.DMA((STAGE_SLOTS,)),
            pltpu.SemaphoreType.DMA((STAGE_SLOTS,)),
        ],
        compiler_params=pltpu.CompilerParams(
            dimension_semantics=("arbitrary",),
            vmem_limit_bytes=_vmem_limit(resident)),
        name="layer0_convpool_ffn",
    )(x2d, g, w_in_all, dw_w, dw_b, ln_g, ln_b, pool_w, pool_s, w_out_all, gf, w1_all, w3_all, w2_all)


def _hgrn_kernel(h_ref, g_ref, w_in_hbm, lb_logits_ref, gn_ref, w_out_hbm, o_ref,
                 z_ref, y_ref, state_ref, w_in_ref, w_out_ref, stage_in, stage_out, sem_in, sem_out,
                 *, tm, layer, mixer_index):
    t = pl.program_id(1)
    d_hg = y_ref.shape[1]
    dk = d_hg // HG_HEADS

    @pl.when((pl.program_id(0) == 0) & (t == 0))
    def _():
        _fetch_weights_as_bf16([(w_in_hbm.at[mixer_index], w_in_ref, stage_in, sem_in),
                                (w_out_hbm.at[mixer_index], w_out_ref, stage_out, sem_out)])

    @pl.when(t == 0)
    def _():
        state_ref[...] = jnp.zeros(state_ref.shape, F32)

    x = h_ref[...]
    n = _rmsnorm(x, g_ref[...]).astype(BF16)
    z_ref[...] = jnp.dot(n, w_in_ref[...], preferred_element_type=F32)

    logits = lb_logits_ref[...]
    e = jnp.exp(logits - jnp.max(logits, axis=0, keepdims=True))
    p = e / jnp.sum(e, axis=0, keepdims=True)
    lb = jnp.sum(p[0:layer + 1], axis=0, keepdims=True) - p[0:1]

    row = lax.broadcasted_iota(jnp.int32, (CHUNK, CHUNK), 0)
    col = lax.broadcasted_iota(jnp.int32, (CHUNK, CHUNK), 1)
    causal = row >= col
    sublane = lax.broadcasted_iota(jnp.int32, (SUBLANES, LANES), 0)
    gn = gn_ref[...]

    def chunk_step(c, carry):
        r0 = pl.multiple_of(c * CHUNK, CHUNK)
        rows = pl.ds(r0, CHUNK)
        q = z_ref[rows, 0:d_hg]
        f = lb + (1.0 - lb) * jax.nn.sigmoid(z_ref[rows, d_hg:2 * d_hg])
        log_f = jnp.log(f)
        k = 1.0 - f
        v = z_ref[rows, 2 * d_hg:3 * d_hg].astype(BF16)
        gate = z_ref[rows, 3 * d_hg:4 * d_hg]
        gate = gate * jax.nn.sigmoid(gate)

        b_tiles = []
        for lt in range(d_hg // LANES):
            running = None
            column = []
            for rt in range(CHUNK // SUBLANES):
                x = log_f[rt * SUBLANES:(rt + 1) * SUBLANES, lt * LANES:(lt + 1) * LANES]
                for s in (1, 2, 4):
                    x = x + jnp.where(sublane >= s, pltpu.roll(x, s, axis=0), 0.0)
                if running is not None:
                    x = x + running
                running = jnp.broadcast_to(x[SUBLANES - 1:SUBLANES, :], (SUBLANES, LANES))
                column.append(x)
            b_tiles.append(jnp.concatenate(column, axis=0))
        b = jnp.concatenate(b_tiles, axis=1)

        b_mid = b[CHUNK // 2 - 1:CHUNK // 2, :]
        b_last = b[CHUNK - 1:CHUNK, :]
        q_rel = q * jnp.exp(b - b_mid)
        k_rel = k * jnp.exp(b_mid - b)
        q_in = (q_rel * jnp.exp(b_mid)).astype(BF16)
        k_up = (k_rel * jnp.exp(b_last - b_mid)).astype(BF16)
        q_rel = q_rel.astype(BF16)
        k_rel = k_rel.astype(BF16)
        decay_last = jnp.exp(b_last)

        heads = [slice(hd * dk, (hd + 1) * dk) for hd in range(HG_HEADS)]
        nt = (((1,), (1,)), ((), ()))
        tn = (((0,), (0,)), ((), ()))
        scores = [lax.dot_general(q_rel[:, sl], k_rel[:, sl], nt, preferred_element_type=F32)
                  for sl in heads]
        state_t = [state_ref[hd] for hd in range(HG_HEADS)]
        o_inter = [lax.dot_general(q_in[:, sl], st.astype(BF16), nt, preferred_element_type=F32)
                   for sl, st in zip(heads, state_t)]
        kv = [lax.dot_general(v[:, sl], k_up[:, sl], tn, preferred_element_type=F32) for sl in heads]
        for hd, sl in enumerate(heads):
            state_ref[hd] = state_t[hd] * decay_last[:, sl] + kv[hd]
        scores = [jnp.where(causal, s, 0.0).astype(BF16) for s in scores]
        outs = [jnp.dot(s, v[:, sl], preferred_element_type=F32) + oi
                for s, sl, oi in zip(scores, heads, o_inter)]
        for o, sl in zip(outs, heads):
            o = o * lax.rsqrt(jnp.mean(o * o, axis=-1, keepdims=True) + EPS) * gn[:, sl]
            y_ref[rows, sl] = (o * gate[:, sl]).astype(BF16)
        return carry

    lax.fori_loop(0, tm // CHUNK, chunk_step, 0, unroll=True)
    o_ref[...] = x + jnp.dot(y_ref[...], w_out_ref[...], preferred_element_type=F32)


def _hgrn_mixer(h, g, w_in_all, lb_logits, gn_g, w_out_all, *, layer, mixer_index):
    bsz, seq, d = h.shape
    w_in, w_out = (jax.ShapeDtypeStruct(w.shape[1:], w.dtype) for w in (w_in_all, w_out_all))
    d_hg = w_out.shape[0]
    dk = d_hg // HG_HEADS
    tm = TOKEN_TILE
    resident = (w_in.size + w_out.size) * 2 + 4 * tm * d * 4 + tm * 4 * d_hg * 4 * 2 \
        + tm * d_hg * 2 + HG_HEADS * dk * dk * 4 + 2 * STAGE_SLOTS * HGRN_STAGE_BYTES
    hbm = pl.BlockSpec(memory_space=pl.ANY)
    return pl.pallas_call(
        functools.partial(_hgrn_kernel, tm=tm, layer=layer, mixer_index=mixer_index),
        grid=(bsz, seq // tm),
        in_specs=[
            pl.BlockSpec((None, tm, d), lambda b, t: (b, t, 0)),
            _resident((1, d), 2),
            hbm,
            _resident(lb_logits.shape, 2),
            _resident((1, d_hg), 2),
            hbm,
        ],
        out_specs=pl.BlockSpec((None, tm, d), lambda b, t: (b, t, 0)),
        out_shape=jax.ShapeDtypeStruct((bsz, seq, d), F32),
        scratch_shapes=[
            pltpu.VMEM((tm, 4 * d_hg), F32),
            pltpu.VMEM((tm, d_hg), BF16),
            pltpu.VMEM((HG_HEADS, dk, dk), F32),
            pltpu.VMEM(w_in.shape, BF16),
            pltpu.VMEM(w_out.shape, BF16),
            pltpu.VMEM(_stage_shape(w_in, HGRN_STAGE_BYTES), F32),
            pltpu.VMEM(_stage_shape(w_out, HGRN_STAGE_BYTES), F32),
            pltpu.SemaphoreType.DMA((STAGE_SLOTS,)),
            pltpu.SemaphoreType.DMA((STAGE_SLOTS,)),
        ],
        compiler_params=pltpu.CompilerParams(
            dimension_semantics=("arbitrary", "arbitrary"),
            vmem_limit_bytes=_vmem_limit(resident)),
        name="hgrn_mixer",
    )(h, g, w_in_all, lb_logits, gn_g, w_out_all)


def kernel(x, norm_mix_g, norm_ffn_g, final_g, cp_w_in, cp_dw_w, cp_dw_b, cp_ln_g, cp_ln_b, cp_pool_w,
           cp_pool_scale, cp_w_out, hg_w_in, hg_lb_logits, hg_gn_g, hg_w_out, ffn_w1, ffn_w3, ffn_w2):
    bsz, seq, d = x.shape
    depth = norm_mix_g.shape[0]
    row = lambda a: a.reshape(1, -1).astype(F32)
    h = x
    for layer in range(depth):
        j = layer // 2
        last = layer == depth - 1
        if layer % 2 == 0:
            assert not last, "the fused conv/pool + FFN layer kernel has no final norm"
            h = _layer0(
                h.reshape(bsz * seq, d), row(norm_mix_g[layer]), cp_w_in, cp_dw_w[j],
                row(cp_dw_b[j]), row(cp_ln_g[j]), row(cp_ln_b[j]), cp_pool_w[j].astype(BF16),
                row(cp_pool_scale[j]), cp_w_out, row(norm_ffn_g[layer]), ffn_w1, ffn_w3, ffn_w2,
                seq=seq, layer=layer, mixer_index=j).reshape(bsz, seq, d)
        else:
            h = _hgrn_mixer(
                h, row(norm_mix_g[layer]), hg_w_in, hg_lb_logits.astype(F32),
                row(hg_gn_g[j]), hg_w_out, layer=layer, mixer_index=j)
            h = _ffn_block(
                h.reshape(bsz * seq, d), row(norm_ffn_g[layer]), ffn_w1, ffn_w3, ffn_w2, row(final_g),
                layer=layer, final_norm=last).reshape(bsz, seq, d)
    return h
```

```python
import functools
import itertools

import jax
import jax.numpy as jnp
from jax import lax
from jax.experimental import pallas as pl
from jax.experimental.pallas import tpu as pltpu

F32 = jnp.float32
BF16 = jnp.bfloat16

EPS = 1e-6
CHUNK = 64
CONV_WIDTH = 31
POOL_WINDOWS = (2, 4, 8, 16)
HG_HEADS = 8

V7X_VMEM_BYTES = 64 * 1024 * 1024
SUBLANES = 8
LANES = 128

TOKEN_TILE = 512
CONV_HALO = 32
POOL_HALO = 16
TIME_STRIDE = 4
TIME_BLOCK = SUBLANES * TIME_STRIDE
CONV_ACCS = 2
FFN_CHUNK = 1024
STAGE_SLOTS = 4
FFN_STAGE_BYTES = 2 << 20
HGRN_STAGE_BYTES = 2 << 20
LAYER0_STAGE_BYTES = 1 << 20


def _interleaved(start):
    return pl.ds(start, SUBLANES, stride=TIME_STRIDE)


def _vmem_limit(resident_bytes):
    return int(min(resident_bytes * 1.5 + (8 << 20), V7X_VMEM_BYTES - (6 << 20)))


def _rmsnorm(x, g):
    return x * lax.rsqrt(jnp.mean(x * x, axis=-1, keepdims=True) + EPS) * g


def _resident(shape, ngrid):
    zeros = (0,) * len(shape)
    if ngrid == 1:
        index_map = lambda i: zeros
    else:
        index_map = lambda b, t: zeros
    return pl.BlockSpec(shape, index_map, pipeline_mode=pl.Buffered(1))


def _stage_shape(w, stage_bytes):
    n_rows, n_cols = w.shape
    pack = 2 * SUBLANES
    rows = max(pack, (stage_bytes // (4 * n_cols)) // pack * pack)
    while n_rows % rows:
        rows -= pack
    return (STAGE_SLOTS, rows, n_cols)


def _fetch_weights_as_bf16(jobs):
    queues = {}
    for src, dst, stage, sem in jobs:
        queue = queues.setdefault(id(stage), [])
        rows = stage.shape[1]
        for i in range(dst.shape[0] // rows):
            queue.append((src, dst, stage, sem, i * rows, rows, len(queue) % stage.shape[0]))
    chunks = [c for group in itertools.zip_longest(*queues.values()) for c in group if c is not None]

    def copy(c):
        src, _, stage, sem, r0, rows, slot = c
        return pltpu.make_async_copy(src.at[pl.ds(r0, rows)], stage.at[slot], sem.at[slot])

    starts_after = [[] for _ in chunks]
    last_in_slot = {}
    for k, c in enumerate(chunks):
        key = (id(c[2]), c[6])
        if key in last_in_slot:
            starts_after[last_in_slot[key]].append(k)
        else:
            copy(c).start()
        last_in_slot[key] = k
    for k, c in enumerate(chunks):
        copy(c).wait()
        _, dst, stage, _, r0, rows, slot = c
        dst[pl.ds(r0, rows), :] = stage[slot].astype(BF16)
        for nxt in starts_after[k]:
            copy(chunks[nxt]).start()


def _ffn_kernel(h_ref, g_ref, w1_hbm, w3_hbm, w2_hbm, fg_ref, o_ref,
                w1_ref, w3_ref, w2_ref, stage_in, stage_out, sem_in, sem_out, *, layer, final_norm):
    @pl.when(pl.program_id(0) == 0)
    def _():
        _fetch_weights_as_bf16([(w1_hbm.at[layer], w1_ref, stage_in, sem_in),
                                (w3_hbm.at[layer], w3_ref, stage_in, sem_in),
                                (w2_hbm.at[layer], w2_ref, stage_out, sem_out)])

    half = h_ref.shape[0] // 2
    rows = [slice(0, half), slice(half, 2 * half)]
    h = [h_ref[r, :] for r in rows]
    n = [_rmsnorm(hh, g_ref[...]).astype(BF16) for hh in h]
    ab = [(jnp.dot(nn, w1_ref[...], preferred_element_type=F32),
           jnp.dot(nn, w3_ref[...], preferred_element_type=F32)) for nn in n]
    for r, hh, (a, b) in zip(rows, h, ab):
        gated = (a * jax.nn.sigmoid(a) * b).astype(BF16)
        out = hh + jnp.dot(gated, w2_ref[...], preferred_element_type=F32)
        if final_norm:
            out = _rmsnorm(out, fg_ref[...])
        o_ref[r, :] = out


def _ffn_block(h2d, g, w1_all, w3_all, w2_all, final_g, *, layer, final_norm):
    n_tok, d = h2d.shape
    w1, w3, w2 = (jax.ShapeDtypeStruct(w.shape[1:], w.dtype) for w in (w1_all, w3_all, w2_all))
    d_ff = w1.shape[1]
    tm = TOKEN_TILE
    stages = [_stage_shape(w1, FFN_STAGE_BYTES), _stage_shape(w2, FFN_STAGE_BYTES)]
    assert stages[0] == _stage_shape(w3, FFN_STAGE_BYTES)
    resident = 3 * d * d_ff * 2 + 4 * tm * d * 4 + 3 * tm * d_ff * 4 + 2 * STAGE_SLOTS * FFN_STAGE_BYTES
    hbm = pl.BlockSpec(memory_space=pl.ANY)
    return pl.pallas_call(
        functools.partial(_ffn_kernel, layer=layer, final_norm=final_norm),
        grid=(n_tok // tm,),
        in_specs=[
            pl.BlockSpec((tm, d), lambda i: (i, 0)),
            _resident((1, d), 1),
            hbm, hbm, hbm,
            _resident((1, d), 1),
        ],
        out_specs=pl.BlockSpec((tm, d), lambda i: (i, 0)),
        out_shape=jax.ShapeDtypeStruct((n_tok, d), F32),
        scratch_shapes=[
            pltpu.VMEM(w1.shape, BF16),
            pltpu.VMEM(w3.shape, BF16),
            pltpu.VMEM(w2.shape, BF16),
            pltpu.VMEM(stages[0], F32),
            pltpu.VMEM(stages[1], F32),
            pltpu.SemaphoreType.DMA((STAGE_SLOTS,)),
            pltpu.SemaphoreType.DMA((STAGE_SLOTS,)),
        ],
        compiler_params=pltpu.CompilerParams(
            dimension_semantics=("arbitrary",),
            vmem_limit_bytes=_vmem_limit(resident)),
        name="ffn_block",
    )(h2d, g, w1_all, w3_all, w2_all, final_g)


def _layer0_kernel(x_ref, g_ref, w_in_hbm, dw_w_ref, dw_b_ref, ln_g_ref, ln_b_ref, pool_w_ref, pool_s_ref,
                   w_out_hbm, gf_ref, w1_hbm, w3_hbm, w2_hbm, o_ref,
                   a_ext, u_ext, conv_buf, dlt_buf, h_prev, gated_ref, w_in_ref, w_out_ref, w1_ref, w3_ref, w2_ref,
                   stage_in, stage_d, stage_ff, sem_in, sem_d, sem_ff,
                   *, tm, tiles_per_seq, n_tiles, layer, mixer_index):
    step = pl.program_id(0)
    t = lax.rem(jnp.minimum(step, n_tiles - 1), tiles_per_seq)
    n_conv = a_ext.shape[0]
    n_pool = u_ext.shape[0]
    d_conv = n_conv * LANES
    d_ff = w1_ref.shape[1]

    @pl.when(step == 0)
    def _():
        h_prev[...] = jnp.zeros(h_prev.shape, F32)
        a_ext[:, tm:tm + CONV_HALO, :] = jnp.zeros((n_conv, CONV_HALO, LANES), F32)
        u_ext[:, tm:tm + POOL_HALO, :] = jnp.zeros((n_pool, POOL_HALO, LANES), F32)
        _fetch_weights_as_bf16([(w_in_hbm.at[mixer_index], w_in_ref, stage_in, sem_in),
                                (w_out_hbm.at[mixer_index], w_out_ref, stage_d, sem_d),
                                (w2_hbm.at[layer], w2_ref, stage_d, sem_d),
                                (w1_hbm.at[layer], w1_ref, stage_ff, sem_ff),
                                (w3_hbm.at[layer], w3_ref, stage_ff, sem_ff)])

    live = {}

    def ffn_begin():
        live["hp"] = h_prev[...]
        live["nf"] = _rmsnorm(live["hp"], gf_ref[...]).astype(BF16)

    ff_chunks = [(c0, min(c0 + FFN_CHUNK, d_ff)) for c0 in range(0, d_ff, FFN_CHUNK)]
    gate_in, up_in = {}, {}

    def ffn_gate(j):
        c0, c1 = ff_chunks[j]
        gate_in[j] = jnp.dot(live["nf"], w1_ref[:, c0:c1], preferred_element_type=F32)

    def ffn_up(j):
        c0, c1 = ff_chunks[j]
        up_in[j] = jnp.dot(live["nf"], w3_ref[:, c0:c1], preferred_element_type=F32)

    def ffn_act(j):
        c0, c1 = ff_chunks[j]
        a, b = gate_in.pop(j), up_in.pop(j)
        gated_ref[:, c0:c1] = (a * jax.nn.sigmoid(a) * b).astype(BF16)

    def ffn_out():
        o_ref[...] = live["hp"] + jnp.dot(gated_ref[...], w2_ref[...], preferred_element_type=F32)

    ffn_stages = []
    for j in range(len(ff_chunks)):
        ffn_stages += [functools.partial(ffn_gate, j), functools.partial(ffn_up, j), functools.partial(ffn_act, j)]
    ffn_stages.append(ffn_out)

    def mixer_begin():
        x = x_ref[...]
        n = _rmsnorm(x, g_ref[...]).astype(BF16)
        z = jnp.dot(n, w_in_ref[:, :2 * d_conv], preferred_element_type=F32)
        live["x"], live["n"] = x, n

        inside = t > 0
        a_ext[:, 0:CONV_HALO, :] = jnp.where(inside, a_ext[:, tm:tm + CONV_HALO, :], 0.0)
        u_ext[:, 0:POOL_HALO, :] = jnp.where(inside, u_ext[:, tm:tm + POOL_HALO, :], 0.0)

        glu = z[:, :d_conv] * jax.nn.sigmoid(z[:, d_conv:2 * d_conv])
        for s in range(n_conv):
            a_ext[s, CONV_HALO:CONV_HALO + tm, :] = glu[:, s * LANES:(s + 1) * LANES]

    def pool_in():
        u = jnp.dot(live["n"], w_in_ref[:, 2 * d_conv:], preferred_element_type=F32)
        for s in range(n_pool):
            u_ext[s, POOL_HALO:POOL_HALO + tm, :] = u[:, s * LANES:(s + 1) * LANES]

    base = CONV_HALO - (CONV_WIDTH - 1)

    always = step >= 0
    chain = []

    def conv_group(s, starts):
        lanes = slice(s * LANES, (s + 1) * LANES)
        bias = jnp.broadcast_to(dw_b_ref[:, lanes], (SUBLANES, LANES))
        if chain:
            bias = jnp.where(always, bias, chain[-1])
        accs = [bias] * len(starts)
        loaded = {}
        for k in range(CONV_WIDTH):
            wk = jnp.broadcast_to(dw_w_ref[k:k + 1, lanes], (SUBLANES, LANES))
            for i, r in enumerate(starts):
                src = r + base + k
                if src not in loaded:
                    loaded[src] = a_ext[s, _interleaved(src), :]
                accs[i] = accs[i] + loaded[src] * wk
        for i, r in enumerate(starts):
            conv_buf[s, _interleaved(r), :] = accs[i]
        chain.append(accs[-1])

    def pool_group(gi, w):
        for t0 in range(0, tm, TIME_BLOCK):
            loaded = {}
            for m in range(TIME_STRIDE):
                for j in range(w):
                    src = t0 + POOL_HALO + m - j
                    if src not in loaded:
                        loaded[src] = u_ext[gi, _interleaved(src), :]
                tok = loaded[t0 + POOL_HALO + m]
                tot = functools.reduce(jnp.add, [loaded[t0 + POOL_HALO + m - j] for j in range(w)])
                if t0 < w - 1:
                    frame = t * tm + t0 + m + TIME_STRIDE * lax.broadcasted_iota(jnp.int32, (SUBLANES, LANES), 0)
                    mean = tot / jnp.minimum(frame + 1, w).astype(F32)
                elif w & (w - 1) == 0:
                    mean = tot * (1.0 / w)
                else:
                    mean = tot / float(w)
                dlt_buf[gi, _interleaved(t0 + m), :] = mean - tok

    vreg_starts = [t0 + m for t0 in range(0, tm, TIME_BLOCK) for m in range(TIME_STRIDE)]
    mixer_stages = [functools.partial(conv_group, s, vreg_starts[i:i + CONV_ACCS])
                    for s in range(n_conv) for i in range(0, len(vreg_starts), CONV_ACCS)]
    mixer_stages.append(pool_in)
    mixer_stages += [functools.partial(pool_group, gi, w) for gi, w in enumerate(POOL_WINDOWS)]

    def mixer_end():
        cs = [conv_buf[s] for s in range(n_conv)]
        mu = jnp.sum(functools.reduce(jnp.add, cs), axis=-1, keepdims=True) / d_conv
        devs = [c - mu for c in cs]
        var = jnp.sum(functools.reduce(jnp.add, [dv * dv for dv in devs]), axis=-1, keepdims=True) / d_conv
        inv = lax.rsqrt(var + EPS)
        a_out = []
        for s in range(n_conv):
            lanes = slice(s * LANES, (s + 1) * LANES)
            c = devs[s] * inv * ln_g_ref[:, lanes] + ln_b_ref[:, lanes]
            a_out.append((c * jax.nn.sigmoid(c)).astype(BF16))
        pooled = [jnp.dot(dlt_buf[gi].astype(BF16), pool_w_ref[gi], preferred_element_type=F32)
                  for gi in range(n_pool)]
        p_out = (jnp.concatenate(pooled, axis=-1) * pool_s_ref[...]).astype(BF16)
        cat = jnp.concatenate(a_out + [p_out], axis=-1)
        h_prev[...] = live["x"] + jnp.dot(cat, w_out_ref[...], preferred_element_type=F32)

    ffn_begin()
    ffn_stages[0]()
    mixer_begin()
    ffn_rest = ffn_stages[1:]
    per = -(-len(mixer_stages) // len(ffn_rest))
    for i, ffn_stage in enumerate(ffn_rest):
        for stage in mixer_stages[i * per:(i + 1) * per]:
            stage()
        ffn_stage()
    mixer_end()


def _layer0(x2d, g, w_in_all, dw_w, dw_b, ln_g, ln_b, pool_w, pool_s, w_out_all, gf, w1_all, w3_all, w2_all,
            *, seq, layer, mixer_index):
    n_tok, d = x2d.shape
    w_in, w_out, w1, w3, w2 = (jax.ShapeDtypeStruct(w.shape[1:], w.dtype)
                               for w in (w_in_all, w_out_all, w1_all, w3_all, w2_all))
    d_conv = dw_w.shape[1]
    d_pool = pool_s.shape[1]
    d_ff = w1.shape[1]
    tm = TOKEN_TILE
    n_tiles = n_tok // tm
    assert d_pool == len(POOL_WINDOWS) * LANES and d_conv % LANES == 0
    assert tm % TIME_BLOCK == 0 and TIME_STRIDE % CONV_ACCS == 0 and seq % tm == 0 and n_tok % seq == 0
    stages = [_stage_shape(w, LAYER0_STAGE_BYTES) for w in (w_in, w_out, w1)]
    assert stages[1] == _stage_shape(w2, LAYER0_STAGE_BYTES) and stages[2] == _stage_shape(w3, LAYER0_STAGE_BYTES)
    resident = (w_in.size + w_out.size + pool_w.size + 3 * d * d_ff) * 2 + 6 * tm * d * 4 \
        + tm * w_in.shape[1] * 4 + (2 * tm + CONV_HALO) * d_conv * 4 + (2 * tm + POOL_HALO) * d_pool * 4 \
        + 3 * tm * FFN_CHUNK * 4 + 3 * STAGE_SLOTS * LAYER0_STAGE_BYTES
    hbm = pl.BlockSpec(memory_space=pl.ANY)
    return pl.pallas_call(
        functools.partial(_layer0_kernel, tm=tm, tiles_per_seq=seq // tm, n_tiles=n_tiles,
                          layer=layer, mixer_index=mixer_index),
        grid=(n_tiles + 1,),
        in_specs=[
            pl.BlockSpec((tm, d), lambda s: (jnp.minimum(s, n_tiles - 1), 0)),
            _resident((1, d), 1),
            hbm,
            _resident(dw_w.shape, 1),
            _resident((1, d_conv), 1),
            _resident((1, d_conv), 1),
            _resident((1, d_conv), 1),
            _resident(pool_w.shape, 1),
            _resident((1, d_pool), 1),
            hbm,
            _resident((1, d), 1),
            hbm, hbm, hbm,
        ],
        out_specs=pl.BlockSpec((tm, d), lambda s: (jnp.maximum(s - 1, 0), 0)),
        out_shape=jax.ShapeDtypeStruct((n_tok, d), F32),
        scratch_shapes=[
            pltpu.VMEM((d_conv // LANES, CONV_HALO + tm, LANES), F32),
            pltpu.VMEM((d_pool // LANES, POOL_HALO + tm, LANES), F32),
            pltpu.VMEM((d_conv // LANES, tm, LANES), F32),
            pltpu.VMEM((d_pool // LANES, tm, LANES), F32),
            pltpu.VMEM((tm, d), F32),
            pltpu.VMEM((tm, d_ff), BF16),
            pltpu.VMEM(w_in.shape, BF16),
            pltpu.VMEM(w_out.shape, BF16),
            pltpu.VMEM(w1.shape, BF16),
            pltpu.VMEM(w3.shape, BF16),
            pltpu.VMEM(w2.shape, BF16),
            pltpu.VMEM(stages[0], F32),
            pltpu.VMEM(stages[1], F32),
            pltpu.VMEM(stages[2], F32),
            pltpu.SemaphoreType.DMA((STAGE_SLOTS,)),
            pltpu.SemaphoreType.DMA((STAGE_SLOTS,)),
            pltpu.SemaphoreType.DMA((STAGE_SLOTS,)),
        ],
        compiler_params=pltpu.CompilerParams(
            dimension_semantics=("arbitrary",),
            vmem_limit_bytes=_vmem_limit(resident)),
        name="layer0_convpool_ffn",
    )(x2d, g, w_in_all, dw_w, dw_b, ln_g, ln_b, pool_w, pool_s, w_out_all, gf, w1_all, w3_all, w2_all)


def _hgrn_kernel(h_ref, g_ref, w_in_hbm, lb_logits_ref, gn_ref, w_out_hbm, o_ref,
                 z_ref, y_ref, state_ref, w_in_ref, w_out_ref, stage_in, stage_out, sem_in, sem_out,
                 *, tm, layer, mixer_index):
    t = pl.program_id(1)
    d_hg = y_ref.shape[1]
    dk = d_hg // HG_HEADS

    @pl.when((pl.program_id(0) == 0) & (t == 0))
    def _():
        _fetch_weights_as_bf16([(w_in_hbm.at[mixer_index], w_in_ref, stage_in, sem_in),
                                (w_out_hbm.at[mixer_index], w_out_ref, stage_out, sem_out)])

    @pl.when(t == 0)
    def _():
        state_ref[...] = jnp.zeros(state_ref.shape, F32)

    x = h_ref[...]
    n = _rmsnorm(x, g_ref[...]).astype(BF16)
    z_ref[...] = jnp.dot(n, w_in_ref[...], preferred_element_type=F32)

    logits = lb_logits_ref[...]
    e = jnp.exp(logits - jnp.max(logits, axis=0, keepdims=True))
    p = e / jnp.sum(e, axis=0, keepdims=True)
    lb = jnp.sum(p[0:layer + 1], axis=0, keepdims=True) - p[0:1]

    row = lax.broadcasted_iota(jnp.int32, (CHUNK, CHUNK), 0)
    col = lax.broadcasted_iota(jnp.int32, (CHUNK, CHUNK), 1)
    causal = row >= col
    sublane = lax.broadcasted_iota(jnp.int32, (SUBLANES, LANES), 0)
    gn = gn_ref[...]

    def chunk_step(c, carry):
        r0 = pl.multiple_of(c * CHUNK, CHUNK)
        rows = pl.ds(r0, CHUNK)
        q = z_ref[rows, 0:d_hg]
        f = lb + (1.0 - lb) * jax.nn.sigmoid(z_ref[rows, d_hg:2 * d_hg])
        log_f = jnp.log(f)
        k = 1.0 - f
        v = z_ref[rows, 2 * d_hg:3 * d_hg].astype(BF16)
        gate = z_ref[rows, 3 * d_hg:4 * d_hg]
        gate = gate * jax.nn.sigmoid(gate)

        b_tiles = []
        for lt in range(d_hg // LANES):
            running = None
            column = []
            for rt in range(CHUNK // SUBLANES):
                x = log_f[rt * SUBLANES:(rt + 1) * SUBLANES, lt * LANES:(lt + 1) * LANES]
                for s in (1, 2, 4):
                    x = x + jnp.where(sublane >= s, pltpu.roll(x, s, axis=0), 0.0)
                if running is not None:
                    x = x + running
                running = jnp.broadcast_to(x[SUBLANES - 1:SUBLANES, :], (SUBLANES, LANES))
                column.append(x)
            b_tiles.append(jnp.concatenate(column, axis=0))
        b = jnp.concatenate(b_tiles, axis=1)

        b_mid = b[CHUNK // 2 - 1:CHUNK // 2, :]
        b_last = b[CHUNK - 1:CHUNK, :]
        q_rel = q * jnp.exp(b - b_mid)
        k_rel = k * jnp.exp(b_mid - b)
        q_in = (q_rel * jnp.exp(b_mid)).astype(BF16)
        k_up = (k_rel * jnp.exp(b_last - b_mid)).astype(BF16)
        q_rel = q_rel.astype(BF16)
        k_rel = k_rel.astype(BF16)
        decay_last = jnp.exp(b_last)

        heads = [slice(hd * dk, (hd + 1) * dk) for hd in range(HG_HEADS)]
        nt = (((1,), (1,)), ((), ()))
        tn = (((0,), (0,)), ((), ()))
        scores = [lax.dot_general(q_rel[:, sl], k_rel[:, sl], nt, preferred_element_type=F32)
                  for sl in heads]
        state_t = [state_ref[hd] for hd in range(HG_HEADS)]
        o_inter = [lax.dot_general(q_in[:, sl], st.astype(BF16), nt, preferred_element_type=F32)
                   for sl, st in zip(heads, state_t)]
        kv = [lax.dot_general(v[:, sl], k_up[:, sl], tn, preferred_element_type=F32) for sl in heads]
        for hd, sl in enumerate(heads):
            state_ref[hd] = state_t[hd] * decay_last[:, sl] + kv[hd]
        scores = [jnp.where(causal, s, 0.0).astype(BF16) for s in scores]
        outs = [jnp.dot(s, v[:, sl], preferred_element_type=F32) + oi
                for s, sl, oi in zip(scores, heads, o_inter)]
        for o, sl in zip(outs, heads):
            o = o * lax.rsqrt(jnp.mean(o * o, axis=-1, keepdims=True) + EPS) * gn[:, sl]
            y_ref[rows, sl] = (o * gate[:, sl]).astype(BF16)
        return carry

    lax.fori_loop(0, tm // CHUNK, chunk_step, 0, unroll=True)
    o_ref[...] = x + jnp.dot(y_ref[...], w_out_ref[...], preferred_element_type=F32)


def _hgrn_mixer(h, g, w_in_all, lb_logits, gn_g, w_out_all, *, layer, mixer_index):
    bsz, seq, d = h.shape
    w_in, w_out = (jax.ShapeDtypeStruct(w.shape[1:], w.dtype) for w in (w_in_all, w_out_all))
    d_hg = w_out.shape[0]
    dk = d_hg // HG_HEADS
    tm = TOKEN_TILE
    resident = (w_in.size + w_out.size) * 2 + 4 * tm * d * 4 + tm * 4 * d_hg * 4 * 2 \
        + tm * d_hg * 2 + HG_HEADS * dk * dk * 4 + 2 * STAGE_SLOTS * HGRN_STAGE_BYTES
    hbm = pl.BlockSpec(memory_space=pl.ANY)
    return pl.pallas_call(
        functools.partial(_hgrn_kernel, tm=tm, layer=layer, mixer_index=mixer_index),
        grid=(bsz, seq // tm),
        in_specs=[
            pl.BlockSpec((None, tm, d), lambda b, t: (b, t, 0)),
            _resident((1, d), 2),
            hbm,
            _resident(lb_logits.shape, 2),
            _resident((1, d_hg), 2),
            hbm,
        ],
        out_specs=pl.BlockSpec((None, tm, d), lambda b, t: (b, t, 0)),
        out_shape=jax.ShapeDtypeStruct((bsz, seq, d), F32),
        scratch_shapes=[
            pltpu.VMEM((tm, 4 * d_hg), F32),
            pltpu.VMEM((tm, d_hg), BF16),
            pltpu.VMEM((HG_HEADS, dk, dk), F32),
            pltpu.VMEM(w_in.shape, BF16),
            pltpu.VMEM(w_out.shape, BF16),
            pltpu.VMEM(_stage_shape(w_in, HGRN_STAGE_BYTES), F32),
            pltpu.VMEM(_stage_shape(w_out, HGRN_STAGE_BYTES), F32),
            pltpu.SemaphoreType.DMA((STAGE_SLOTS,)),
            pltpu.SemaphoreType.DMA((STAGE_SLOTS,)),
        ],
        compiler_params=pltpu.CompilerParams(
            dimension_semantics=("arbitrary", "arbitrary"),
            vmem_limit_bytes=_vmem_limit(resident)),
        name="hgrn_mixer",
    )(h, g, w_in_all, lb_logits, gn_g, w_out_all)


def kernel(x, norm_mix_g, norm_ffn_g, final_g, cp_w_in, cp_dw_w, cp_dw_b, cp_ln_g, cp_ln_b, cp_pool_w,
           cp_pool_scale, cp_w_out, hg_w_in, hg_lb_logits, hg_gn_g, hg_w_out, ffn_w1, ffn_w3, ffn_w2):
    bsz, seq, d = x.shape
    depth = norm_mix_g.shape[0]
    row = lambda a: a.reshape(1, -1).astype(F32)
    h = x
    for layer in range(depth):
        j = layer // 2
        last = layer == depth - 1
        if layer % 2 == 0:
            assert not last, "the fused conv/pool + FFN layer kernel has no final norm"
            h = _layer0(
                h.reshape(bsz * seq, d), row(norm_mix_g[layer]), cp_w_in, cp_dw_w[j],
                row(cp_dw_b[j]), row(cp_ln_g[j]), row(cp_ln_b[j]), cp_pool_w[j].astype(BF16),
                row(cp_pool_scale[j]), cp_w_out, row(norm_ffn_g[layer]), ffn_w1, ffn_w3, ffn_w2,
                seq=seq, layer=layer, mixer_index=j).reshape(bsz, seq, d)
        else:
            h = _hgrn_mixer(
                h, row(norm_mix_g[layer]), hg_w_in, hg_lb_logits.astype(F32),
                row(hg_gn_g[j]), hg_w_out, layer=layer, mixer_index=j)
            h = _ffn_block(
                h.reshape(bsz * seq, d), row(norm_ffn_g[layer]), ffn_w1, ffn_w3, ffn_w2, row(final_g),
                layer=layer, final_norm=last).reshape(bsz, seq, d)
    return h
```

```python
import functools
import itertools

import jax
import jax.numpy as jnp
from jax import lax
from jax.experimental import pallas as pl
from jax.experimental.pallas import tpu as pltpu

F32 = jnp.float32
BF16 = jnp.bfloat16

EPS = 1e-6
CHUNK = 64
CONV_WIDTH = 31
POOL_WINDOWS = (2, 4, 8, 16)
HG_HEADS = 8

V7X_VMEM_BYTES = 64 * 1024 * 1024
SUBLANES = 8
LANES = 128
VMEM_UNREQUESTED_BYTES = 6 << 20
SPILL_ROOM_FACTOR, SPILL_ROOM_BYTES = 1.5, 8 << 20

TOKEN_TILE = 512
CONV_HALO = 32
POOL_HALO = 16
TIME_STRIDE = 4
TIME_BLOCK = SUBLANES * TIME_STRIDE
CONV_ACCS = 2
FFN_CHUNK = 1024
STAGE_SLOTS = 3
FFN_STAGE_BYTES = 2 << 20
HGRN_STAGE_BYTES = 2 << 20
LAYER0_STAGE_BYTES = 1 << 20


def _interleaved(start):
    return pl.ds(start, SUBLANES, stride=TIME_STRIDE)


def _vmem_limit(resident_bytes):
    return int(min(resident_bytes * SPILL_ROOM_FACTOR + SPILL_ROOM_BYTES, V7X_VMEM_BYTES - VMEM_UNREQUESTED_BYTES))


def _rmsnorm(x, g):
    return x * lax.rsqrt(jnp.mean(x * x, axis=-1, keepdims=True) + EPS) * g


def _resident(shape, ngrid):
    zeros = (0,) * len(shape)
    if ngrid == 1:
        index_map = lambda i: zeros
    else:
        index_map = lambda b, t: zeros
    return pl.BlockSpec(shape, index_map, pipeline_mode=pl.Buffered(1))


def _stage_shape(w, stage_bytes):
    n_rows, n_cols = w.shape
    pack = 2 * SUBLANES
    rows = max(pack, (stage_bytes // (4 * n_cols)) // pack * pack)
    while n_rows % rows:
        rows -= pack
    return (STAGE_SLOTS, rows, n_cols)


def _fetch_weights_as_bf16(jobs):
    queues = {}
    for src, dst, stage, sem in jobs:
        queue = queues.setdefault(id(stage), [])
        rows = stage.shape[1]
        for i in range(dst.shape[0] // rows):
            queue.append((src, dst, stage, sem, i * rows, rows, len(queue) % stage.shape[0]))
    chunks = [c for group in itertools.zip_longest(*queues.values()) for c in group if c is not None]

    def copy(c):
        src, _, stage, sem, r0, rows, slot = c
        return pltpu.make_async_copy(src.at[pl.ds(r0, rows)], stage.at[slot], sem.at[slot])

    starts_after = [[] for _ in chunks]
    last_in_slot = {}
    for k, c in enumerate(chunks):
        key = (id(c[2]), c[6])
        if key in last_in_slot:
            starts_after[last_in_slot[key]].append(k)
        else:
            copy(c).start()
        last_in_slot[key] = k
    for k, c in enumerate(chunks):
        copy(c).wait()
        _, dst, stage, _, r0, rows, slot = c
        dst[pl.ds(r0, rows), :] = stage[slot].astype(BF16)
        for nxt in starts_after[k]:
            copy(chunks[nxt]).start()


def _ffn_kernel(h_ref, g_ref, w1_hbm, w3_hbm, w2_hbm, fg_ref, o_ref,
                w1_ref, w3_ref, w2_ref, stage_in, stage_out, sem_in, sem_out, *, layer, final_norm):
    @pl.when(pl.program_id(0) == 0)
    def _():
        _fetch_weights_as_bf16([(w1_hbm.at[layer], w1_ref, stage_in, sem_in),
                                (w3_hbm.at[layer], w3_ref, stage_in, sem_in),
                                (w2_hbm.at[layer], w2_ref, stage_out, sem_out)])

    half = h_ref.shape[0] // 2
    rows = [slice(0, half), slice(half, 2 * half)]
    h = [h_ref[r, :] for r in rows]
    n = [_rmsnorm(hh, g_ref[...]).astype(BF16) for hh in h]
    ab = [(jnp.dot(nn, w1_ref[...], preferred_element_type=F32),
           jnp.dot(nn, w3_ref[...], preferred_element_type=F32)) for nn in n]
    for r, hh, (a, b) in zip(rows, h, ab):
        gated = (a * jax.nn.sigmoid(a) * b).astype(BF16)
        out = hh + jnp.dot(gated, w2_ref[...], preferred_element_type=F32)
        if final_norm:
            out = _rmsnorm(out, fg_ref[...])
        o_ref[r, :] = out


def _ffn_block(h2d, g, w1_all, w3_all, w2_all, final_g, *, layer, final_norm):
    n_tok, d = h2d.shape
    w1, w3, w2 = (jax.ShapeDtypeStruct(w.shape[1:], w.dtype) for w in (w1_all, w3_all, w2_all))
    d_ff = w1.shape[1]
    tm = TOKEN_TILE
    stages = [_stage_shape(w1, FFN_STAGE_BYTES), _stage_shape(w2, FFN_STAGE_BYTES)]
    assert stages[0] == _stage_shape(w3, FFN_STAGE_BYTES)
    resident = 3 * d * d_ff * 2 + 4 * tm * d * 4 + 3 * tm * d_ff * 4 + 2 * STAGE_SLOTS * FFN_STAGE_BYTES
    hbm = pl.BlockSpec(memory_space=pl.ANY)
    return pl.pallas_call(
        functools.partial(_ffn_kernel, layer=layer, final_norm=final_norm),
        grid=(n_tok // tm,),
        in_specs=[
            pl.BlockSpec((tm, d), lambda i: (i, 0)),
            _resident((1, d), 1),
            hbm, hbm, hbm,
            _resident((1, d), 1),
        ],
        out_specs=pl.BlockSpec((tm, d), lambda i: (i, 0)),
        out_shape=jax.ShapeDtypeStruct((n_tok, d), F32),
        scratch_shapes=[
            pltpu.VMEM(w1.shape, BF16),
            pltpu.VMEM(w3.shape, BF16),
            pltpu.VMEM(w2.shape, BF16),
            pltpu.VMEM(stages[0], F32),
            pltpu.VMEM(stages[1], F32),
            pltpu.SemaphoreType.DMA((STAGE_SLOTS,)),
            pltpu.SemaphoreType.DMA((STAGE_SLOTS,)),
        ],
        compiler_params=pltpu.CompilerParams(
            dimension_semantics=("arbitrary",),
            vmem_limit_bytes=_vmem_limit(resident)),
        name="ffn_block",
    )(h2d, g, w1_all, w3_all, w2_all, final_g)


def _layer0_kernel(x_ref, g_ref, w_in_hbm, dw_w_ref, dw_b_ref, ln_g_ref, ln_b_ref, pool_w_ref, pool_s_ref,
                   w_out_hbm, gf_ref, w1_hbm, w3_hbm, w2_hbm, o_ref,
                   a_ext, u_ext, conv_buf, dlt_buf, h_prev, gated_ref, w_in_ref, w_out_ref, w1_ref, w3_ref, w2_ref,
                   stage_in, stage_d, stage_ff, sem_in, sem_d, sem_ff,
                   *, tm, tiles_per_seq, n_tiles, layer, mixer_index):
    step = pl.program_id(0)
    t = lax.rem(jnp.minimum(step, n_tiles - 1), tiles_per_seq)
    n_conv = a_ext.shape[0]
    n_pool = u_ext.shape[0]
    d_conv = n_conv * LANES
    d_ff = w1_ref.shape[1]

    @pl.when(step == 0)
    def _():
        h_prev[...] = jnp.zeros(h_prev.shape, F32)
        a_ext[:, tm:tm + CONV_HALO, :] = jnp.zeros((n_conv, CONV_HALO, LANES), F32)
        u_ext[:, tm:tm + POOL_HALO, :] = jnp.zeros((n_pool, POOL_HALO, LANES), F32)
        _fetch_weights_as_bf16([(w_in_hbm.at[mixer_index], w_in_ref, stage_in, sem_in),
                                (w_out_hbm.at[mixer_index], w_out_ref, stage_d, sem_d),
                                (w2_hbm.at[layer], w2_ref, stage_d, sem_d),
                                (w1_hbm.at[layer], w1_ref, stage_ff, sem_ff),
                                (w3_hbm.at[layer], w3_ref, stage_ff, sem_ff)])

    live = {}

    def ffn_begin():
        live["hp"] = h_prev[...]
        live["nf"] = _rmsnorm(live["hp"], gf_ref[...]).astype(BF16)

    ff_chunks = [(c0, min(c0 + FFN_CHUNK, d_ff)) for c0 in range(0, d_ff, FFN_CHUNK)]
    gate_in, up_in = {}, {}

    def ffn_gate(j):
        c0, c1 = ff_chunks[j]
        gate_in[j] = jnp.dot(live["nf"], w1_ref[:, c0:c1], preferred_element_type=F32)

    def ffn_up(j):
        c0, c1 = ff_chunks[j]
        up_in[j] = jnp.dot(live["nf"], w3_ref[:, c0:c1], preferred_element_type=F32)

    def ffn_act(j):
        c0, c1 = ff_chunks[j]
        a, b = gate_in.pop(j), up_in.pop(j)
        gated_ref[:, c0:c1] = (a * jax.nn.sigmoid(a) * b).astype(BF16)

    def ffn_out():
        o_ref[...] = live["hp"] + jnp.dot(gated_ref[...], w2_ref[...], preferred_element_type=F32)

    ffn_stages = []
    for j in range(len(ff_chunks)):
        ffn_stages += [functools.partial(ffn_gate, j), functools.partial(ffn_up, j), functools.partial(ffn_act, j)]
    ffn_stages.append(ffn_out)

    def mixer_begin():
        x = x_ref[...]
        n = _rmsnorm(x, g_ref[...]).astype(BF16)
        z = jnp.dot(n, w_in_ref[:, :2 * d_conv], preferred_element_type=F32)
        live["x"], live["n"] = x, n

        inside = t > 0
        a_ext[:, 0:CONV_HALO, :] = jnp.where(inside, a_ext[:, tm:tm + CONV_HALO, :], 0.0)
        u_ext[:, 0:POOL_HALO, :] = jnp.where(inside, u_ext[:, tm:tm + POOL_HALO, :], 0.0)

        glu = z[:, :d_conv] * jax.nn.sigmoid(z[:, d_conv:2 * d_conv])
        for s in range(n_conv):
            a_ext[s, CONV_HALO:CONV_HALO + tm, :] = glu[:, s * LANES:(s + 1) * LANES]

    def pool_in():
        u = jnp.dot(live["n"], w_in_ref[:, 2 * d_conv:], preferred_element_type=F32)
        for s in range(n_pool):
            u_ext[s, POOL_HALO:POOL_HALO + tm, :] = u[:, s * LANES:(s + 1) * LANES]

    base = CONV_HALO - (CONV_WIDTH - 1)

    always = step >= 0
    chain = []

    def conv_group(s, starts):
        lanes = slice(s * LANES, (s + 1) * LANES)
        bias = jnp.broadcast_to(dw_b_ref[:, lanes], (SUBLANES, LANES))
        if chain:
            bias = jnp.where(always, bias, chain[-1])
        accs = [bias] * len(starts)
        loaded = {}
        for k in range(CONV_WIDTH):
            wk = jnp.broadcast_to(dw_w_ref[k:k + 1, lanes], (SUBLANES, LANES))
            for i, r in enumerate(starts):
                src = r + base + k
                if src not in loaded:
                    loaded[src] = a_ext[s, _interleaved(src), :]
                accs[i] = accs[i] + loaded[src] * wk
        for i, r in enumerate(starts):
            conv_buf[s, _interleaved(r), :] = accs[i]
        chain.append(accs[-1])

    def pool_group(gi, w):
        for t0 in range(0, tm, TIME_BLOCK):
            loaded = {}
            for m in range(TIME_STRIDE):
                for j in range(w):
                    src = t0 + POOL_HALO + m - j
                    if src not in loaded:
                        loaded[src] = u_ext[gi, _interleaved(src), :]
                tok = loaded[t0 + POOL_HALO + m]
                tot = functools.reduce(jnp.add, [loaded[t0 + POOL_HALO + m - j] for j in range(w)])
                if t0 < w - 1:
                    frame = t * tm + t0 + m + TIME_STRIDE * lax.broadcasted_iota(jnp.int32, (SUBLANES, LANES), 0)
                    mean = tot / jnp.minimum(frame + 1, w).astype(F32)
                elif w & (w - 1) == 0:
                    mean = tot * (1.0 / w)
                else:
                    mean = tot / float(w)
                dlt_buf[gi, _interleaved(t0 + m), :] = mean - tok

    vreg_starts = [t0 + m for t0 in range(0, tm, TIME_BLOCK) for m in range(TIME_STRIDE)]
    mixer_stages = [functools.partial(conv_group, s, vreg_starts[i:i + CONV_ACCS])
                    for s in range(n_conv) for i in range(0, len(vreg_starts), CONV_ACCS)]
    mixer_stages.append(pool_in)
    mixer_stages += [functools.partial(pool_group, gi, w) for gi, w in enumerate(POOL_WINDOWS)]

    def mixer_end():
        cs = [conv_buf[s] for s in range(n_conv)]
        mu = jnp.sum(functools.reduce(jnp.add, cs), axis=-1, keepdims=True) / d_conv
        devs = [c - mu for c in cs]
        var = jnp.sum(functools.reduce(jnp.add, [dv * dv for dv in devs]), axis=-1, keepdims=True) / d_conv
        inv = lax.rsqrt(var + EPS)
        a_out = []
        for s in range(n_conv):
            lanes = slice(s * LANES, (s + 1) * LANES)
            c = devs[s] * inv * ln_g_ref[:, lanes] + ln_b_ref[:, lanes]
            a_out.append((c * jax.nn.sigmoid(c)).astype(BF16))
        pooled = [jnp.dot(dlt_buf[gi].astype(BF16), pool_w_ref[gi], preferred_element_type=F32)
                  for gi in range(n_pool)]
        p_out = (jnp.concatenate(pooled, axis=-1) * pool_s_ref[...]).astype(BF16)
        cat = jnp.concatenate(a_out + [p_out], axis=-1)
        h_prev[...] = live["x"] + jnp.dot(cat, w_out_ref[...], preferred_element_type=F32)

    ffn_begin()
    ffn_stages[0]()
    mixer_begin()
    ffn_rest = ffn_stages[1:]
    per = -(-len(mixer_stages) // len(ffn_rest))
    for i, ffn_stage in enumerate(ffn_rest):
        for stage in mixer_stages[i * per:(i + 1) * per]:
            stage()
        ffn_stage()
    mixer_end()


def _layer0(x2d, g, w_in_all, dw_w, dw_b, ln_g, ln_b, pool_w, pool_s, w_out_all, gf, w1_all, w3_all, w2_all,
            *, seq, layer, mixer_index):
    n_tok, d = x2d.shape
    w_in, w_out, w1, w3, w2 = (jax.ShapeDtypeStruct(w.shape[1:], w.dtype)
                               for w in (w_in_all, w_out_all, w1_all, w3_all, w2_all))
    d_conv = dw_w.shape[1]
    d_pool = pool_s.shape[1]
    d_ff = w1.shape[1]
    tm = TOKEN_TILE
    n_tiles = n_tok // tm
    assert d_pool == len(POOL_WINDOWS) * LANES and d_conv % LANES == 0
    assert tm % TIME_BLOCK == 0 and TIME_STRIDE % CONV_ACCS == 0 and seq % tm == 0 and n_tok % seq == 0
    stages = [_stage_shape(w, LAYER0_STAGE_BYTES) for w in (w_in, w_out, w1)]
    assert stages[1] == _stage_shape(w2, LAYER0_STAGE_BYTES) and stages[2] == _stage_shape(w3, LAYER0_STAGE_BYTES)
    resident = (w_in.size + w_out.size + pool_w.size + 3 * d * d_ff) * 2 + 6 * tm * d * 4 \
        + tm * w_in.shape[1] * 4 + (2 * tm + CONV_HALO) * d_conv * 4 + (2 * tm + POOL_HALO) * d_pool * 4 \
        + 3 * tm * FFN_CHUNK * 4 + 3 * STAGE_SLOTS * LAYER0_STAGE_BYTES
    hbm = pl.BlockSpec(memory_space=pl.ANY)
    return pl.pallas_call(
        functools.partial(_layer0_kernel, tm=tm, tiles_per_seq=seq // tm, n_tiles=n_tiles,
                          layer=layer, mixer_index=mixer_index),
        grid=(n_tiles + 1,),
        in_specs=[
            pl.BlockSpec((tm, d), lambda s: (jnp.minimum(s, n_tiles - 1), 0)),
            _resident((1, d), 1),
            hbm,
            _resident(dw_w.shape, 1),
            _resident((1, d_conv), 1),
            _resident((1, d_conv), 1),
            _resident((1, d_conv), 1),
            _resident(pool_w.shape, 1),
            _resident((1, d_pool), 1),
            hbm,
            _resident((1, d), 1),
            hbm, hbm, hbm,
        ],
        out_specs=pl.BlockSpec((tm, d), lambda s: (jnp.maximum(s - 1, 0), 0)),
        out_shape=jax.ShapeDtypeStruct((n_tok, d), F32),
        scratch_shapes=[
            pltpu.VMEM((d_conv // LANES, CONV_HALO + tm, LANES), F32),
            pltpu.VMEM((d_pool // LANES, POOL_HALO + tm, LANES), F32),
            pltpu.VMEM((d_conv // LANES, tm, LANES), F32),
            pltpu.VMEM((d_pool // LANES, tm, LANES), F32),
            pltpu.VMEM((tm, d), F32),
            pltpu.VMEM((tm, d_ff), BF16),
            pltpu.VMEM(w_in.shape, BF16),
            pltpu.VMEM(w_out.shape, BF16),
            pltpu.VMEM(w1.shape, BF16),
            pltpu.VMEM(w3.shape, BF16),
            pltpu.VMEM(w2.shape, BF16),
            pltpu.VMEM(stages[0], F32),
            pltpu.VMEM(stages[1], F32),
            pltpu.VMEM(stages[2], F32),
            pltpu.SemaphoreType.DMA((STAGE_SLOTS,)),
            pltpu.SemaphoreType.DMA((STAGE_SLOTS,)),
            pltpu.SemaphoreType.DMA((STAGE_SLOTS,)),
        ],
        compiler_params=pltpu.CompilerParams(
            dimension_semantics=("arbitrary",),
            vmem_limit_bytes=_vmem_limit(resident)),
        name="layer0_convpool_ffn",
    )(x2d, g, w_in_all, dw_w, dw_b, ln_g, ln_b, pool_w, pool_s, w_out_all, gf, w1_all, w3_all, w2_all)


def _hgrn_kernel(h_ref, g_ref, w_in_hbm, lb_logits_ref, gn_ref, w_out_hbm, o_ref,
                 z_ref, y_ref, state_ref, w_in_ref, w_out_ref, stage_in, stage_out, sem_in, sem_out,
                 *, tm, layer, mixer_index):
    t = pl.program_id(1)
    d_hg = y_ref.shape[1]
    dk = d_hg // HG_HEADS

    @pl.when((pl.program_id(0) == 0) & (t == 0))
    def _():
        _fetch_weights_as_bf16([(w_in_hbm.at[mixer_index], w_in_ref, stage_in, sem_in),
                                (w_out_hbm.at[mixer_index], w_out_ref, stage_out, sem_out)])

    @pl.when(t == 0)
    def _():
        state_ref[...] = jnp.zeros(state_ref.shape, F32)

    x = h_ref[...]
    n = _rmsnorm(x, g_ref[...]).astype(BF16)
    z_ref[...] = jnp.dot(n, w_in_ref[...], preferred_element_type=F32)

    logits = lb_logits_ref[...]
    e = jnp.exp(logits - jnp.max(logits, axis=0, keepdims=True))
    p = e / jnp.sum(e, axis=0, keepdims=True)
    lb = jnp.sum(p[0:layer + 1], axis=0, keepdims=True) - p[0:1]

    row = lax.broadcasted_iota(jnp.int32, (CHUNK, CHUNK), 0)
    col = lax.broadcasted_iota(jnp.int32, (CHUNK, CHUNK), 1)
    causal = row >= col
    sublane = lax.broadcasted_iota(jnp.int32, (SUBLANES, LANES), 0)
    gn = gn_ref[...]

    def chunk_step(c, carry):
        r0 = pl.multiple_of(c * CHUNK, CHUNK)
        rows = pl.ds(r0, CHUNK)
        q = z_ref[rows, 0:d_hg]
        f = lb + (1.0 - lb) * jax.nn.sigmoid(z_ref[rows, d_hg:2 * d_hg])
        log_f = jnp.log(f)
        k = 1.0 - f
        v = z_ref[rows, 2 * d_hg:3 * d_hg].astype(BF16)
        gate = z_ref[rows, 3 * d_hg:4 * d_hg]
        gate = gate * jax.nn.sigmoid(gate)

        b_tiles = []
        for lt in range(d_hg // LANES):
            running = None
            column = []
            for rt in range(CHUNK // SUBLANES):
                x = log_f[rt * SUBLANES:(rt + 1) * SUBLANES, lt * LANES:(lt + 1) * LANES]
                for s in (1, 2, 4):
                    x = x + jnp.where(sublane >= s, pltpu.roll(x, s, axis=0), 0.0)
                if running is not None:
                    x = x + running
                running = jnp.broadcast_to(x[SUBLANES - 1:SUBLANES, :], (SUBLANES, LANES))
                column.append(x)
            b_tiles.append(jnp.concatenate(column, axis=0))
        b = jnp.concatenate(b_tiles, axis=1)

        b_mid = b[CHUNK // 2 - 1:CHUNK // 2, :]
        b_last = b[CHUNK - 1:CHUNK, :]
        q_rel = q * jnp.exp(b - b_mid)
        k_rel = k * jnp.exp(b_mid - b)
        q_in = (q_rel * jnp.exp(b_mid)).astype(BF16)
        k_up = (k_rel * jnp.exp(b_last - b_mid)).astype(BF16)
        q_rel = q_rel.astype(BF16)
        k_rel = k_rel.astype(BF16)
        decay_last = jnp.exp(b_last)

        heads = [slice(hd * dk, (hd + 1) * dk) for hd in range(HG_HEADS)]
        nt = (((1,), (1,)), ((), ()))
        tn = (((0,), (0,)), ((), ()))
        scores = [lax.dot_general(q_rel[:, sl], k_rel[:, sl], nt, preferred_element_type=F32)
                  for sl in heads]
        state_t = [state_ref[hd] for hd in range(HG_HEADS)]
        o_inter = [lax.dot_general(q_in[:, sl], st.astype(BF16), nt, preferred_element_type=F32)
                   for sl, st in zip(heads, state_t)]
        kv = [lax.dot_general(v[:, sl], k_up[:, sl], tn, preferred_element_type=F32) for sl in heads]
        for hd, sl in enumerate(heads):
            state_ref[hd] = state_t[hd] * decay_last[:, sl] + kv[hd]
        scores = [jnp.where(causal, s, 0.0).astype(BF16) for s in scores]
        outs = [jnp.dot(s, v[:, sl], preferred_element_type=F32) + oi
                for s, sl, oi in zip(scores, heads, o_inter)]
        for o, sl in zip(outs, heads):
            o = o * lax.rsqrt(jnp.mean(o * o, axis=-1, keepdims=True) + EPS) * gn[:, sl]
            y_ref[rows, sl] = (o * gate[:, sl]).astype(BF16)
        return carry

    lax.fori_loop(0, tm // CHUNK, chunk_step, 0, unroll=True)
    o_ref[...] = x + jnp.dot(y_ref[...], w_out_ref[...], preferred_element_type=F32)


def _hgrn_mixer(h, g, w_in_all, lb_logits, gn_g, w_out_all, *, layer, mixer_index):
    bsz, seq, d = h.shape
    w_in, w_out = (jax.ShapeDtypeStruct(w.shape[1:], w.dtype) for w in (w_in_all, w_out_all))
    d_hg = w_out.shape[0]
    dk = d_hg // HG_HEADS
    tm = TOKEN_TILE
    resident = (w_in.size + w_out.size) * 2 + 4 * tm * d * 4 + tm * 4 * d_hg * 4 * 2 \
        + tm * d_hg * 2 + HG_HEADS * dk * dk * 4 + 2 * STAGE_SLOTS * HGRN_STAGE_BYTES
    hbm = pl.BlockSpec(memory_space=pl.ANY)
    return pl.pallas_call(
        functools.partial(_hgrn_kernel, tm=tm, layer=layer, mixer_index=mixer_index),
        grid=(bsz, seq // tm),
        in_specs=[
            pl.BlockSpec((None, tm, d), lambda b, t: (b, t, 0)),
            _resident((1, d), 2),
            hbm,
            _resident(lb_logits.shape, 2),
            _resident((1, d_hg), 2),
            hbm,
        ],
        out_specs=pl.BlockSpec((None, tm, d), lambda b, t: (b, t, 0)),
        out_shape=jax.ShapeDtypeStruct((bsz, seq, d), F32),
        scratch_shapes=[
            pltpu.VMEM((tm, 4 * d_hg), F32),
            pltpu.VMEM((tm, d_hg), BF16),
            pltpu.VMEM((HG_HEADS, dk, dk), F32),
            pltpu.VMEM(w_in.shape, BF16),
            pltpu.VMEM(w_out.shape, BF16),
            pltpu.VMEM(_stage_shape(w_in, HGRN_STAGE_BYTES), F32),
            pltpu.VMEM(_stage_shape(w_out, HGRN_STAGE_BYTES), F32),
            pltpu.SemaphoreType.DMA((STAGE_SLOTS,)),
            pltpu.SemaphoreType.DMA((STAGE_SLOTS,)),
        ],
        compiler_params=pltpu.CompilerParams(
            dimension_semantics=("arbitrary", "arbitrary"),
            vmem_limit_bytes=_vmem_limit(resident)),
        name="hgrn_mixer",
    )(h, g, w_in_all, lb_logits, gn_g, w_out_all)


def kernel(x, norm_mix_g, norm_ffn_g, final_g, cp_w_in, cp_dw_w, cp_dw_b, cp_ln_g, cp_ln_b, cp_pool_w,
           cp_pool_scale, cp_w_out, hg_w_in, hg_lb_logits, hg_gn_g, hg_w_out, ffn_w1, ffn_w3, ffn_w2):
    bsz, seq, d = x.shape
    depth = norm_mix_g.shape[0]
    row = lambda a: a.reshape(1, -1).astype(F32)
    h = x
    for layer in range(depth):
        j = layer // 2
        last = layer == depth - 1
        if layer % 2 == 0:
            assert not last, "the fused conv/pool + FFN layer kernel has no final norm"
            h = _layer0(
                h.reshape(bsz * seq, d), row(norm_mix_g[layer]), cp_w_in, cp_dw_w[j],
                row(cp_dw_b[j]), row(cp_ln_g[j]), row(cp_ln_b[j]), cp_pool_w[j].astype(BF16),
                row(cp_pool_scale[j]), cp_w_out, row(norm_ffn_g[layer]), ffn_w1, ffn_w3, ffn_w2,
                seq=seq, layer=layer, mixer_index=j).reshape(bsz, seq, d)
        else:
            h = _hgrn_mixer(
                h, row(norm_mix_g[layer]), hg_w_in, hg_lb_logits.astype(F32),
                row(hg_gn_g[j]), hg_w_out, layer=layer, mixer_index=j)
            h = _ffn_block(
                h.reshape(bsz * seq, d), row(norm_ffn_g[layer]), ffn_w1, ffn_w3, ffn_w2, row(final_g),
                layer=layer, final_norm=last).reshape(bsz, seq, d)
    return h
```

```python
import functools
import itertools

import jax
import jax.numpy as jnp
from jax import lax
from jax.experimental import pallas as pl
from jax.experimental.pallas import tpu as pltpu

F32 = jnp.float32
BF16 = jnp.bfloat16

EPS = 1e-6
CHUNK = 64
CONV_WIDTH = 31
POOL_WINDOWS = (2, 4, 8, 16)
HG_HEADS = 8

V7X_VMEM_BYTES = 64 * 1024 * 1024
SUBLANES = 8
LANES = 128
VMEM_UNREQUESTED_BYTES = 6 << 20
SPILL_ROOM_FACTOR, SPILL_ROOM_BYTES = 1.5, 8 << 20

TOKEN_TILE = 512
CONV_HALO = 32
POOL_HALO = 16
TIME_STRIDE = 4
TIME_BLOCK = SUBLANES * TIME_STRIDE
CONV_ACCS = 2
FFN_CHUNK = 1024
STAGE_SLOTS = 3
FFN_STAGE_BYTES = 2 << 20
HGRN_STAGE_BYTES = 2 << 20
LAYER0_STAGE_BYTES = 1 << 20


def _interleaved(start):
    return pl.ds(start, SUBLANES, stride=TIME_STRIDE)


def _vmem_limit(resident_bytes):
    return int(min(resident_bytes * SPILL_ROOM_FACTOR + SPILL_ROOM_BYTES, V7X_VMEM_BYTES - VMEM_UNREQUESTED_BYTES))


def _rmsnorm(x, g):
    return x * lax.rsqrt(jnp.mean(x * x, axis=-1, keepdims=True) + EPS) * g


def _resident(shape, ngrid):
    zeros = (0,) * len(shape)
    if ngrid == 1:
        index_map = lambda i: zeros
    else:
        index_map = lambda b, t: zeros
    return pl.BlockSpec(shape, index_map, pipeline_mode=pl.Buffered(1))


def _stage_shape(w, stage_bytes):
    n_rows, n_cols = w.shape
    pack = 2 * SUBLANES
    rows = max(pack, (stage_bytes // (4 * n_cols)) // pack * pack)
    while n_rows % rows:
        rows -= pack
    return (STAGE_SLOTS, rows, n_cols)


def _fetch_weights_as_bf16(jobs):
    queues = {}
    for src, dst, stage, sem in jobs:
        queue = queues.setdefault(id(stage), [])
        rows = stage.shape[1]
        for i in range(dst.shape[0] // rows):
            queue.append((src, dst, stage, sem, i * rows, rows, len(queue) % stage.shape[0]))
    chunks = [c for group in itertools.zip_longest(*queues.values()) for c in group if c is not None]

    def copy(c):
        src, _, stage, sem, r0, rows, slot = c
        return pltpu.make_async_copy(src.at[pl.ds(r0, rows)], stage.at[slot], sem.at[slot])

    starts_after = [[] for _ in chunks]
    last_in_slot = {}
    for k, c in enumerate(chunks):
        key = (id(c[2]), c[6])
        if key in last_in_slot:
            starts_after[last_in_slot[key]].append(k)
        else:
            copy(c).start()
        last_in_slot[key] = k
    for k, c in enumerate(chunks):
        copy(c).wait()
        _, dst, stage, _, r0, rows, slot = c
        dst[pl.ds(r0, rows), :] = stage[slot].astype(BF16)
        for nxt in starts_after[k]:
            copy(chunks[nxt]).start()


def _ffn_kernel(h_ref, g_ref, w1_hbm, w3_hbm, w2_hbm, fg_ref, o_ref,
                w1_ref, w3_ref, w2_ref, stage_in, stage_out, sem_in, sem_out, *, layer, final_norm):
    @pl.when(pl.program_id(0) == 0)
    def _():
        _fetch_weights_as_bf16([(w1_hbm.at[layer], w1_ref, stage_in, sem_in),
                                (w3_hbm.at[layer], w3_ref, stage_in, sem_in),
                                (w2_hbm.at[layer], w2_ref, stage_out, sem_out)])

    half = h_ref.shape[0] // 2
    rows = [slice(0, half), slice(half, 2 * half)]
    h = [h_ref[r, :] for r in rows]
    n = [_rmsnorm(hh, g_ref[...]).astype(BF16) for hh in h]
    ab = [(jnp.dot(nn, w1_ref[...], preferred_element_type=F32),
           jnp.dot(nn, w3_ref[...], preferred_element_type=F32)) for nn in n]
    for r, hh, (a, b) in zip(rows, h, ab):
        gated = (a * jax.nn.sigmoid(a) * b).astype(BF16)
        out = hh + jnp.dot(gated, w2_ref[...], preferred_element_type=F32)
        if final_norm:
            out = _rmsnorm(out, fg_ref[...])
        o_ref[r, :] = out


def _ffn_block(h2d, g, w1_all, w3_all, w2_all, final_g, *, layer, final_norm):
    n_tok, d = h2d.shape
    w1, w3, w2 = (jax.ShapeDtypeStruct(w.shape[1:], w.dtype) for w in (w1_all, w3_all, w2_all))
    d_ff = w1.shape[1]
    tm = TOKEN_TILE
    stages = [_stage_shape(w1, FFN_STAGE_BYTES), _stage_shape(w2, FFN_STAGE_BYTES)]
    assert stages[0] == _stage_shape(w3, FFN_STAGE_BYTES)
    resident = 3 * d * d_ff * 2 + 4 * tm * d * 4 + 3 * tm * d_ff * 4 + 2 * STAGE_SLOTS * FFN_STAGE_BYTES
    hbm = pl.BlockSpec(memory_space=pl.ANY)
    return pl.pallas_call(
        functools.partial(_ffn_kernel, layer=layer, final_norm=final_norm),
        grid=(n_tok // tm,),
        in_specs=[
            pl.BlockSpec((tm, d), lambda i: (i, 0)),
            _resident((1, d), 1),
            hbm, hbm, hbm,
            _resident((1, d), 1),
        ],
        out_specs=pl.BlockSpec((tm, d), lambda i: (i, 0)),
        out_shape=jax.ShapeDtypeStruct((n_tok, d), F32),
        scratch_shapes=[
            pltpu.VMEM(w1.shape, BF16),
            pltpu.VMEM(w3.shape, BF16),
            pltpu.VMEM(w2.shape, BF16),
            pltpu.VMEM(stages[0], F32),
            pltpu.VMEM(stages[1], F32),
            pltpu.SemaphoreType.DMA((STAGE_SLOTS,)),
            pltpu.SemaphoreType.DMA((STAGE_SLOTS,)),
        ],
        compiler_params=pltpu.CompilerParams(
            dimension_semantics=("arbitrary",),
            vmem_limit_bytes=_vmem_limit(resident)),
        name="ffn_block",
    )(h2d, g, w1_all, w3_all, w2_all, final_g)


def _layer0_kernel(x_ref, g_ref, w_in_hbm, dw_w_ref, dw_b_ref, ln_g_ref, ln_b_ref, pool_w_ref, pool_s_ref,
                   w_out_hbm, gf_ref, w1_hbm, w3_hbm, w2_hbm, o_ref,
                   a_ext, u_ext, conv_buf, dlt_buf, h_prev, gated_ref, w_in_ref, w_out_ref, w1_ref, w3_ref, w2_ref,
                   stage_in, stage_d, stage_ff, sem_in, sem_d, sem_ff,
                   *, tm, tiles_per_seq, n_tiles, layer, mixer_index):
    step = pl.program_id(0)
    t = lax.rem(jnp.minimum(step, n_tiles - 1), tiles_per_seq)
    n_conv = a_ext.shape[0]
    n_pool = u_ext.shape[0]
    d_conv = n_conv * LANES
    d_ff = w1_ref.shape[1]

    @pl.when(step == 0)
    def _():
        h_prev[...] = jnp.zeros(h_prev.shape, F32)
        a_ext[:, tm:tm + CONV_HALO, :] = jnp.zeros((n_conv, CONV_HALO, LANES), F32)
        u_ext[:, tm:tm + POOL_HALO, :] = jnp.zeros((n_pool, POOL_HALO, LANES), F32)
        _fetch_weights_as_bf16([(w_in_hbm.at[mixer_index], w_in_ref, stage_in, sem_in),
                                (w_out_hbm.at[mixer_index], w_out_ref, stage_d, sem_d),
                                (w2_hbm.at[layer], w2_ref, stage_d, sem_d),
                                (w1_hbm.at[layer], w1_ref, stage_ff, sem_ff),
                                (w3_hbm.at[layer], w3_ref, stage_ff, sem_ff)])

    live = {}

    def ffn_begin():
        live["hp"] = h_prev[...]
        live["nf"] = _rmsnorm(live["hp"], gf_ref[...]).astype(BF16)

    ff_chunks = [(c0, min(c0 + FFN_CHUNK, d_ff)) for c0 in range(0, d_ff, FFN_CHUNK)]
    gate_in, up_in = {}, {}

    def ffn_gate(j):
        c0, c1 = ff_chunks[j]
        gate_in[j] = jnp.dot(live["nf"], w1_ref[:, c0:c1], preferred_element_type=F32)

    def ffn_up(j):
        c0, c1 = ff_chunks[j]
        up_in[j] = jnp.dot(live["nf"], w3_ref[:, c0:c1], preferred_element_type=F32)

    def ffn_act(j):
        c0, c1 = ff_chunks[j]
        a, b = gate_in.pop(j), up_in.pop(j)
        gated_ref[:, c0:c1] = (a * jax.nn.sigmoid(a) * b).astype(BF16)

    def ffn_out():
        o_ref[...] = live["hp"] + jnp.dot(gated_ref[...], w2_ref[...], preferred_element_type=F32)

    ffn_stages = []
    for j in range(len(ff_chunks)):
        ffn_stages += [functools.partial(ffn_gate, j), functools.partial(ffn_up, j), functools.partial(ffn_act, j)]
    ffn_stages.append(ffn_out)

    def mixer_begin():
        x = x_ref[...]
        n = _rmsnorm(x, g_ref[...]).astype(BF16)
        z = jnp.dot(n, w_in_ref[:, :2 * d_conv], preferred_element_type=F32)
        live["x"], live["n"] = x, n

        inside = t > 0
        a_ext[:, 0:CONV_HALO, :] = jnp.where(inside, a_ext[:, tm:tm + CONV_HALO, :], 0.0)
        u_ext[:, 0:POOL_HALO, :] = jnp.where(inside, u_ext[:, tm:tm + POOL_HALO, :], 0.0)

        glu = z[:, :d_conv] * jax.nn.sigmoid(z[:, d_conv:2 * d_conv])
        for s in range(n_conv):
            a_ext[s, CONV_HALO:CONV_HALO + tm, :] = glu[:, s * LANES:(s + 1) * LANES]

    def pool_in():
        u = jnp.dot(live["n"], w_in_ref[:, 2 * d_conv:], preferred_element_type=F32)
        for s in range(n_pool):
            u_ext[s, POOL_HALO:POOL_HALO + tm, :] = u[:, s * LANES:(s + 1) * LANES]

    base = CONV_HALO - (CONV_WIDTH - 1)

    always = step >= 0
    chain = []

    def conv_group(s, starts):
        lanes = slice(s * LANES, (s + 1) * LANES)
        bias = jnp.broadcast_to(dw_b_ref[:, lanes], (SUBLANES, LANES))
        if chain:
            bias = jnp.where(always, bias, chain[-1])
        accs = [bias] * len(starts)
        loaded = {}
        for k in range(CONV_WIDTH):
            wk = jnp.broadcast_to(dw_w_ref[k:k + 1, lanes], (SUBLANES, LANES))
            for i, r in enumerate(starts):
                src = r + base + k
                if src not in loaded:
                    loaded[src] = a_ext[s, _interleaved(src), :]
                accs[i] = accs[i] + loaded[src] * wk
        for i, r in enumerate(starts):
            conv_buf[s, _interleaved(r), :] = accs[i]
        chain.append(accs[-1])

    def pool_group(gi, w):
        for t0 in range(0, tm, TIME_BLOCK):
            loaded = {}
            for m in range(TIME_STRIDE):
                for j in range(w):
                    src = t0 + POOL_HALO + m - j
                    if src not in loaded:
                        loaded[src] = u_ext[gi, _interleaved(src), :]
                tok = loaded[t0 + POOL_HALO + m]
                tot = functools.reduce(jnp.add, [loaded[t0 + POOL_HALO + m - j] for j in range(w)])
                if t0 < w - 1:
                    frame = t * tm + t0 + m + TIME_STRIDE * lax.broadcasted_iota(jnp.int32, (SUBLANES, LANES), 0)
                    mean = tot / jnp.minimum(frame + 1, w).astype(F32)
                elif w & (w - 1) == 0:
                    mean = tot * (1.0 / w)
                else:
                    mean = tot / float(w)
                dlt_buf[gi, _interleaved(t0 + m), :] = mean - tok

    vreg_starts = [t0 + m for t0 in range(0, tm, TIME_BLOCK) for m in range(TIME_STRIDE)]
    mixer_stages = [functools.partial(conv_group, s, vreg_starts[i:i + CONV_ACCS])
                    for s in range(n_conv) for i in range(0, len(vreg_starts), CONV_ACCS)]
    mixer_stages.append(pool_in)
    mixer_stages += [functools.partial(pool_group, gi, w) for gi, w in enumerate(POOL_WINDOWS)]

    def mixer_end():
        cs = [conv_buf[s] for s in range(n_conv)]
        mu = jnp.sum(functools.reduce(jnp.add, cs), axis=-1, keepdims=True) / d_conv
        devs = [c - mu for c in cs]
        var = jnp.sum(functools.reduce(jnp.add, [dv * dv for dv in devs]), axis=-1, keepdims=True) / d_conv
        inv = lax.rsqrt(var + EPS)
        a_out = []
        for s in range(n_conv):
            lanes = slice(s * LANES, (s + 1) * LANES)
            c = devs[s] * inv * ln_g_ref[:, lanes] + ln_b_ref[:, lanes]
            a_out.append((c * jax.nn.sigmoid(c)).astype(BF16))
        pooled = [jnp.dot(dlt_buf[gi].astype(BF16), pool_w_ref[gi], preferred_element_type=F32)
                  for gi in range(n_pool)]
        p_out = (jnp.concatenate(pooled, axis=-1) * pool_s_ref[...]).astype(BF16)
        cat = jnp.concatenate(a_out + [p_out], axis=-1)
        h_prev[...] = live["x"] + jnp.dot(cat, w_out_ref[...], preferred_element_type=F32)

    ffn_begin()
    ffn_stages[0]()
    mixer_begin()
    ffn_rest = ffn_stages[1:]
    per = -(-len(mixer_stages) // len(ffn_rest))
    for i, ffn_stage in enumerate(ffn_rest):
        for stage in mixer_stages[i * per:(i + 1) * per]:
            stage()
        ffn_stage()
    mixer_end()


def _layer0(x2d, g, w_in_all, dw_w, dw_b, ln_g, ln_b, pool_w, pool_s, w_out_all, gf, w1_all, w3_all, w2_all,
            *, seq, layer, mixer_index):
    n_tok, d = x2d.shape
    w_in, w_out, w1, w3, w2 = (jax.ShapeDtypeStruct(w.shape[1:], w.dtype)
                               for w in (w_in_all, w_out_all, w1_all, w3_all, w2_all))
    d_conv = dw_w.shape[1]
    d_pool = pool_s.shape[1]
    d_ff = w1.shape[1]
    tm = TOKEN_TILE
    n_tiles = n_tok // tm
    assert d_pool == len(POOL_WINDOWS) * LANES and d_conv % LANES == 0
    assert tm % TIME_BLOCK == 0 and TIME_STRIDE % CONV_ACCS == 0 and seq % tm == 0 and n_tok % seq == 0
    stages = [_stage_shape(w, LAYER0_STAGE_BYTES) for w in (w_in, w_out, w1)]
    assert stages[1] == _stage_shape(w2, LAYER0_STAGE_BYTES) and stages[2] == _stage_shape(w3, LAYER0_STAGE_BYTES)
    resident = (w_in.size + w_out.size + pool_w.size + 3 * d * d_ff) * 2 + 6 * tm * d * 4 \
        + tm * w_in.shape[1] * 4 + (2 * tm + CONV_HALO) * d_conv * 4 + (2 * tm + POOL_HALO) * d_pool * 4 \
        + 3 * tm * FFN_CHUNK * 4 + 3 * STAGE_SLOTS * LAYER0_STAGE_BYTES
    hbm = pl.BlockSpec(memory_space=pl.ANY)
    return pl.pallas_call(
        functools.partial(_layer0_kernel, tm=tm, tiles_per_seq=seq // tm, n_tiles=n_tiles,
                          layer=layer, mixer_index=mixer_index),
        grid=(n_tiles + 1,),
        in_specs=[
            pl.BlockSpec((tm, d), lambda s: (jnp.minimum(s, n_tiles - 1), 0)),
            _resident((1, d), 1),
            hbm,
            _resident(dw_w.shape, 1),
            _resident((1, d_conv), 1),
            _resident((1, d_conv), 1),
            _resident((1, d_conv), 1),
            _resident(pool_w.shape, 1),
            _resident((1, d_pool), 1),
            hbm,
            _resident((1, d), 1),
            hbm, hbm, hbm,
        ],
        out_specs=pl.BlockSpec((tm, d), lambda s: (jnp.maximum(s - 1, 0), 0)),
        out_shape=jax.ShapeDtypeStruct((n_tok, d), F32),
        scratch_shapes=[
            pltpu.VMEM((d_conv // LANES, CONV_HALO + tm, LANES), F32),
            pltpu.VMEM((d_pool // LANES, POOL_HALO + tm, LANES), F32),
            pltpu.VMEM((d_conv // LANES, tm, LANES), F32),
            pltpu.VMEM((d_pool // LANES, tm, LANES), F32),
            pltpu.VMEM((tm, d), F32),
            pltpu.VMEM((tm, d_ff), BF16),
            pltpu.VMEM(w_in.shape, BF16),
            pltpu.VMEM(w_out.shape, BF16),
            pltpu.VMEM(w1.shape, BF16),
            pltpu.VMEM(w3.shape, BF16),
            pltpu.VMEM(w2.shape, BF16),
            pltpu.VMEM(stages[0], F32),
            pltpu.VMEM(stages[1], F32),
            pltpu.VMEM(stages[2], F32),
            pltpu.SemaphoreType.DMA((STAGE_SLOTS,)),
            pltpu.SemaphoreType.DMA((STAGE_SLOTS,)),
            pltpu.SemaphoreType.DMA((STAGE_SLOTS,)),
        ],
        compiler_params=pltpu.CompilerParams(
            dimension_semantics=("arbitrary",),
            vmem_limit_bytes=_vmem_limit(resident)),
        name="layer0_convpool_ffn",
    )(x2d, g, w_in_all, dw_w, dw_b, ln_g, ln_b, pool_w, pool_s, w_out_all, gf, w1_all, w3_all, w2_all)


def _hgrn_kernel(h_ref, g_ref, w_in_hbm, lb_logits_ref, gn_ref, w_out_hbm, o_ref,
                 z_ref, y_ref, state_ref, w_in_ref, w_out_ref, stage_in, stage_out, sem_in, sem_out,
                 *, tm, layer, mixer_index):
    t = pl.program_id(1)
    d_hg = y_ref.shape[1]
    dk = d_hg // HG_HEADS

    @pl.when((pl.program_id(0) == 0) & (t == 0))
    def _():
        _fetch_weights_as_bf16([(w_in_hbm.at[mixer_index], w_in_ref, stage_in, sem_in),
                                (w_out_hbm.at[mixer_index], w_out_ref, stage_out, sem_out)])

    @pl.when(t == 0)
    def _():
        state_ref[...] = jnp.zeros(state_ref.shape, F32)

    x = h_ref[...]
    n = _rmsnorm(x, g_ref[...]).astype(BF16)
    z_ref[...] = jnp.dot(n, w_in_ref[...], preferred_element_type=F32)

    logits = lb_logits_ref[...]
    e = jnp.exp(logits - jnp.max(logits, axis=0, keepdims=True))
    p = e / jnp.sum(e, axis=0, keepdims=True)
    lb = jnp.sum(p[0:layer + 1], axis=0, keepdims=True) - p[0:1]

    row = lax.broadcasted_iota(jnp.int32, (CHUNK, CHUNK), 0)
    col = lax.broadcasted_iota(jnp.int32, (CHUNK, CHUNK), 1)
    causal = row >= col
    sublane = lax.broadcasted_iota(jnp.int32, (SUBLANES, LANES), 0)
    gn = gn_ref[...]

    def chunk_step(c, carry):
        r0 = pl.multiple_of(c * CHUNK, CHUNK)
        rows = pl.ds(r0, CHUNK)
        q = z_ref[rows, 0:d_hg]
        f = lb + (1.0 - lb) * jax.nn.sigmoid(z_ref[rows, d_hg:2 * d_hg])
        log_f = jnp.log(f)
        k = 1.0 - f
        v = z_ref[rows, 2 * d_hg:3 * d_hg].astype(BF16)
        gate = z_ref[rows, 3 * d_hg:4 * d_hg]
        gate = gate * jax.nn.sigmoid(gate)

        b_tiles = []
        for lt in range(d_hg // LANES):
            running = None
            column = []
            for rt in range(CHUNK // SUBLANES):
                x = log_f[rt * SUBLANES:(rt + 1) * SUBLANES, lt * LANES:(lt + 1) * LANES]
                for s in (1, 2, 4):
                    x = x + jnp.where(sublane >= s, pltpu.roll(x, s, axis=0), 0.0)
                if running is not None:
                    x = x + running
                running = jnp.broadcast_to(x[SUBLANES - 1:SUBLANES, :], (SUBLANES, LANES))
                column.append(x)
            b_tiles.append(jnp.concatenate(column, axis=0))
        b = jnp.concatenate(b_tiles, axis=1)

        b_mid = b[CHUNK // 2 - 1:CHUNK // 2, :]
        b_last = b[CHUNK - 1:CHUNK, :]
        q_rel = q * jnp.exp(b - b_mid)
        k_rel = k * jnp.exp(b_mid - b)
        q_in = (q_rel * jnp.exp(b_mid)).astype(BF16)
        k_up = (k_rel * jnp.exp(b_last - b_mid)).astype(BF16)
        q_rel = q_rel.astype(BF16)
        k_rel = k_rel.astype(BF16)
        decay_last = jnp.exp(b_last)

        heads = [slice(hd * dk, (hd + 1) * dk) for hd in range(HG_HEADS)]
        nt = (((1,), (1,)), ((), ()))
        tn = (((0,), (0,)), ((), ()))
        scores = [lax.dot_general(q_rel[:, sl], k_rel[:, sl], nt, preferred_element_type=F32)
                  for sl in heads]
        state_t = [state_ref[hd] for hd in range(HG_HEADS)]
        o_inter = [lax.dot_general(q_in[:, sl], st.astype(BF16), nt, preferred_element_type=F32)
                   for sl, st in zip(heads, state_t)]
        scores = [jnp.where(causal, s, 0.0).astype(BF16) for s in scores]
        outs = [jnp.dot(s, v[:, sl], preferred_element_type=F32) + oi
                for s, sl, oi in zip(scores, heads, o_inter)]
        kv = [lax.dot_general(v[:, sl], k_up[:, sl], tn, preferred_element_type=F32) for sl in heads]
        for hd, sl in enumerate(heads):
            state_ref[hd] = state_t[hd] * decay_last[:, sl] + kv[hd]
        for o, sl in zip(outs, heads):
            o = o * lax.rsqrt(jnp.mean(o * o, axis=-1, keepdims=True) + EPS) * gn[:, sl]
            y_ref[rows, sl] = (o * gate[:, sl]).astype(BF16)
        return carry

    lax.fori_loop(0, tm // CHUNK, chunk_step, 0, unroll=True)
    o_ref[...] = x + jnp.dot(y_ref[...], w_out_ref[...], preferred_element_type=F32)


def _hgrn_mixer(h, g, w_in_all, lb_logits, gn_g, w_out_all, *, layer, mixer_index):
    bsz, seq, d = h.shape
    w_in, w_out = (jax.ShapeDtypeStruct(w.shape[1:], w.dtype) for w in (w_in_all, w_out_all))
    d_hg = w_out.shape[0]
    dk = d_hg // HG_HEADS
    tm = TOKEN_TILE
    resident = (w_in.size + w_out.size) * 2 + 4 * tm * d * 4 + tm * 4 * d_hg * 4 * 2 \
        + tm * d_hg * 2 + HG_HEADS * dk * dk * 4 + 2 * STAGE_SLOTS * HGRN_STAGE_BYTES
    hbm = pl.BlockSpec(memory_space=pl.ANY)
    return pl.pallas_call(
        functools.partial(_hgrn_kernel, tm=tm, layer=layer, mixer_index=mixer_index),
        grid=(bsz, seq // tm),
        in_specs=[
            pl.BlockSpec((None, tm, d), lambda b, t: (b, t, 0)),
            _resident((1, d), 2),
            hbm,
            _resident(lb_logits.shape, 2),
            _resident((1, d_hg), 2),
            hbm,
        ],
        out_specs=pl.BlockSpec((None, tm, d), lambda b, t: (b, t, 0)),
        out_shape=jax.ShapeDtypeStruct((bsz, seq, d), F32),
        scratch_shapes=[
            pltpu.VMEM((tm, 4 * d_hg), F32),
            pltpu.VMEM((tm, d_hg), BF16),
            pltpu.VMEM((HG_HEADS, dk, dk), F32),
            pltpu.VMEM(w_in.shape, BF16),
            pltpu.VMEM(w_out.shape, BF16),
            pltpu.VMEM(_stage_shape(w_in, HGRN_STAGE_BYTES), F32),
            pltpu.VMEM(_stage_shape(w_out, HGRN_STAGE_BYTES), F32),
            pltpu.SemaphoreType.DMA((STAGE_SLOTS,)),
            pltpu.SemaphoreType.DMA((STAGE_SLOTS,)),
        ],
        compiler_params=pltpu.CompilerParams(
            dimension_semantics=("arbitrary", "arbitrary"),
            vmem_limit_bytes=_vmem_limit(resident)),
        name="hgrn_mixer",
    )(h, g, w_in_all, lb_logits, gn_g, w_out_all)


def kernel(x, norm_mix_g, norm_ffn_g, final_g, cp_w_in, cp_dw_w, cp_dw_b, cp_ln_g, cp_ln_b, cp_pool_w,
           cp_pool_scale, cp_w_out, hg_w_in, hg_lb_logits, hg_gn_g, hg_w_out, ffn_w1, ffn_w3, ffn_w2):
    bsz, seq, d = x.shape
    depth = norm_mix_g.shape[0]
    row = lambda a: a.reshape(1, -1).astype(F32)
    h = x
    for layer in range(depth):
        j = layer // 2
        last = layer == depth - 1
        if layer % 2 == 0:
            assert not last, "the fused conv/pool + FFN layer kernel has no final norm"
            h = _layer0(
                h.reshape(bsz * seq, d), row(norm_mix_g[layer]), cp_w_in, cp_dw_w[j],
                row(cp_dw_b[j]), row(cp_ln_g[j]), row(cp_ln_b[j]), cp_pool_w[j].astype(BF16),
                row(cp_pool_scale[j]), cp_w_out, row(norm_ffn_g[layer]), ffn_w1, ffn_w3, ffn_w2,
                seq=seq, layer=layer, mixer_index=j).reshape(bsz, seq, d)
        else:
            h = _hgrn_mixer(
                h, row(norm_mix_g[layer]), hg_w_in, hg_lb_logits.astype(F32),
                row(hg_gn_g[j]), hg_w_out, layer=layer, mixer_index=j)
            h = _ffn_block(
                h.reshape(bsz * seq, d), row(norm_ffn_g[layer]), ffn_w1, ffn_w3, ffn_w2, row(final_g),
                layer=layer, final_norm=last).reshape(bsz, seq, d)
    return h
```
